```python
import math
import jax, jax.numpy as jnp
from jax import lax
import numpy as np

D_MODEL = 2048
BATCH = 2
SEQ = 16384
DEPTH = 1
DEC_BATCH = 16
DEC_SEQ = 64
PAST_LEN = 2048

CHUNK = 64
MLP_WIDTH = D_MODEL // 2
ATTN_WIDTH = D_MODEL - MLP_WIDTH
MLP_GROUPS = 8
MLP_GROUP_DIM = MLP_WIDTH // MLP_GROUPS
MLP_CHUNK = 128
N_HEADS = 8
HEAD_DIM = ATTN_WIDTH // (2 * N_HEADS)
V_HEAD_DIM = 2 * HEAD_DIM
QK_WIDTH = N_HEADS * 2 * HEAD_DIM
V_WIDTH = N_HEADS * V_HEAD_DIM
IN_WIDTH = 2 * MLP_WIDTH + 2 * QK_WIDTH + V_WIDTH
D_FF = ((8 * D_MODEL // 3 + 255) // 256) * 256
N_BUCKETS = 32
MAX_DISTANCE = 128
Q_BLOCK = 128
EPS = 1e-6

kernel_name = 'hybrid_gmlp_diffattn_stream_step'


def rmsnorm(x, g):
    xf = x.astype(jnp.float32)
    y = xf * lax.rsqrt(jnp.mean(xf * xf, axis=-1, keepdims=True) + EPS)
    return (y * g.astype(jnp.float32)).astype(x.dtype)


def layernorm(x, g, b):
    xf = x.astype(jnp.float32)
    mu = jnp.mean(xf, axis=-1, keepdims=True)
    var = jnp.mean(jnp.square(xf - mu), axis=-1, keepdims=True)
    y = (xf - mu) * lax.rsqrt(var + EPS)
    return (y * g.astype(jnp.float32) + b.astype(jnp.float32)).astype(x.dtype)


def modulate(h, shift, scale):
    return h * (1 + scale[:, None, :]) + shift[:, None, :]


def adaln(c, w, b, n):
    m = jnp.einsum('bd,de->be', jax.nn.silu(c), w) + b
    return jnp.split(m, n, axis=-1)


def t5_bucket(rel):
    nb = N_BUCKETS // 2
    max_exact = nb // 2
    ret = jnp.where(rel > 0, nb, 0)
    n = jnp.abs(rel)
    nf = jnp.maximum(n, 1).astype(jnp.float32)
    large = max_exact + (jnp.log(nf / max_exact) / math.log(MAX_DISTANCE / max_exact)
                         * (nb - max_exact)).astype(jnp.int32)
    large = jnp.minimum(large, nb - 1)
    return ret + jnp.where(n < max_exact, n, large)


def diff_attention(q, k, v, q_pos, k_pos, lam, rel_bias):
    bias = jnp.transpose(rel_bias[t5_bucket(k_pos[None, :] - q_pos[:, None])], (2, 0, 1)).astype(jnp.float32)
    allowed = (k_pos[None, :] // CHUNK) <= (q_pos[:, None] // CHUNK)
    logits = jnp.einsum('bqhtd,bkhtd->bthqk', q, k).astype(jnp.float32) * (HEAD_DIM ** -0.5) + bias[None, None]
    logits = jnp.where(allowed[None, None, None], logits, -jnp.inf)
    probs = jax.nn.softmax(logits, axis=-1)
    attn = probs[:, 0] - lam * probs[:, 1]
    return jnp.einsum('bhqk,bkhd->bqhd', attn.astype(v.dtype), v)


def prompt_attention(q, k, v, pos, lam, rel_bias):
    B, S = q.shape[:2]
    nb = S // Q_BLOCK
    qb = jnp.swapaxes(q.reshape(B, nb, Q_BLOCK, N_HEADS, 2, HEAD_DIM), 0, 1)
    pb = pos.reshape(nb, Q_BLOCK)
    out = lax.map(lambda a: diff_attention(a[0], k, v, a[1], pos, lam, rel_bias), (qb, pb))
    return jnp.swapaxes(out, 0, 1).reshape(B, S, N_HEADS, V_HEAD_DIM)


def diff_lambda(lq1, lk1, lq2, lk2, lam_init):
    f = jnp.float32
    return (jnp.exp(jnp.sum(lq1.astype(f) * lk1.astype(f))) -
            jnp.exp(jnp.sum(lq2.astype(f) * lk2.astype(f))) + lam_init)


def mixer_in(x, shift, scale, g_norm, w_in, ln_g, ln_b):
    B, S = x.shape[:2]
    h = modulate(rmsnorm(x, g_norm), shift, scale)
    z = jnp.einsum('bsd,de->bse', h, w_in)
    o1 = MLP_WIDTH
    o2 = 2 * MLP_WIDTH
    o3 = o2 + QK_WIDTH
    o4 = o3 + QK_WIDTH
    u = jax.nn.gelu(z[..., :o1])
    gv = layernorm(jax.nn.gelu(z[..., o1:o2]), ln_g, ln_b)
    q = z[..., o2:o3].reshape(B, S, N_HEADS, 2, HEAD_DIM)
    k = z[..., o3:o4].reshape(B, S, N_HEADS, 2, HEAD_DIM)
    v = z[..., o4:].reshape(B, S, N_HEADS, V_HEAD_DIM)
    return u, gv, q, k, v


def spatial_gate(u, gv, w_s, b_s):
    T = u.shape[2]
    idx = jnp.arange(T)
    mask = (idx[None, :] // CHUNK) <= (idx[:, None] // CHUNK)
    w = jnp.where(mask[None], w_s[:, :T, :T], 0)
    mixed = jnp.einsum('gij,bnjgc->bnigc', w, gv) + jnp.transpose(b_s[:, :T])[None, None, :, :, None]
    return u * mixed


def mixer_out(x, m, a, gate, sub_g, lam_init, w_out):
    B, S = x.shape[:2]
    a = rmsnorm(a, sub_g) * (1 - lam_init)
    cat = jnp.concatenate([m.reshape(B, S, MLP_WIDTH), a.reshape(B, S, ATTN_WIDTH)], axis=-1)
    return x + gate[:, None, :] * jnp.einsum('bse,ed->bsd', cat, w_out)


def ffn_sublayer(x, shift, scale, gate, g_norm, w_ffn_in, w_ffn_out):
    h = modulate(rmsnorm(x, g_norm), shift, scale)
    zg, zu = jnp.split(jnp.einsum('bsd,df->bsf', h, w_ffn_in), 2, axis=-1)
    return x + gate[:, None, :] * jnp.einsum('bsf,fd->bsd', jax.nn.silu(zg) * zu, w_ffn_out)


def setup_inputs(seed: int = 0) -> dict:
    key = jax.random.key(seed)
    ks = jax.random.split(key, 32)
    f = jnp.float32
    nrm = lambda k, s, sc: jax.random.normal(k, s, f) * sc
    D = D_MODEL
    return {
        'x_prompt': nrm(ks[0], (BATCH, SEQ, D), 1.0),
        'x_sample': nrm(ks[1], (DEC_BATCH, DEC_SEQ, D), 1.0),
        'cache_k': nrm(ks[2], (DEPTH, DEC_BATCH, PAST_LEN, N_HEADS, 2 * HEAD_DIM), 1.0),
        'cache_v': nrm(ks[3], (DEPTH, DEC_BATCH, PAST_LEN, N_HEADS, V_HEAD_DIM), 1.0),
        'c_prompt': nrm(ks[4], (BATCH, D), 1.0),
        'c_sample': nrm(ks[5], (DEC_BATCH, D), 1.0),
        'rel_bias': nrm(ks[6], (N_BUCKETS, N_HEADS), 0.5),
        'w_ada': nrm(ks[7], (DEPTH, D, 6 * D), 0.5 * D ** -0.5),
        'b_ada': nrm(ks[8], (DEPTH, 6 * D), 0.01),
        'w_ada_final': nrm(ks[9], (D, 2 * D), 0.5 * D ** -0.5),
        'b_ada_final': nrm(ks[10], (2 * D,), 0.01),
        'g_mix': 1.0 + nrm(ks[11], (DEPTH, D), 0.01),
        'g_ffn': 1.0 + nrm(ks[12], (DEPTH, D), 0.01),
        'g_final': 1.0 + nrm(ks[13], (D,), 0.01),
        'w_in': nrm(ks[14], (DEPTH, D, IN_WIDTH), D ** -0.5),
        'mlp_ln_g': 1.0 + nrm(ks[15], (DEPTH, MLP_WIDTH), 0.01),
        'mlp_ln_b': nrm(ks[16], (DEPTH, MLP_WIDTH), 0.01),
        'w_s': nrm(ks[17], (DEPTH, MLP_GROUPS, MLP_CHUNK, MLP_CHUNK), MLP_CHUNK ** -0.5),
        'b_s': 1.0 + nrm(ks[18], (DEPTH, MLP_GROUPS, MLP_CHUNK), 0.01),
        'lambda_q1': nrm(ks[19], (DEPTH, HEAD_DIM), 0.1),
        'lambda_k1': nrm(ks[20], (DEPTH, HEAD_DIM), 0.1),
        'lambda_q2': nrm(ks[21], (DEPTH, HEAD_DIM), 0.1),
        'lambda_k2': nrm(ks[22], (DEPTH, HEAD_DIM), 0.1),
        'sub_g': 1.0 + nrm(ks[23], (DEPTH, V_HEAD_DIM), 0.01),
        'w_out': nrm(ks[24], (DEPTH, D, D), D ** -0.5),
        'w_ffn_in': nrm(ks[25], (DEPTH, D, 2 * D_FF), D ** -0.5),
        'w_ffn_out': nrm(ks[26], (DEPTH, D_FF, D), D_FF ** -0.5),
    }


def reference(x_prompt, x_sample, cache_k, cache_v, c_prompt, c_sample, rel_bias,
              w_ada, b_ada, w_ada_final, b_ada_final, g_mix, g_ffn, g_final,
              w_in, mlp_ln_g, mlp_ln_b, w_s, b_s, lambda_q1, lambda_k1, lambda_q2, lambda_k2,
              sub_g, w_out, w_ffn_in, w_ffn_out):
    B, S, _ = x_prompt.shape
    DB, T, _ = x_sample.shape
    nc = S // MLP_CHUNK
    pos_p = jnp.arange(S, dtype=jnp.int32)
    q_pos_s = PAST_LEN + jnp.arange(T, dtype=jnp.int32)
    k_pos_s = jnp.arange(PAST_LEN + T, dtype=jnp.int32)
    xp, xs = x_prompt, x_sample
    kp_rows, vp_rows, ks_rows, vs_rows, gvs_rows = [], [], [], [], []
    for l in range(DEPTH):
        lam_init = 0.8 - 0.6 * math.exp(-0.3 * l)
        lam = diff_lambda(lambda_q1[l], lambda_k1[l], lambda_q2[l], lambda_k2[l], lam_init)

        sh1, sc1, gt1, sh2, sc2, gt2 = adaln(c_prompt, w_ada[l], b_ada[l], 6)
        u, gv, q, k, v = mixer_in(xp, sh1, sc1, g_mix[l], w_in[l], mlp_ln_g[l], mlp_ln_b[l])
        m = spatial_gate(u.reshape(B, nc, MLP_CHUNK, MLP_GROUPS, MLP_GROUP_DIM),
                         gv.reshape(B, nc, MLP_CHUNK, MLP_GROUPS, MLP_GROUP_DIM), w_s[l], b_s[l])
        a = prompt_attention(q, k, v, pos_p, lam, rel_bias)
        xp = mixer_out(xp, m, a, gt1, sub_g[l], lam_init, w_out[l])
        xp = ffn_sublayer(xp, sh2, sc2, gt2, g_ffn[l], w_ffn_in[l], w_ffn_out[l])
        kp_rows.append(k.reshape(B, S, N_HEADS, 2 * HEAD_DIM))
        vp_rows.append(v)

        sh1, sc1, gt1, sh2, sc2, gt2 = adaln(c_sample, w_ada[l], b_ada[l], 6)
        u, gv, q, k, v = mixer_in(xs, sh1, sc1, g_mix[l], w_in[l], mlp_ln_g[l], mlp_ln_b[l])
        gv_s = gv.reshape(DB, T, MLP_GROUPS, MLP_GROUP_DIM)
        m = spatial_gate(u.reshape(DB, 1, T, MLP_GROUPS, MLP_GROUP_DIM), gv_s[:, None], w_s[l], b_s[l])
        k_all = jnp.concatenate([cache_k[l].reshape(DB, PAST_LEN, N_HEADS, 2, HEAD_DIM), k], axis=1)
        v_all = jnp.concatenate([cache_v[l], v], axis=1)
        a = diff_attention(q, k_all, v_all, q_pos_s, k_pos_s, lam, rel_bias)
        xs = mixer_out(xs, m, a, gt1, sub_g[l], lam_init, w_out[l])
        xs = ffn_sublayer(xs, sh2, sc2, gt2, g_ffn[l], w_ffn_in[l], w_ffn_out[l])
        ks_rows.append(k.reshape(DB, T, N_HEADS, 2 * HEAD_DIM))
        vs_rows.append(v)
        gvs_rows.append(gv_s)

    shp, scp = adaln(c_prompt, w_ada_final, b_ada_final, 2)
    shs, scs = adaln(c_sample, w_ada_final, b_ada_final, 2)
    y_prompt = modulate(rmsnorm(xp, g_final), shp, scp)
    y_sample = modulate(rmsnorm(xs, g_final), shs, scs)
    new_k_prompt = jnp.stack(kp_rows)
    new_v_prompt = jnp.stack(vp_rows)
    new_k_sample = jnp.stack(ks_rows)
    new_v_sample = jnp.stack(vs_rows)
    new_gv_sample = jnp.stack(gvs_rows)
    return (y_prompt, y_sample, new_k_prompt, new_v_prompt, new_k_sample, new_v_sample, new_gv_sample)
```

```python
import functools
import math

import jax
import jax.numpy as jnp
from jax import lax
from jax.experimental import pallas as pl
from jax.experimental.pallas import tpu as pltpu

LANES = 128
SUBLANES = 8
VMEM_LIMIT_BYTES = 56 * 1024 * 1024

CHUNK = 64
N_HEADS = 8
HEAD_DIM = 64
V_HEAD_DIM = 128
MLP_GROUPS = 8
MLP_GROUP_DIM = 128
N_BUCKETS = 32
MAX_DISTANCE = 128
EPS = 1e-6
MASK_VALUE = -1e30

BF16 = jnp.bfloat16
F32 = jnp.float32


def _params(sem):
    return pltpu.CompilerParams(dimension_semantics=sem, vmem_limit_bytes=VMEM_LIMIT_BYTES)


def _adaln_kernel(c_ref, w_ref, b_ref, o_ref):
    c = c_ref[...]
    a = c * jax.nn.sigmoid(c)
    o_ref[...] = jnp.dot(a, w_ref[...], preferred_element_type=F32) + b_ref[...]


def _adaln(c, w, b, tn=1024):
    rows, d = c.shape
    n = w.shape[1]
    return pl.pallas_call(
        _adaln_kernel,
        grid=(n // tn,),
        in_specs=[
            pl.BlockSpec((rows, d), lambda j: (0, 0)),
            pl.BlockSpec((d, tn), lambda j: (0, j)),
            pl.BlockSpec((1, tn), lambda j: (0, j)),
        ],
        out_specs=pl.BlockSpec((rows, tn), lambda j: (0, j)),
        out_shape=jax.ShapeDtypeStruct((rows, n), F32),
        compiler_params=_params(("arbitrary",)),
        name="adaln",
    )(c, w, b.reshape(1, n))


def _rel_bias_kernel(tab_ref, o_ref, *, q_start, k_start, shift_far):
    h = pl.program_id(0)
    nq, nk = o_ref.shape[1], o_ref.shape[2]
    q_pos = q_start + lax.broadcasted_iota(jnp.int32, (nq, nk), 0)
    k_pos = k_start + lax.broadcasted_iota(jnp.int32, (nq, nk), 1)
    rel = k_pos - q_pos
    nb = N_BUCKETS // 2
    max_exact = nb // 2
    ret = jnp.where(rel > 0, nb, 0)
    n = jnp.abs(rel)
    nf = jnp.maximum(n, 1).astype(F32)
    large = max_exact + (jnp.log(nf / max_exact) / math.log(MAX_DISTANCE / max_exact)
                         * (nb - max_exact)).astype(jnp.int32)
    large = jnp.minimum(large, nb - 1)
    bucket = ret + jnp.where(n < max_exact, n, large)
    bias = jnp.zeros((nq, nk), F32)
    for bkt in range(N_BUCKETS):
        bias = jnp.where(bucket == bkt, tab_ref[bkt, h], bias)
    if shift_far:
        bias = bias - tab_ref[nb - 1, h]
    allowed = (k_pos // CHUNK) <= (q_pos // CHUNK)
    o_ref[0] = jnp.where(allowed, bias, MASK_VALUE)


def _rel_bias_tiles(rel_bias, nq, nk, q_start, k_start, shift_far):
    return pl.pallas_call(
        functools.partial(_rel_bias_kernel, q_start=q_start, k_start=k_start, shift_far=shift_far),
        grid=(N_HEADS,),
        in_specs=[pl.BlockSpec(memory_space=pltpu.SMEM)],
        out_specs=pl.BlockSpec((1, nq, nk), lambda h: (h, 0, 0)),
        out_shape=jax.ShapeDtypeStruct((N_HEADS, nq, nk), F32),
        compiler_params=_params(("arbitrary",)),
        name="rel_bias",
    )(rel_bias)


def _mixer_in_kernel(x_ref, sh_ref, sc_ref, g_ref, w_ref, lng_ref, lnb_ref, ws_ref, bs_ref,
                     *refs, t_chunk, emit_gv):
    if emit_gv:
        m_ref, q_ref, k_ref, kb_ref, v_ref, vb_ref, gv_ref, h_s, u_s = refs
    else:
        m_ref, q_ref, k_ref, kb_ref, v_ref, vb_ref, h_s, u_s = refs
        gv_ref = None
    j = pl.program_id(1)
    rows = h_s.shape[0]

    def store_per_head(ref, val):
        for hd in range(N_HEADS):
            ref[pl.ds(hd, rows, stride=N_HEADS), :] = val[:, hd * LANES:(hd + 1) * LANES]

    @pl.when(j == 0)
    def _():
        x = x_ref[...]
        y = x * lax.rsqrt(jnp.mean(x * x, axis=-1, keepdims=True) + EPS)
        y = y * g_ref[...]
        hm = y * (1.0 + sc_ref[...]) + sh_ref[...]
        h_s[...] = hm.reshape(rows, hm.shape[-1]).astype(BF16)

    z = jnp.dot(h_s[...], w_ref[...], preferred_element_type=F32)

    @pl.when(j == 0)
    def _():
        u_s[...] = jax.nn.gelu(z)

    @pl.when(j == 1)
    def _():
        g = jax.nn.gelu(z)
        mu = jnp.mean(g, axis=-1, keepdims=True)
        var = jnp.mean(jnp.square(g - mu), axis=-1, keepdims=True)
        gv = (g - mu) * lax.rsqrt(var + EPS) * lng_ref[...] + lnb_ref[...]
        if emit_gv:
            store_per_head(gv_ref, gv)
        gvb = gv.astype(BF16)
        ii = lax.broadcasted_iota(jnp.int32, (t_chunk, t_chunk), 0)
        jj = lax.broadcasted_iota(jnp.int32, (t_chunk, t_chunk), 1)
        mask = (jj // CHUNK) <= (ii // CHUNK)
        for grp in range(MLP_GROUPS):
            wg = jnp.where(mask, ws_ref[grp], 0.0).astype(BF16)
            bg = bs_ref[grp]
            cs = slice(grp * MLP_GROUP_DIM, (grp + 1) * MLP_GROUP_DIM)
            for c in range(rows // t_chunk):
                rs = slice(c * t_chunk, (c + 1) * t_chunk)
                mixed = jnp.dot(wg, gvb[rs, cs], preferred_element_type=F32) + bg
                m_ref[rs, cs] = (u_s[rs, cs] * mixed).astype(BF16)

    @pl.when(j == 2)
    def _():
        q_ref[...] = (z * (HEAD_DIM ** -0.5)).astype(BF16)

    @pl.when(j == 3)
    def _():
        store_per_head(k_ref, z)
        kb_ref[...] = z.astype(BF16)

    @pl.when(j == 4)
    def _():
        store_per_head(v_ref, z)
        vb_ref[...] = z.astype(BF16)


def _mixer_in(x, shift, scale, g_mix, w_in_b, ln_g, ln_b, w_s, b_s, *, nb_blk, r_blk, t_chunk, emit_gv):
    nbat, r, d = x.shape
    width = 1024
    nr = r // r_blk
    n_tiles = (nbat // nb_blk) * nr
    rows = nb_blk * r_blk
    tokens = nbat * r
    row_idx = lambda i, j: (i, 0)
    flat = jax.ShapeDtypeStruct((tokens, width), BF16)
    per_head = jax.ShapeDtypeStruct((tokens * N_HEADS, LANES), F32)
    flat_spec = pl.BlockSpec((rows, width), row_idx)
    per_head_spec = pl.BlockSpec((rows * N_HEADS, LANES), row_idx)
    out_shape = [flat, flat, per_head, flat, per_head, flat]
    out_specs = [flat_spec, flat_spec, per_head_spec, flat_spec, per_head_spec, flat_spec]
    if emit_gv:
        out_shape.append(per_head)
        out_specs.append(per_head_spec)
    ws_t = w_s[:, :t_chunk, :t_chunk]
    bs_t = b_s[:, :t_chunk, None]
    return pl.pallas_call(
        functools.partial(_mixer_in_kernel, t_chunk=t_chunk, emit_gv=emit_gv),
        grid=(n_tiles, 5),
        in_specs=[
            pl.BlockSpec((nb_blk, r_blk, d), lambda i, j: (i // nr, i % nr, 0)),
            pl.BlockSpec((nb_blk, 1, d), lambda i, j: (i // nr, 0, 0)),
            pl.BlockSpec((nb_blk, 1, d), lambda i, j: (i // nr, 0, 0)),
            pl.BlockSpec((1, 1, d), lambda i, j: (0, 0, 0)),
            pl.BlockSpec((d, width), lambda i, j: (0, j)),
            pl.BlockSpec((1, width), lambda i, j: (0, 0)),
            pl.BlockSpec((1, width), lambda i, j: (0, 0)),
            pl.BlockSpec((MLP_GROUPS, t_chunk, t_chunk), lambda i, j: (0, 0, 0)),
            pl.BlockSpec((MLP_GROUPS, t_chunk, 1), lambda i, j: (0, 0, 0)),
        ],
        out_specs=out_specs,
        out_shape=out_shape,
        scratch_shapes=[pltpu.VMEM((rows, d), BF16), pltpu.VMEM((rows, width), F32)],
        compiler_params=_params(("arbitrary", "arbitrary")),
        name="mixer_in",
    )(x, shift[:, None, :], scale[:, None, :], g_mix.reshape(1, 1, d), w_in_b,
      ln_g.reshape(1, width), ln_b.reshape(1, width), ws_t, bs_t)


def _split_q(q):
    lane = lax.broadcasted_iota(jnp.int32, q.shape, 1)
    zero = jnp.zeros_like(q)
    return jnp.concatenate([jnp.where(lane < HEAD_DIM, q, zero), jnp.where(lane >= HEAD_DIM, q, zero)], axis=0)


def _softmax_step(s, v, m, l, acc):
    m_new = jnp.maximum(m, jnp.max(s, axis=-1, keepdims=True))
    alpha = jnp.exp(m - m_new)
    p = jnp.exp(s - m_new)
    l_new = alpha * l + jnp.sum(p, axis=-1, keepdims=True)
    acc_new = alpha * acc + jnp.dot(p.astype(BF16), v, preferred_element_type=F32)
    return m_new, l_new, acc_new


def _diff_lambda(lam_ref, lam_init):
    lv = lam_ref[...]
    s1 = jnp.sum(lv[0:1] * lv[1:2], axis=-1, keepdims=True)
    s2 = jnp.sum(lv[2:3] * lv[3:4], axis=-1, keepdims=True)
    return jnp.exp(s1) - jnp.exp(s2) + lam_init


def _finish_heads(l, acc, lam, subg, lam_init, n):
    o = acc / l
    a = o[:n] - lam * o[n:]
    a = a * lax.rsqrt(jnp.mean(a * a, axis=-1, keepdims=True) + EPS)
    return a * subg * (1.0 - lam_init)


def _qk(qs, k):
    return lax.dot_general(qs, k, (((1,), (1,)), ((), ())), preferred_element_type=F32)


def _attn_prompt_kernel(q_ref, k_ref, v_ref, bias_ref, lam_ref, subg_ref, o_ref, *, blk, lam_init):
    qi = pl.program_id(2)
    qs = _split_q(q_ref[0])
    n2 = 2 * blk

    def kv(jblk):
        start = pl.multiple_of(jblk * blk, blk)
        return k_ref[0, pl.ds(start, blk), :], v_ref[0, pl.ds(start, blk), :]

    def far(jblk, carry):
        k, v = kv(jblk)
        return _softmax_step(_qk(qs, k), v, *carry)

    def near(jblk, bias, carry):
        k, v = kv(jblk)
        s = _qk(qs, k) + jnp.concatenate([bias, bias], axis=0)
        return _softmax_step(s, v, *carry)

    carry = (jnp.full((n2, 1), MASK_VALUE, F32), jnp.zeros((n2, 1), F32), jnp.zeros((n2, V_HEAD_DIM), F32))
    carry = lax.fori_loop(0, jnp.maximum(qi - 1, 0), far, carry)
    carry = lax.cond(qi >= 1,
                     lambda c: near(qi - 1, bias_ref[0, :, :blk], c),
                     lambda c: c, carry)
    m, l, acc = near(qi, bias_ref[0, :, blk:], carry)
    lam = _diff_lambda(lam_ref, lam_init)
    o_ref[0] = _finish_heads(l, acc, lam, subg_ref[...], lam_init, blk).astype(o_ref.dtype)


def _attn_prompt(q, k, v, bias, lam_vecs, sub_g, *, blk, lam_init):
    b, s, _ = q.shape
    return pl.pallas_call(
        functools.partial(_attn_prompt_kernel, blk=blk, lam_init=lam_init),
        grid=(b, N_HEADS, s // blk),
        in_specs=[
            pl.BlockSpec((1, blk, LANES), lambda bi, h, qi: (bi, qi, h)),
            pl.BlockSpec((1, s, LANES), lambda bi, h, qi: (bi, 0, h)),
            pl.BlockSpec((1, s, LANES), lambda bi, h, qi: (bi, 0, h)),
            pl.BlockSpec((1, blk, 2 * blk), lambda bi, h, qi: (h, 0, 0)),
            pl.BlockSpec((4, HEAD_DIM), lambda bi, h, qi: (0, 0)),
            pl.BlockSpec((1, V_HEAD_DIM), lambda bi, h, qi: (0, 0)),
        ],
        out_specs=pl.BlockSpec((1, blk, LANES), lambda bi, h, qi: (bi, qi, h)),
        out_shape=jax.ShapeDtypeStruct((b, s, N_HEADS * V_HEAD_DIM), BF16),
        compiler_params=_params(("arbitrary", "arbitrary", "arbitrary")),
        name="attn_prompt",
    )(q, k, v, bias, lam_vecs, sub_g)


def _attn_sample_kernel(q_ref, ck_ref, cv_ref, nk_ref, nv_ref, bias_ref, lam_ref, subg_ref, o_ref,
                        *, past, lam_init):
    t = q_ref.shape[1]
    lam = _diff_lambda(lam_ref, lam_init)
    for hd in range(N_HEADS):
        cs = slice(hd * LANES, (hd + 1) * LANES)
        qs = _split_q(q_ref[0, :, cs])
        bias = bias_ref[hd]
        bias2 = jnp.concatenate([bias, bias], axis=0)
        carry = (jnp.full((2 * t, 1), MASK_VALUE, F32), jnp.zeros((2 * t, 1), F32),
                 jnp.zeros((2 * t, V_HEAD_DIM), F32))
        kc = ck_ref[0, pl.ds(hd, past, stride=N_HEADS), :].astype(BF16)
        vc = cv_ref[0, pl.ds(hd, past, stride=N_HEADS), :].astype(BF16)
        carry = _softmax_step(_qk(qs, kc) + bias2[:, :past], vc, *carry)
        m, l, acc = _softmax_step(_qk(qs, nk_ref[0, :, cs]) + bias2[:, past:], nv_ref[0, :, cs], *carry)
        o_ref[0, :, cs] = _finish_heads(l, acc, lam, subg_ref[...], lam_init, t).astype(o_ref.dtype)


def _attn_sample(q, cache_k, cache_v, new_k, new_v, bias, lam_vecs, sub_g, *, lam_init):
    b, t, width = q.shape
    past = cache_k.shape[1] // N_HEADS
    flat_spec = pl.BlockSpec((1, t, width), lambda bi: (bi, 0, 0))
    cache_spec = pl.BlockSpec((1, past * N_HEADS, LANES), lambda bi: (bi, 0, 0))
    return pl.pallas_call(
        functools.partial(_attn_sample_kernel, past=past, lam_init=lam_init),
        grid=(b,),
        in_specs=[
            flat_spec, cache_spec, cache_spec, flat_spec, flat_spec,
            pl.BlockSpec((N_HEADS, t, past + t), lambda bi: (0, 0, 0)),
            pl.BlockSpec((4, HEAD_DIM), lambda bi: (0, 0)),
            pl.BlockSpec((1, V_HEAD_DIM), lambda bi: (0, 0)),
        ],
        out_specs=flat_spec,
        out_shape=jax.ShapeDtypeStruct((b, t, width), BF16),
        compiler_params=_params(("arbitrary",)),
        name="attn_sample",
    )(q, cache_k, cache_v, new_k, new_v, bias, lam_vecs, sub_g)


def _mixer_out_kernel(x_ref, m_ref, a_ref, gt_ref, w_ref, o_ref):
    half = m_ref.shape[-1]
    y = jnp.dot(m_ref[...], w_ref[:half, :], preferred_element_type=F32)
    y = y + jnp.dot(a_ref[...], w_ref[half:, :], preferred_element_type=F32)
    nb, r, d = x_ref.shape
    o_ref[...] = x_ref[...] + gt_ref[...] * y.reshape(nb, r, d)


def _mixer_out(x, m, a, gate, w_out_b, *, nb_blk, r_blk):
    nbat, r, d = x.shape
    nr = r // r_blk
    rows = nb_blk * r_blk
    half = m.shape[-1]
    return pl.pallas_call(
        _mixer_out_kernel,
        grid=((nbat // nb_blk) * nr,),
        in_specs=[
            pl.BlockSpec((nb_blk, r_blk, d), lambda i: (i // nr, i % nr, 0)),
            pl.BlockSpec((rows, half), lambda i: (i, 0)),
            pl.BlockSpec((rows, half), lambda i: (i, 0)),
            pl.BlockSpec((nb_blk, 1, d), lambda i: (i // nr, 0, 0)),
            pl.BlockSpec((d, d), lambda i: (0, 0)),
        ],
        out_specs=pl.BlockSpec((nb_blk, r_blk, d), lambda i: (i // nr, i % nr, 0)),
        out_shape=jax.ShapeDtypeStruct(x.shape, F32),
        compiler_params=_params(("arbitrary",)),
        name="mixer_out",
    )(x, m, a, gate[:, None, :], w_out_b)


def _ffn_kernel(x_ref, sh_ref, sc_ref, gt_ref, g_ref, wg_ref, wu_ref, wo_ref, gf_ref, shf_ref, scf_ref,
                o_ref, h_s, acc_s):
    f = pl.program_id(1)
    rows = h_s.shape[0]

    @pl.when(f == 0)
    def _():
        x = x_ref[...]
        y = x * lax.rsqrt(jnp.mean(x * x, axis=-1, keepdims=True) + EPS)
        hm = (y * g_ref[...]) * (1.0 + sc_ref[...]) + sh_ref[...]
        h_s[...] = hm.reshape(rows, hm.shape[-1]).astype(BF16)
        acc_s[...] = jnp.zeros_like(acc_s)

    h = h_s[...]
    zg = jnp.dot(h, wg_ref[...], preferred_element_type=F32)
    zu = jnp.dot(h, wu_ref[...], preferred_element_type=F32)
    act = (zg * jax.nn.sigmoid(zg) * zu).astype(BF16)
    acc_s[...] += jnp.dot(act, wo_ref[...], preferred_element_type=F32)

    @pl.when(f == pl.num_programs(1) - 1)
    def _():
        nb, r, d = x_ref.shape
        x2 = x_ref[...] + gt_ref[...] * acc_s[...].reshape(nb, r, d)
        y = x2 * lax.rsqrt(jnp.mean(x2 * x2, axis=-1, keepdims=True) + EPS)
        o_ref[...] = (y * gf_ref[...]) * (1.0 + scf_ref[...]) + shf_ref[...]


def _ffn(x, shift, scale, gate, g_ffn, w_in_b, w_out_b, g_final, shift_f, scale_f, *, nb_blk, r_blk, tf):
    nbat, r, d = x.shape
    d_ff = w_out_b.shape[0]
    nf = d_ff // tf
    nr = r // r_blk
    rows = nb_blk * r_blk
    x_spec = pl.BlockSpec((nb_blk, r_blk, d), lambda i, f: (i // nr, i % nr, 0))
    vec_spec = pl.BlockSpec((nb_blk, 1, d), lambda i, f: (i // nr, 0, 0))
    par_spec = pl.BlockSpec((1, 1, d), lambda i, f: (0, 0, 0))
    return pl.pallas_call(
        _ffn_kernel,
        grid=((nbat // nb_blk) * nr, nf),
        in_specs=[
            x_spec, vec_spec, vec_spec, vec_spec, par_spec,
            pl.BlockSpec((d, tf), lambda i, f: (0, f)),
            pl.BlockSpec((d, tf), lambda i, f: (0, f + nf)),
            pl.BlockSpec((tf, d), lambda i, f: (f, 0)),
            par_spec, vec_spec, vec_spec,
        ],
        out_specs=x_spec,
        out_shape=jax.ShapeDtypeStruct(x.shape, F32),
        scratch_shapes=[pltpu.VMEM((rows, d), BF16), pltpu.VMEM((rows, d), F32)],
        compiler_params=_params(("arbitrary", "arbitrary")),
        name="ffn",
    )(x, shift[:, None, :], scale[:, None, :], gate[:, None, :], g_ffn.reshape(1, 1, d),
      w_in_b, w_in_b, w_out_b, g_final.reshape(1, 1, d), shift_f[:, None, :], scale_f[:, None, :])


ATTN_BLOCK = 256
PROMPT_ROWS = 512
SAMPLE_BATCH_BLOCK = 8
FFN_TILE = 512


def kernel(x_prompt, x_sample, cache_k, cache_v, c_prompt, c_sample, rel_bias, w_ada, b_ada, w_ada_final,
           b_ada_final, g_mix, g_ffn, g_final, w_in, mlp_ln_g, mlp_ln_b, w_s, b_s, lambda_q1, lambda_k1,
           lambda_q2, lambda_k2, sub_g, w_out, w_ffn_in, w_ffn_out):
    B, S, D = x_prompt.shape
    DB, T, _ = x_sample.shape
    depth = w_in.shape[0]
    past = cache_k.shape[2]
    width = N_HEADS * V_HEAD_DIM
    mlp_chunk = w_s.shape[-1]

    c_all = jnp.concatenate([c_prompt, c_sample], axis=0)
    mod_f = _adaln(c_all, w_ada_final, b_ada_final)
    bias_p = _rel_bias_tiles(rel_bias, ATTN_BLOCK, 2 * ATTN_BLOCK, ATTN_BLOCK, 0, True)
    bias_s = _rel_bias_tiles(rel_bias, T, past + T, past, 0, False)

    assert depth == 1, "the final adaLN norm is fused into the single layer's FFN kernel"
    lam_init = 0.8 - 0.6 * math.exp(-0.3 * 0)
    lam_vecs = jnp.stack([lambda_q1[0], lambda_k1[0], lambda_q2[0], lambda_k2[0]])
    subg = sub_g.reshape(1, V_HEAD_DIM)
    mod = _adaln(c_all, w_ada[0], b_ada[0])
    sh1, sc1, gt1, sh2, sc2, gt2 = jnp.split(mod, 6, axis=-1)
    shf, scf = jnp.split(mod_f, 2, axis=-1)
    w_in_b = w_in[0].astype(BF16)
    w_out_b = w_out[0].astype(BF16)
    w_f_in_b = w_ffn_in[0].astype(BF16)
    w_f_out_b = w_ffn_out[0].astype(BF16)
    mixer_w = (g_mix[0], w_in_b, mlp_ln_g[0], mlp_ln_b[0], w_s[0], b_s[0])

    m, q, kp, kb, vp, vb = _mixer_in(x_prompt, sh1[:B], sc1[:B], *mixer_w, nb_blk=1, r_blk=PROMPT_ROWS,
                                     t_chunk=mlp_chunk, emit_gv=False)
    a = _attn_prompt(q.reshape(B, S, width), kb.reshape(B, S, width), vb.reshape(B, S, width),
                     bias_p, lam_vecs, subg, blk=ATTN_BLOCK, lam_init=lam_init)
    xp = _mixer_out(x_prompt, m, a.reshape(B * S, width), gt1[:B], w_out_b, nb_blk=1, r_blk=PROMPT_ROWS)
    yp = _ffn(xp, sh2[:B], sc2[:B], gt2[:B], g_ffn[0], w_f_in_b, w_f_out_b, g_final, shf[:B], scf[:B],
              nb_blk=1, r_blk=PROMPT_ROWS, tf=FFN_TILE)

    m, q, ks, kb, vs, vb, gvs = _mixer_in(x_sample, sh1[B:], sc1[B:], *mixer_w, nb_blk=SAMPLE_BATCH_BLOCK,
                                          r_blk=T, t_chunk=T, emit_gv=True)
    a = _attn_sample(q.reshape(DB, T, width), cache_k.reshape(DB, past * N_HEADS, LANES),
                     cache_v.reshape(DB, past * N_HEADS, LANES), kb.reshape(DB, T, width),
                     vb.reshape(DB, T, width), bias_s, lam_vecs, subg, lam_init=lam_init)
    xs = _mixer_out(x_sample, m, a.reshape(DB * T, width), gt1[B:], w_out_b, nb_blk=SAMPLE_BATCH_BLOCK, r_blk=T)
    ys = _ffn(xs, sh2[B:], sc2[B:], gt2[B:], g_ffn[0], w_f_in_b, w_f_out_b, g_final, shf[B:], scf[B:],
              nb_blk=SAMPLE_BATCH_BLOCK, r_blk=T, tf=FFN_TILE)

    head_shape = (N_HEADS, V_HEAD_DIM)
    return (yp, ys, kp.reshape(1, B, S, *head_shape), vp.reshape(1, B, S, *head_shape),
            ks.reshape(1, DB, T, *head_shape), vs.reshape(1, DB, T, *head_shape),
            gvs.reshape(1, DB, T, MLP_GROUPS, MLP_GROUP_DIM))
```

```python
import functools
import math

import jax
import jax.numpy as jnp
from jax import lax
from jax.experimental import pallas as pl
from jax.experimental.pallas import tpu as pltpu

LANES = 128
SUBLANES = 8
VMEM_LIMIT_BYTES = 56 * 1024 * 1024
MXU_WIDTH = 256

ATTN_STRIP = MXU_WIDTH
ONES_ROWS = SUBLANES

CHUNK = 64
N_HEADS = 8
HEAD_DIM = 64
V_HEAD_DIM = 128
MLP_GROUPS = 8
MLP_GROUP_DIM = 128
N_BUCKETS = 32
MAX_DISTANCE = 128
EPS = 1e-6
MASK_VALUE = -1e30
LOG2E = math.log2(math.e)

BF16 = jnp.bfloat16
F32 = jnp.float32


def _params(sem):
    return pltpu.CompilerParams(dimension_semantics=sem, vmem_limit_bytes=VMEM_LIMIT_BYTES)


def _adaln_kernel(c_ref, w_ref, b_ref, o_ref):
    c = c_ref[...]
    a = c * jax.nn.sigmoid(c)
    o_ref[...] = jnp.dot(a, w_ref[...], preferred_element_type=F32) + b_ref[...]


def _adaln(c, w, b, tn=1024):
    rows, d = c.shape
    n = w.shape[1]
    return pl.pallas_call(
        _adaln_kernel,
        grid=(n // tn,),
        in_specs=[
            pl.BlockSpec((rows, d), lambda j: (0, 0)),
            pl.BlockSpec((d, tn), lambda j: (0, j)),
            pl.BlockSpec((1, tn), lambda j: (0, j)),
        ],
        out_specs=pl.BlockSpec((rows, tn), lambda j: (0, j)),
        out_shape=jax.ShapeDtypeStruct((rows, n), F32),
        compiler_params=_params(("arbitrary",)),
        name="adaln",
    )(c, w, b.reshape(1, n))


def _rel_bias_kernel(tab_ref, o_ref, *, nq, nk, q_start, k_start, shift_far, keys_on_rows):
    h = pl.program_id(0)
    shape, q_axis, k_axis = ((nk, nq), 1, 0) if keys_on_rows else ((nq, nk), 0, 1)
    q_pos = q_start + lax.broadcasted_iota(jnp.int32, shape, q_axis)
    k_pos = k_start + lax.broadcasted_iota(jnp.int32, shape, k_axis)
    rel = k_pos - q_pos
    nb = N_BUCKETS // 2
    max_exact = nb // 2
    ret = jnp.where(rel > 0, nb, 0)
    n = jnp.abs(rel)
    nf = jnp.maximum(n, 1).astype(F32)
    large = max_exact + (jnp.log(nf / max_exact) / math.log(MAX_DISTANCE / max_exact)
                         * (nb - max_exact)).astype(jnp.int32)
    large = jnp.minimum(large, nb - 1)
    bucket = ret + jnp.where(n < max_exact, n, large)
    bias = jnp.zeros(shape, F32)
    for bkt in range(N_BUCKETS):
        bias = jnp.where(bucket == bkt, tab_ref[bkt, h], bias)
    if shift_far:
        bias = bias - tab_ref[nb - 1, h]
    allowed = (k_pos // CHUNK) <= (q_pos // CHUNK)
    o_ref[0] = jnp.where(allowed, bias * LOG2E, MASK_VALUE)


def _rel_bias_tiles(rel_bias, nq, nk, q_start, k_start, shift_far, keys_on_rows):
    out_tile = (nk, nq) if keys_on_rows else (nq, nk)
    return pl.pallas_call(
        functools.partial(_rel_bias_kernel, nq=nq, nk=nk, q_start=q_start, k_start=k_start,
                          shift_far=shift_far, keys_on_rows=keys_on_rows),
        grid=(N_HEADS,),
        in_specs=[pl.BlockSpec(memory_space=pltpu.SMEM)],
        out_specs=pl.BlockSpec((1,) + out_tile, lambda h: (h, 0, 0)),
        out_shape=jax.ShapeDtypeStruct((N_HEADS,) + out_tile, F32),
        compiler_params=_params(("arbitrary",)),
        name="rel_bias",
    )(rel_bias)


def _mixer_in_kernel(x_ref, sh_ref, sc_ref, g_ref, w_ref, lng_ref, lnb_ref, ws_ref, bs_ref,
                     *refs, t_chunk, emit_gv, v_transposed):
    if emit_gv:
        m_ref, q_ref, k_ref, kb_ref, v_ref, vb_ref, gv_ref, h_s, u_s = refs
    else:
        m_ref, q_ref, k_ref, kb_ref, v_ref, vb_ref, h_s, u_s = refs
        gv_ref = None
    j = pl.program_id(1)
    rows = h_s.shape[0]

    def store_per_head(ref, val):
        for hd in range(N_HEADS):
            ref[pl.ds(hd, rows, stride=N_HEADS), :] = val[:, hd * LANES:(hd + 1) * LANES]

    @pl.when(j == 0)
    def _():
        x = x_ref[...]
        y = x * lax.rsqrt(jnp.mean(x * x, axis=-1, keepdims=True) + EPS)
        y = y * g_ref[...]
        hm = y * (1.0 + sc_ref[...]) + sh_ref[...]
        h_s[...] = hm.reshape(rows, hm.shape[-1]).astype(BF16)

    z = jnp.dot(h_s[...], w_ref[...], preferred_element_type=F32)

    @pl.when(j == 0)
    def _():
        u_s[...] = jax.nn.gelu(z)

    @pl.when(j == 1)
    def _():
        g = jax.nn.gelu(z)
        mu = jnp.mean(g, axis=-1, keepdims=True)
        var = jnp.mean(jnp.square(g - mu), axis=-1, keepdims=True)
        gv = (g - mu) * lax.rsqrt(var + EPS) * lng_ref[...] + lnb_ref[...]
        if emit_gv:
            store_per_head(gv_ref, gv)
        gvb = gv.astype(BF16)
        ii = lax.broadcasted_iota(jnp.int32, (t_chunk, t_chunk), 0)
        jj = lax.broadcasted_iota(jnp.int32, (t_chunk, t_chunk), 1)
        mask = (jj // CHUNK) <= (ii // CHUNK)
        for grp in range(MLP_GROUPS):
            wg = jnp.where(mask, ws_ref[grp], 0.0).astype(BF16)
            bg = bs_ref[grp]
            cs = slice(grp * MLP_GROUP_DIM, (grp + 1) * MLP_GROUP_DIM)
            for c in range(rows // t_chunk):
                rs = slice(c * t_chunk, (c + 1) * t_chunk)
                mixed = jnp.dot(wg, gvb[rs, cs], preferred_element_type=F32) + bg
                m_ref[rs, cs] = (u_s[rs, cs] * mixed).astype(BF16)

    @pl.when(j == 2)
    def _():
        q_ref[...] = (z * (HEAD_DIM ** -0.5 * LOG2E)).astype(BF16)

    @pl.when(j == 3)
    def _():
        store_per_head(k_ref, z)
        kb_ref[...] = z.astype(BF16)

    @pl.when(j == 4)
    def _():
        store_per_head(v_ref, z)
        if v_transposed:
            vb_ref[0] = z.T.astype(BF16)
        else:
            vb_ref[...] = z.astype(BF16)


def _mixer_in(x, shift, scale, g_mix, w_in_b, ln_g, ln_b, w_s, b_s, *, nb_blk, r_blk, t_chunk, emit_gv,
              v_transposed):
    nbat, r, d = x.shape
    width = 1024
    nr = r // r_blk
    n_tiles = (nbat // nb_blk) * nr
    rows = nb_blk * r_blk
    tokens = nbat * r
    row_idx = lambda i, j: (i, 0)
    flat = jax.ShapeDtypeStruct((tokens, width), BF16)
    per_head = jax.ShapeDtypeStruct((tokens * N_HEADS, LANES), F32)
    flat_spec = pl.BlockSpec((rows, width), row_idx)
    per_head_spec = pl.BlockSpec((rows * N_HEADS, LANES), row_idx)
    out_shape = [flat, flat, per_head, flat, per_head, flat]
    out_specs = [flat_spec, flat_spec, per_head_spec, flat_spec, per_head_spec, flat_spec]
    if v_transposed:
        assert nb_blk == 1
        out_shape[5] = jax.ShapeDtypeStruct((nbat, width, r), BF16)
        out_specs[5] = pl.BlockSpec((1, width, r_blk), lambda i, j: (i // nr, 0, i % nr))
    if emit_gv:
        out_shape.append(per_head)
        out_specs.append(per_head_spec)
    ws_t = w_s[:, :t_chunk, :t_chunk]
    bs_t = b_s[:, :t_chunk, None]
    return pl.pallas_call(
        functools.partial(_mixer_in_kernel, t_chunk=t_chunk, emit_gv=emit_gv, v_transposed=v_transposed),
        grid=(n_tiles, 5),
        in_specs=[
            pl.BlockSpec((nb_blk, r_blk, d), lambda i, j: (i // nr, i % nr, 0)),
            pl.BlockSpec((nb_blk, 1, d), lambda i, j: (i // nr, 0, 0)),
            pl.BlockSpec((nb_blk, 1, d), lambda i, j: (i // nr, 0, 0)),
            pl.BlockSpec((1, 1, d), lambda i, j: (0, 0, 0)),
            pl.BlockSpec((d, width), lambda i, j: (0, j)),
            pl.BlockSpec((1, width), lambda i, j: (0, 0)),
            pl.BlockSpec((1, width), lambda i, j: (0, 0)),
            pl.BlockSpec((MLP_GROUPS, t_chunk, t_chunk), lambda i, j: (0, 0, 0)),
            pl.BlockSpec((MLP_GROUPS, t_chunk, 1), lambda i, j: (0, 0, 0)),
        ],
        out_specs=out_specs,
        out_shape=out_shape,
        scratch_shapes=[pltpu.VMEM((rows, d), BF16), pltpu.VMEM((rows, width), F32)],
        compiler_params=_params(("arbitrary", "arbitrary")),
        name="mixer_in",
    )(x, shift[:, None, :], scale[:, None, :], g_mix.reshape(1, 1, d), w_in_b,
      ln_g.reshape(1, width), ln_b.reshape(1, width), ws_t, bs_t)


def _split_q(q):
    lane = lax.broadcasted_iota(jnp.int32, q.shape, 1)
    zero = jnp.zeros_like(q)
    return jnp.concatenate([jnp.where(lane < HEAD_DIM, q, zero), jnp.where(lane >= HEAD_DIM, q, zero)], axis=0)


def _softmax_step(s, v, m, l, acc):
    m_new = jnp.maximum(m, jnp.max(s, axis=-1, keepdims=True))
    alpha = jnp.exp2(m - m_new)
    p = jnp.exp2(s - m_new)
    l_new = alpha * l + jnp.sum(p, axis=-1, keepdims=True)
    acc_new = alpha * acc + jnp.dot(p.astype(BF16), v, preferred_element_type=F32)
    return m_new, l_new, acc_new


def _diff_lambda(lam_ref, lam_init):
    lv = lam_ref[...]
    s1 = jnp.sum(lv[0:1] * lv[1:2], axis=-1, keepdims=True)
    s2 = jnp.sum(lv[2:3] * lv[3:4], axis=-1, keepdims=True)
    return jnp.exp(s1) - jnp.exp(s2) + lam_init


def _finish_heads(l, acc, lam, subg, lam_init, n):
    o = acc / l
    a = o[:n] - lam * o[n:]
    a = a * lax.rsqrt(jnp.mean(a * a, axis=-1, keepdims=True) + EPS)
    return a * subg * (1.0 - lam_init)


def _qk(qs, k):
    return lax.dot_general(qs, k, (((1,), (1,)), ((), ())), preferred_element_type=F32)


def _attn_prompt_kernel(q_ref, k_ref, vt_ref, bias_ref, lam_ref, subg_ref, o_ref,
                        s_s, p_s, alpha_s, m_s, acc_s, *, blk, lam_init):
    qi = pl.program_id(2)
    halves = blk // ATTN_STRIP
    strips = [(c, h) for c in range(2) for h in range(halves)]
    q = q_ref[0]
    lane = lax.broadcasted_iota(jnp.int32, q.shape, 1)
    zero = jnp.zeros_like(q)
    qs = (jnp.where(lane < HEAD_DIM, q, zero), jnp.where(lane >= HEAD_DIM, q, zero))

    ones = jnp.ones((ONES_ROWS, blk), BF16)

    def beat(acc=None, logit=None, soft=None):
        if acc is not None:
            vt = vt_ref[0, :, pl.ds(pl.multiple_of(acc[0] * blk, blk), blk)]
            vt1 = jnp.concatenate([vt, ones], axis=0)
        if logit is not None:
            k = k_ref[0, pl.ds(pl.multiple_of(logit[0] * blk, blk), blk), :]
        for n, (c, h) in enumerate(strips):
            cols = slice(h * ATTN_STRIP, (h + 1) * ATTN_STRIP)
            if acc is not None:
                slot = acc[1]
                acc_s[n] = alpha_s[slot, n] * acc_s[n] + jnp.dot(vt1, p_s[slot, n], preferred_element_type=F32)
            if logit is not None:
                s_s[logit[1], n] = lax.dot_general(k, qs[c][cols], (((1,), (1,)), ((), ())),
                                                   preferred_element_type=F32)
            if soft is not None:
                slot, bias = soft
                s = s_s[slot, n]
                if bias is not None:
                    s = s + (bias if bias.ndim == 0 else bias[:, cols])
                m_old = m_s[n]
                m_new = jnp.maximum(m_old, jnp.max(s, axis=0, keepdims=True))
                m_s[n] = m_new
                alpha_s[slot, n] = jnp.exp2(m_old - m_new)
                p_s[slot, n] = jnp.exp2(s - m_new).astype(BF16)

    m_s[...] = jnp.full(m_s.shape, MASK_VALUE, F32)
    acc_s[...] = jnp.zeros(acc_s.shape, F32)
    alpha_s[1] = jnp.ones(alpha_s.shape[1:], F32)
    p_s[1] = jnp.zeros(p_s.shape[1:], BF16)
    beat(logit=(0, 0))

    n_far = jnp.maximum(qi - 1, 0)
    n_pairs = n_far // 2

    @pl.loop(0, n_pairs)
    def _(t):
        j = 2 * t
        beat(acc=(jnp.maximum(j - 1, 0), 1), logit=(j + 1, 1), soft=(0, None))
        beat(acc=(j, 0), logit=(j + 2, 0), soft=(1, None))

    x = 2 * n_pairs
    bias_x = jnp.where(n_far % 2 == 1, 0.0, MASK_VALUE)
    bias_a = jnp.where(qi == 0, bias_ref[0, blk:, :], bias_ref[0, :blk, :])
    bias_b = jnp.where(qi == 0, MASK_VALUE, bias_ref[0, blk:, :])
    beat(acc=(jnp.maximum(x - 1, 0), 1), logit=(n_far, 1), soft=(0, bias_x))
    beat(acc=(x, 0), logit=(n_far + 1, 0), soft=(1, bias_a))
    beat(acc=(n_far, 1), soft=(0, bias_b))
    beat(acc=(n_far + 1, 0))

    lam = _diff_lambda(lam_ref, lam_init)
    for h in range(halves):
        o = []
        for c in range(2):
            acc = acc_s[c * halves + h]
            o.append(acc[:V_HEAD_DIM] / acc[V_HEAD_DIM:V_HEAD_DIM + 1])
        a = o[0] - lam * o[1]
        a = a * lax.rsqrt(jnp.mean(a * a, axis=0, keepdims=True) + EPS)
        a = a * subg_ref[...] * (1.0 - lam_init)
        o_ref[0, h * ATTN_STRIP:(h + 1) * ATTN_STRIP, :] = a.T.astype(o_ref.dtype)


def _attn_prompt(q, k, vt, bias, lam_vecs, sub_g_col, *, blk, lam_init):
    b, s, _ = q.shape
    n_strips = 2 * (blk // ATTN_STRIP)
    return pl.pallas_call(
        functools.partial(_attn_prompt_kernel, blk=blk, lam_init=lam_init),
        grid=(b, N_HEADS, s // blk),
        in_specs=[
            pl.BlockSpec((1, blk, LANES), lambda bi, h, qi: (bi, qi, h)),
            pl.BlockSpec((1, s, LANES), lambda bi, h, qi: (bi, 0, h)),
            pl.BlockSpec((1, V_HEAD_DIM, s), lambda bi, h, qi: (bi, h, 0)),
            pl.BlockSpec((1, 2 * blk, blk), lambda bi, h, qi: (h, 0, 0)),
            pl.BlockSpec((4, HEAD_DIM), lambda bi, h, qi: (0, 0)),
            pl.BlockSpec((V_HEAD_DIM, 1), lambda bi, h, qi: (0, 0)),
        ],
        out_specs=pl.BlockSpec((1, blk, LANES), lambda bi, h, qi: (bi, qi, h)),
        out_shape=jax.ShapeDtypeStruct((b, s, N_HEADS * V_HEAD_DIM), BF16),
        scratch_shapes=[
            pltpu.VMEM((2, n_strips, blk, ATTN_STRIP), F32),
            pltpu.VMEM((2, n_strips, blk, ATTN_STRIP), BF16),
            pltpu.VMEM((2, n_strips, 1, ATTN_STRIP), F32),
            pltpu.VMEM((n_strips, 1, ATTN_STRIP), F32),
            pltpu.VMEM((n_strips, V_HEAD_DIM + ONES_ROWS, ATTN_STRIP), F32),
        ],
        compiler_params=_params(("arbitrary", "arbitrary", "arbitrary")),
        name="attn_prompt",
    )(q, k, vt, bias, lam_vecs, sub_g_col)


def _attn_sample_kernel(q_ref, ck_ref, cv_ref, nk_ref, nv_ref, bias_ref, lam_ref, subg_ref, o_ref,
                        *, past, lam_init):
    t = q_ref.shape[1]
    lam = _diff_lambda(lam_ref, lam_init)
    for hd in range(N_HEADS):
        cs = slice(hd * LANES, (hd + 1) * LANES)
        qs = _split_q(q_ref[0, :, cs])
        bias = bias_ref[hd]
        bias2 = jnp.concatenate([bias, bias], axis=0)
        carry = (jnp.full((2 * t, 1), MASK_VALUE, F32), jnp.zeros((2 * t, 1), F32),
                 jnp.zeros((2 * t, V_HEAD_DIM), F32))
        kc = ck_ref[0, pl.ds(hd, past, stride=N_HEADS), :].astype(BF16)
        vc = cv_ref[0, pl.ds(hd, past, stride=N_HEADS), :].astype(BF16)
        carry = _softmax_step(_qk(qs, kc) + bias2[:, :past], vc, *carry)
        m, l, acc = _softmax_step(_qk(qs, nk_ref[0, :, cs]) + bias2[:, past:], nv_ref[0, :, cs], *carry)
        o_ref[0, :, cs] = _finish_heads(l, acc, lam, subg_ref[...], lam_init, t).astype(o_ref.dtype)


def _attn_sample(q, cache_k, cache_v, new_k, new_v, bias, lam_vecs, sub_g, *, lam_init):
    b, t, width = q.shape
    past = cache_k.shape[1] // N_HEADS
    flat_spec = pl.BlockSpec((1, t, width), lambda bi: (bi, 0, 0))
    cache_spec = pl.BlockSpec((1, past * N_HEADS, LANES), lambda bi: (bi, 0, 0))
    return pl.pallas_call(
        functools.partial(_attn_sample_kernel, past=past, lam_init=lam_init),
        grid=(b,),
        in_specs=[
            flat_spec, cache_spec, cache_spec, flat_spec, flat_spec,
            pl.BlockSpec((N_HEADS, t, past + t), lambda bi: (0, 0, 0)),
            pl.BlockSpec((4, HEAD_DIM), lambda bi: (0, 0)),
            pl.BlockSpec((1, V_HEAD_DIM), lambda bi: (0, 0)),
        ],
        out_specs=flat_spec,
        out_shape=jax.ShapeDtypeStruct((b, t, width), BF16),
        compiler_params=_params(("arbitrary",)),
        name="attn_sample",
    )(q, cache_k, cache_v, new_k, new_v, bias, lam_vecs, sub_g)


def _mixer_out_kernel(x_ref, m_ref, a_ref, gt_ref, w_ref, o_ref):
    half = m_ref.shape[-1]
    y = jnp.dot(m_ref[...], w_ref[:half, :], preferred_element_type=F32)
    y = y + jnp.dot(a_ref[...], w_ref[half:, :], preferred_element_type=F32)
    nb, r, d = x_ref.shape
    o_ref[...] = x_ref[...] + gt_ref[...] * y.reshape(nb, r, d)


def _mixer_out(x, m, a, gate, w_out_b, *, nb_blk, r_blk):
    nbat, r, d = x.shape
    nr = r // r_blk
    rows = nb_blk * r_blk
    half = m.shape[-1]
    return pl.pallas_call(
        _mixer_out_kernel,
        grid=((nbat // nb_blk) * nr,),
        in_specs=[
            pl.BlockSpec((nb_blk, r_blk, d), lambda i: (i // nr, i % nr, 0)),
            pl.BlockSpec((rows, half), lambda i: (i, 0)),
            pl.BlockSpec((rows, half), lambda i: (i, 0)),
            pl.BlockSpec((nb_blk, 1, d), lambda i: (i // nr, 0, 0)),
            pl.BlockSpec((d, d), lambda i: (0, 0)),
        ],
        out_specs=pl.BlockSpec((nb_blk, r_blk, d), lambda i: (i // nr, i % nr, 0)),
        out_shape=jax.ShapeDtypeStruct(x.shape, F32),
        compiler_params=_params(("arbitrary",)),
        name="mixer_out",
    )(x, m, a, gate[:, None, :], w_out_b)


def _ffn_kernel(x_ref, sh_ref, sc_ref, gt_ref, g_ref, wg_ref, wu_ref, wo_ref, gf_ref, shf_ref, scf_ref,
                o_ref, h_s, acc_s):
    f = pl.program_id(1)
    rows = h_s.shape[0]

    @pl.when(f == 0)
    def _():
        x = x_ref[...]
        y = x * lax.rsqrt(jnp.mean(x * x, axis=-1, keepdims=True) + EPS)
        hm = (y * g_ref[...]) * (1.0 + sc_ref[...]) + sh_ref[...]
        h_s[...] = hm.reshape(rows, hm.shape[-1]).astype(BF16)
        acc_s[...] = jnp.zeros_like(acc_s)

    h = h_s[...]
    zg = jnp.dot(h, wg_ref[...], preferred_element_type=F32)
    zu = jnp.dot(h, wu_ref[...], preferred_element_type=F32)
    act = (zg * jax.nn.sigmoid(zg) * zu).astype(BF16)
    acc_s[...] += jnp.dot(act, wo_ref[...], preferred_element_type=F32)

    @pl.when(f == pl.num_programs(1) - 1)
    def _():
        nb, r, d = x_ref.shape
        x2 = x_ref[...] + gt_ref[...] * acc_s[...].reshape(nb, r, d)
        y = x2 * lax.rsqrt(jnp.mean(x2 * x2, axis=-1, keepdims=True) + EPS)
        o_ref[...] = (y * gf_ref[...]) * (1.0 + scf_ref[...]) + shf_ref[...]


def _ffn(x, shift, scale, gate, g_ffn, w_in_b, w_out_b, g_final, shift_f, scale_f, *, nb_blk, r_blk, tf):
    nbat, r, d = x.shape
    d_ff = w_out_b.shape[0]
    nf = d_ff // tf
    nr = r // r_blk
    rows = nb_blk * r_blk
    x_spec = pl.BlockSpec((nb_blk, r_blk, d), lambda i, f: (i // nr, i % nr, 0))
    vec_spec = pl.BlockSpec((nb_blk, 1, d), lambda i, f: (i // nr, 0, 0))
    par_spec = pl.BlockSpec((1, 1, d), lambda i, f: (0, 0, 0))
    return pl.pallas_call(
        _ffn_kernel,
        grid=((nbat // nb_blk) * nr, nf),
        in_specs=[
            x_spec, vec_spec, vec_spec, vec_spec, par_spec,
            pl.BlockSpec((d, tf), lambda i, f: (0, f)),
            pl.BlockSpec((d, tf), lambda i, f: (0, f + nf)),
            pl.BlockSpec((tf, d), lambda i, f: (f, 0)),
            par_spec, vec_spec, vec_spec,
        ],
        out_specs=x_spec,
        out_shape=jax.ShapeDtypeStruct(x.shape, F32),
        scratch_shapes=[pltpu.VMEM((rows, d), BF16), pltpu.VMEM((rows, d), F32)],
        compiler_params=_params(("arbitrary", "arbitrary")),
        name="ffn",
    )(x, shift[:, None, :], scale[:, None, :], gate[:, None, :], g_ffn.reshape(1, 1, d),
      w_in_b, w_in_b, w_out_b, g_final.reshape(1, 1, d), shift_f[:, None, :], scale_f[:, None, :])


ATTN_BLOCK = 512
PROMPT_ROWS = 512
SAMPLE_BATCH_BLOCK = 8
FFN_TILE = 512


def kernel(x_prompt, x_sample, cache_k, cache_v, c_prompt, c_sample, rel_bias, w_ada, b_ada, w_ada_final,
           b_ada_final, g_mix, g_ffn, g_final, w_in, mlp_ln_g, mlp_ln_b, w_s, b_s, lambda_q1, lambda_k1,
           lambda_q2, lambda_k2, sub_g, w_out, w_ffn_in, w_ffn_out):
    B, S, D = x_prompt.shape
    DB, T, _ = x_sample.shape
    depth = w_in.shape[0]
    past = cache_k.shape[2]
    width = N_HEADS * V_HEAD_DIM
    mlp_chunk = w_s.shape[-1]

    c_all = jnp.concatenate([c_prompt, c_sample], axis=0)
    mod_f = _adaln(c_all, w_ada_final, b_ada_final)
    bias_p = _rel_bias_tiles(rel_bias, ATTN_BLOCK, 2 * ATTN_BLOCK, ATTN_BLOCK, 0, True, True)
    bias_s = _rel_bias_tiles(rel_bias, T, past + T, past, 0, False, False)

    assert depth == 1, "the final adaLN norm is fused into the single layer's FFN kernel"
    lam_init = 0.8 - 0.6 * math.exp(-0.3 * 0)
    lam_vecs = jnp.stack([lambda_q1[0], lambda_k1[0], lambda_q2[0], lambda_k2[0]])
    subg = sub_g.reshape(1, V_HEAD_DIM)
    mod = _adaln(c_all, w_ada[0], b_ada[0])
    sh1, sc1, gt1, sh2, sc2, gt2 = jnp.split(mod, 6, axis=-1)
    shf, scf = jnp.split(mod_f, 2, axis=-1)
    w_in_b = w_in[0].astype(BF16)
    w_out_b = w_out[0].astype(BF16)
    w_f_in_b = w_ffn_in[0].astype(BF16)
    w_f_out_b = w_ffn_out[0].astype(BF16)
    mixer_w = (g_mix[0], w_in_b, mlp_ln_g[0], mlp_ln_b[0], w_s[0], b_s[0])

    m, q, kp, kb, vp, vt = _mixer_in(x_prompt, sh1[:B], sc1[:B], *mixer_w, nb_blk=1, r_blk=PROMPT_ROWS,
                                     t_chunk=mlp_chunk, emit_gv=False, v_transposed=True)
    a = _attn_prompt(q.reshape(B, S, width), kb.reshape(B, S, width), vt, bias_p, lam_vecs,
                     subg.reshape(V_HEAD_DIM, 1), blk=ATTN_BLOCK, lam_init=lam_init)
    xp = _mixer_out(x_prompt, m, a.reshape(B * S, width), gt1[:B], w_out_b, nb_blk=1, r_blk=PROMPT_ROWS)
    yp = _ffn(xp, sh2[:B], sc2[:B], gt2[:B], g_ffn[0], w_f_in_b, w_f_out_b, g_final, shf[:B], scf[:B],
              nb_blk=1, r_blk=PROMPT_ROWS, tf=FFN_TILE)

    m, q, ks, kb, vs, vb, gvs = _mixer_in(x_sample, sh1[B:], sc1[B:], *mixer_w, nb_blk=SAMPLE_BATCH_BLOCK,
                                          r_blk=T, t_chunk=T, emit_gv=True, v_transposed=False)
    a = _attn_sample(q.reshape(DB, T, width), cache_k.reshape(DB, past * N_HEADS, LANES),
                     cache_v.reshape(DB, past * N_HEADS, LANES), kb.reshape(DB, T, width),
                     vb.reshape(DB, T, width), bias_s, lam_vecs, subg, lam_init=lam_init)
    xs = _mixer_out(x_sample, m, a.reshape(DB * T, width), gt1[B:], w_out_b, nb_blk=SAMPLE_BATCH_BLOCK, r_blk=T)
    ys = _ffn(xs, sh2[B:], sc2[B:], gt2[B:], g_ffn[0], w_f_in_b, w_f_out_b, g_final, shf[B:], scf[B:],
              nb_blk=SAMPLE_BATCH_BLOCK, r_blk=T, tf=FFN_TILE)

    head_shape = (N_HEADS, V_HEAD_DIM)
    return (yp, ys, kp.reshape(1, B, S, *head_shape), vp.reshape(1, B, S, *head_shape),
            ks.reshape(1, DB, T, *head_shape), vs.reshape(1, DB, T, *head_shape),
            gvs.reshape(1, DB, T, MLP_GROUPS, MLP_GROUP_DIM))
```

```python
import functools
import math

import jax
import jax.numpy as jnp
from jax import lax
from jax.experimental import pallas as pl
from jax.experimental.pallas import tpu as pltpu

LANES = 128
SUBLANES = 8
VMEM_LIMIT_BYTES = 56 * 1024 * 1024
MXU_WIDTH = 256

ATTN_STRIP = MXU_WIDTH
ONES_ROWS = SUBLANES

CHUNK = 64
N_HEADS = 8
HEAD_DIM = 64
V_HEAD_DIM = 128
MLP_GROUPS = 8
MLP_GROUP_DIM = 128
N_BUCKETS = 32
MAX_DISTANCE = 128
EPS = 1e-6
MASK_VALUE = -1e30
LOG2E = math.log2(math.e)

BF16 = jnp.bfloat16
F32 = jnp.float32


def _params(sem):
    return pltpu.CompilerParams(dimension_semantics=sem, vmem_limit_bytes=VMEM_LIMIT_BYTES)


def _adaln_kernel(c_ref, w_ref, b_ref, o_ref):
    c = c_ref[...]
    a = c * jax.nn.sigmoid(c)
    o_ref[...] = jnp.dot(a, w_ref[...], preferred_element_type=F32) + b_ref[...]


def _adaln(c, w, b, tn=1024):
    rows, d = c.shape
    n = w.shape[1]
    return pl.pallas_call(
        _adaln_kernel,
        grid=(n // tn,),
        in_specs=[
            pl.BlockSpec((rows, d), lambda j: (0, 0)),
            pl.BlockSpec((d, tn), lambda j: (0, j)),
            pl.BlockSpec((1, tn), lambda j: (0, j)),
        ],
        out_specs=pl.BlockSpec((rows, tn), lambda j: (0, j)),
        out_shape=jax.ShapeDtypeStruct((rows, n), F32),
        compiler_params=_params(("arbitrary",)),
        name="adaln",
    )(c, w, b.reshape(1, n))


def _rel_bias_kernel(tab_ref, o_ref, *, nq, nk, q_start, k_start, shift_far, keys_on_rows):
    h = pl.program_id(0)
    shape, q_axis, k_axis = ((nk, nq), 1, 0) if keys_on_rows else ((nq, nk), 0, 1)
    q_pos = q_start + lax.broadcasted_iota(jnp.int32, shape, q_axis)
    k_pos = k_start + lax.broadcasted_iota(jnp.int32, shape, k_axis)
    rel = k_pos - q_pos
    nb = N_BUCKETS // 2
    max_exact = nb // 2
    ret = jnp.where(rel > 0, nb, 0)
    n = jnp.abs(rel)
    nf = jnp.maximum(n, 1).astype(F32)
    large = max_exact + (jnp.log(nf / max_exact) / math.log(MAX_DISTANCE / max_exact)
                         * (nb - max_exact)).astype(jnp.int32)
    large = jnp.minimum(large, nb - 1)
    bucket = ret + jnp.where(n < max_exact, n, large)
    bias = jnp.zeros(shape, F32)
    for bkt in range(N_BUCKETS):
        bias = jnp.where(bucket == bkt, tab_ref[bkt, h], bias)
    if shift_far:
        bias = bias - tab_ref[nb - 1, h]
    allowed = (k_pos // CHUNK) <= (q_pos // CHUNK)
    o_ref[0] = jnp.where(allowed, bias * LOG2E, MASK_VALUE)


def _rel_bias_tiles(rel_bias, nq, nk, q_start, k_start, shift_far, keys_on_rows):
    out_tile = (nk, nq) if keys_on_rows else (nq, nk)
    return pl.pallas_call(
        functools.partial(_rel_bias_kernel, nq=nq, nk=nk, q_start=q_start, k_start=k_start,
                          shift_far=shift_far, keys_on_rows=keys_on_rows),
        grid=(N_HEADS,),
        in_specs=[pl.BlockSpec(memory_space=pltpu.SMEM)],
        out_specs=pl.BlockSpec((1,) + out_tile, lambda h: (h, 0, 0)),
        out_shape=jax.ShapeDtypeStruct((N_HEADS,) + out_tile, F32),
        compiler_params=_params(("arbitrary",)),
        name="rel_bias",
    )(rel_bias)


def _mixer_in_kernel(x_ref, sh_ref, sc_ref, g_ref, w_ref, lng_ref, lnb_ref, ws_ref, bs_ref,
                     *refs, t_chunk, emit_gv, v_transposed):
    if emit_gv:
        m_ref, q_ref, k_ref, kb_ref, v_ref, vb_ref, gv_ref, h_s, u_s = refs
    else:
        m_ref, q_ref, k_ref, kb_ref, v_ref, vb_ref, h_s, u_s = refs
        gv_ref = None
    j = pl.program_id(1)
    rows = h_s.shape[0]

    def store_per_head(ref, val):
        for hd in range(N_HEADS):
            ref[pl.ds(hd, rows, stride=N_HEADS), :] = val[:, hd * LANES:(hd + 1) * LANES]

    @pl.when(j == 0)
    def _():
        x = x_ref[...]
        y = x * lax.rsqrt(jnp.mean(x * x, axis=-1, keepdims=True) + EPS)
        y = y * g_ref[...]
        hm = y * (1.0 + sc_ref[...]) + sh_ref[...]
        h_s[...] = hm.reshape(rows, hm.shape[-1]).astype(BF16)

    z = jnp.dot(h_s[...], w_ref[...], preferred_element_type=F32)

    @pl.when(j == 0)
    def _():
        u_s[...] = jax.nn.gelu(z)

    @pl.when(j == 1)
    def _():
        g = jax.nn.gelu(z)
        mu = jnp.mean(g, axis=-1, keepdims=True)
        var = jnp.mean(jnp.square(g - mu), axis=-1, keepdims=True)
        gv = (g - mu) * lax.rsqrt(var + EPS) * lng_ref[...] + lnb_ref[...]
        if emit_gv:
            store_per_head(gv_ref, gv)
        gvb = gv.astype(BF16)
        ii = lax.broadcasted_iota(jnp.int32, (t_chunk, t_chunk), 0)
        jj = lax.broadcasted_iota(jnp.int32, (t_chunk, t_chunk), 1)
        mask = (jj // CHUNK) <= (ii // CHUNK)
        for grp in range(MLP_GROUPS):
            wg = jnp.where(mask, ws_ref[grp], 0.0).astype(BF16)
            bg = bs_ref[grp]
            cs = slice(grp * MLP_GROUP_DIM, (grp + 1) * MLP_GROUP_DIM)
            for c in range(rows // t_chunk):
                rs = slice(c * t_chunk, (c + 1) * t_chunk)
                mixed = jnp.dot(wg, gvb[rs, cs], preferred_element_type=F32) + bg
                m_ref[rs, cs] = (u_s[rs, cs] * mixed).astype(BF16)

    @pl.when(j == 2)
    def _():
        q_ref[...] = (z * (HEAD_DIM ** -0.5 * LOG2E)).astype(BF16)

    @pl.when(j == 3)
    def _():
        store_per_head(k_ref, z)
        kb_ref[...] = z.astype(BF16)

    @pl.when(j == 4)
    def _():
        store_per_head(v_ref, z)
        if v_transposed:
            vb_ref[0] = z.T.astype(BF16)
        else:
            vb_ref[...] = z.astype(BF16)


def _mixer_in(x, shift, scale, g_mix, w_in_b, ln_g, ln_b, w_s, b_s, *, nb_blk, r_blk, t_chunk, emit_gv,
              v_transposed):
    nbat, r, d = x.shape
    width = 1024
    nr = r // r_blk
    n_tiles = (nbat // nb_blk) * nr
    rows = nb_blk * r_blk
    tokens = nbat * r
    row_idx = lambda i, j: (i, 0)
    flat = jax.ShapeDtypeStruct((tokens, width), BF16)
    per_head = jax.ShapeDtypeStruct((tokens * N_HEADS, LANES), F32)
    flat_spec = pl.BlockSpec((rows, width), row_idx)
    per_head_spec = pl.BlockSpec((rows * N_HEADS, LANES), row_idx)
    out_shape = [flat, flat, per_head, flat, per_head, flat]
    out_specs = [flat_spec, flat_spec, per_head_spec, flat_spec, per_head_spec, flat_spec]
    if v_transposed:
        assert nb_blk == 1
        out_shape[5] = jax.ShapeDtypeStruct((nbat, width, r), BF16)
        out_specs[5] = pl.BlockSpec((1, width, r_blk), lambda i, j: (i // nr, 0, i % nr))
    if emit_gv:
        out_shape.append(per_head)
        out_specs.append(per_head_spec)
    ws_t = w_s[:, :t_chunk, :t_chunk]
    bs_t = b_s[:, :t_chunk, None]
    return pl.pallas_call(
        functools.partial(_mixer_in_kernel, t_chunk=t_chunk, emit_gv=emit_gv, v_transposed=v_transposed),
        grid=(n_tiles, 5),
        in_specs=[
            pl.BlockSpec((nb_blk, r_blk, d), lambda i, j: (i // nr, i % nr, 0)),
            pl.BlockSpec((nb_blk, 1, d), lambda i, j: (i // nr, 0, 0)),
            pl.BlockSpec((nb_blk, 1, d), lambda i, j: (i // nr, 0, 0)),
            pl.BlockSpec((1, 1, d), lambda i, j: (0, 0, 0)),
            pl.BlockSpec((d, width), lambda i, j: (0, j)),
            pl.BlockSpec((1, width), lambda i, j: (0, 0)),
            pl.BlockSpec((1, width), lambda i, j: (0, 0)),
            pl.BlockSpec((MLP_GROUPS, t_chunk, t_chunk), lambda i, j: (0, 0, 0)),
            pl.BlockSpec((MLP_GROUPS, t_chunk, 1), lambda i, j: (0, 0, 0)),
        ],
        out_specs=out_specs,
        out_shape=out_shape,
        scratch_shapes=[pltpu.VMEM((rows, d), BF16), pltpu.VMEM((rows, width), F32)],
        compiler_params=_params(("arbitrary", "arbitrary")),
        name="mixer_in",
    )(x, shift[:, None, :], scale[:, None, :], g_mix.reshape(1, 1, d), w_in_b,
      ln_g.reshape(1, width), ln_b.reshape(1, width), ws_t, bs_t)


def _split_q(q):
    lane = lax.broadcasted_iota(jnp.int32, q.shape, 1)
    zero = jnp.zeros_like(q)
    return jnp.concatenate([jnp.where(lane < HEAD_DIM, q, zero), jnp.where(lane >= HEAD_DIM, q, zero)], axis=0)


def _softmax_step(s, v, m, l, acc):
    m_new = jnp.maximum(m, jnp.max(s, axis=-1, keepdims=True))
    alpha = jnp.exp2(m - m_new)
    p = jnp.exp2(s - m_new)
    l_new = alpha * l + jnp.sum(p, axis=-1, keepdims=True)
    acc_new = alpha * acc + jnp.dot(p.astype(BF16), v, preferred_element_type=F32)
    return m_new, l_new, acc_new


def _diff_lambda(lam_ref, lam_init):
    lv = lam_ref[...]
    s1 = jnp.sum(lv[0:1] * lv[1:2], axis=-1, keepdims=True)
    s2 = jnp.sum(lv[2:3] * lv[3:4], axis=-1, keepdims=True)
    return jnp.exp(s1) - jnp.exp(s2) + lam_init


def _finish_heads(l, acc, lam, subg, lam_init, n):
    o = acc / l
    a = o[:n] - lam * o[n:]
    a = a * lax.rsqrt(jnp.mean(a * a, axis=-1, keepdims=True) + EPS)
    return a * subg * (1.0 - lam_init)


def _qk(qs, k):
    return lax.dot_general(qs, k, (((1,), (1,)), ((), ())), preferred_element_type=F32)


def _attn_prompt_kernel(q_ref, k_ref, vt_ref, bias_ref, lam_ref, subg_ref, o_ref,
                        s_s, p_s, alpha_s, m_s, acc_s, *, bq, bk, lam_init):
    qi = pl.program_id(2)
    halves = bq // ATTN_STRIP
    strips = [(c, h) for c in range(2) for h in range(halves)]
    n_near = bq // bk + 1
    q = q_ref[0]
    lane = lax.broadcasted_iota(jnp.int32, q.shape, 1)
    zero = jnp.zeros_like(q)
    qs = (jnp.where(lane < HEAD_DIM, q, zero), jnp.where(lane >= HEAD_DIM, q, zero))

    ones = jnp.ones((ONES_ROWS, bk), BF16)

    def beat(acc=None, logit=None, soft=None):
        if acc is not None:
            vt = vt_ref[0, :, pl.ds(pl.multiple_of(acc[0] * bk, bk), bk)]
            vt1 = jnp.concatenate([vt, ones], axis=0)
        if logit is not None:
            k = k_ref[0, pl.ds(pl.multiple_of(logit[0] * bk, bk), bk), :]
        for n, (c, h) in enumerate(strips):
            cols = slice(h * ATTN_STRIP, (h + 1) * ATTN_STRIP)
            if acc is not None:
                slot = acc[1]
                acc_s[n] = alpha_s[slot, n] * acc_s[n] + jnp.dot(vt1, p_s[slot, n], preferred_element_type=F32)
            if logit is not None:
                s_s[logit[1], n] = lax.dot_general(k, qs[c][cols], (((1,), (1,)), ((), ())),
                                                   preferred_element_type=F32)
            if soft is not None:
                slot, bias = soft
                s = s_s[slot, n]
                if bias is not None:
                    s = s + (bias if bias.ndim == 0 else bias[:, cols])
                m_old = m_s[n]
                m_new = jnp.maximum(m_old, jnp.max(s, axis=0, keepdims=True))
                m_s[n] = m_new
                alpha_s[slot, n] = jnp.exp2(m_old - m_new)
                p_s[slot, n] = jnp.exp2(s - m_new).astype(BF16)

    m_s[...] = jnp.full(m_s.shape, MASK_VALUE, F32)
    acc_s[...] = jnp.zeros(acc_s.shape, F32)
    alpha_s[1] = jnp.ones(alpha_s.shape[1:], F32)
    p_s[1] = jnp.zeros(p_s.shape[1:], BF16)
    beat(logit=(0, 0))

    n_far = jnp.maximum((qi * bq - MAX_DISTANCE) // bk, 0)
    n_pairs = n_far // 2

    @pl.loop(0, n_pairs)
    def _(t):
        j = 2 * t
        beat(acc=(jnp.maximum(j - 1, 0), 1), logit=(j + 1, 1), soft=(0, None))
        beat(acc=(j, 0), logit=(j + 2, 0), soft=(1, None))

    x = 2 * n_pairs
    tail = [(x, jnp.where(n_far % 2 == 1, 0.0, MASK_VALUE))]
    for i in range(n_near):
        shifted = bias_ref[0, (i + 1) * bk:(i + 2) * bk, :] if i + 1 < n_near else MASK_VALUE
        tail.append((n_far + i, jnp.where(qi == 0, shifted, bias_ref[0, i * bk:(i + 1) * bk, :])))
    for t, (_, bias_t) in enumerate(tail):
        prev = (jnp.maximum(x - 1, 0), 1) if t == 0 else (tail[t - 1][0], (t - 1) % 2)
        nxt = (tail[t + 1][0], (t + 1) % 2) if t + 1 < len(tail) else None
        beat(acc=prev, logit=nxt, soft=(t % 2, bias_t))
    beat(acc=(tail[-1][0], (len(tail) - 1) % 2))

    lam = _diff_lambda(lam_ref, lam_init)
    for h in range(halves):
        o = []
        for c in range(2):
            acc = acc_s[c * halves + h]
            o.append(acc[:V_HEAD_DIM] / acc[V_HEAD_DIM:V_HEAD_DIM + 1])
        a = o[0] - lam * o[1]
        a = a * lax.rsqrt(jnp.mean(a * a, axis=0, keepdims=True) + EPS)
        a = a * subg_ref[...] * (1.0 - lam_init)
        o_ref[0, h * ATTN_STRIP:(h + 1) * ATTN_STRIP, :] = a.T.astype(o_ref.dtype)


def _attn_prompt(q, k, vt, bias, lam_vecs, sub_g_col, *, bq, bk, lam_init):
    b, s, _ = q.shape
    n_strips = 2 * (bq // ATTN_STRIP)
    assert bq % bk == 0 and bk >= MAX_DISTANCE and s >= bq + bk
    return pl.pallas_call(
        functools.partial(_attn_prompt_kernel, bq=bq, bk=bk, lam_init=lam_init),
        grid=(b, N_HEADS, s // bq),
        in_specs=[
            pl.BlockSpec((1, bq, LANES), lambda bi, h, qi: (bi, qi, h)),
            pl.BlockSpec((1, s, LANES), lambda bi, h, qi: (bi, 0, h)),
            pl.BlockSpec((1, V_HEAD_DIM, s), lambda bi, h, qi: (bi, h, 0)),
            pl.BlockSpec((1, bq + bk, bq), lambda bi, h, qi: (h, 0, 0)),
            pl.BlockSpec((4, HEAD_DIM), lambda bi, h, qi: (0, 0)),
            pl.BlockSpec((V_HEAD_DIM, 1), lambda bi, h, qi: (0, 0)),
        ],
        out_specs=pl.BlockSpec((1, bq, LANES), lambda bi, h, qi: (bi, qi, h)),
        out_shape=jax.ShapeDtypeStruct((b, s, N_HEADS * V_HEAD_DIM), BF16),
        scratch_shapes=[
            pltpu.VMEM((2, n_strips, bk, ATTN_STRIP), F32),
            pltpu.VMEM((2, n_strips, bk, ATTN_STRIP), BF16),
            pltpu.VMEM((2, n_strips, 1, ATTN_STRIP), F32),
            pltpu.VMEM((n_strips, 1, ATTN_STRIP), F32),
            pltpu.VMEM((n_strips, V_HEAD_DIM + ONES_ROWS, ATTN_STRIP), F32),
        ],
        compiler_params=_params(("arbitrary", "arbitrary", "arbitrary")),
        name="attn_prompt",
    )(q, k, vt, bias, lam_vecs, sub_g_col)


def _attn_sample_kernel(q_ref, ck_ref, cv_ref, nk_ref, nv_ref, bias_ref, lam_ref, subg_ref, o_ref,
                        *, past, lam_init):
    t = q_ref.shape[1]
    lam = _diff_lambda(lam_ref, lam_init)
    for hd in range(N_HEADS):
        cs = slice(hd * LANES, (hd + 1) * LANES)
        qs = _split_q(q_ref[0, :, cs])
        bias = bias_ref[hd]
        bias2 = jnp.concatenate([bias, bias], axis=0)
        carry = (jnp.full((2 * t, 1), MASK_VALUE, F32), jnp.zeros((2 * t, 1), F32),
                 jnp.zeros((2 * t, V_HEAD_DIM), F32))
        kc = ck_ref[0, pl.ds(hd, past, stride=N_HEADS), :].astype(BF16)
        vc = cv_ref[0, pl.ds(hd, past, stride=N_HEADS), :].astype(BF16)
        carry = _softmax_step(_qk(qs, kc) + bias2[:, :past], vc, *carry)
        m, l, acc = _softmax_step(_qk(qs, nk_ref[0, :, cs]) + bias2[:, past:], nv_ref[0, :, cs], *carry)
        o_ref[0, :, cs] = _finish_heads(l, acc, lam, subg_ref[...], lam_init, t).astype(o_ref.dtype)


def _attn_sample(q, cache_k, cache_v, new_k, new_v, bias, lam_vecs, sub_g, *, lam_init):
    b, t, width = q.shape
    past = cache_k.shape[1] // N_HEADS
    flat_spec = pl.BlockSpec((1, t, width), lambda bi: (bi, 0, 0))
    cache_spec = pl.BlockSpec((1, past * N_HEADS, LANES), lambda bi: (bi, 0, 0))
    return pl.pallas_call(
        functools.partial(_attn_sample_kernel, past=past, lam_init=lam_init),
        grid=(b,),
        in_specs=[
            flat_spec, cache_spec, cache_spec, flat_spec, flat_spec,
            pl.BlockSpec((N_HEADS, t, past + t), lambda bi: (0, 0, 0)),
            pl.BlockSpec((4, HEAD_DIM), lambda bi: (0, 0)),
            pl.BlockSpec((1, V_HEAD_DIM), lambda bi: (0, 0)),
        ],
        out_specs=flat_spec,
        out_shape=jax.ShapeDtypeStruct((b, t, width), BF16),
        compiler_params=_params(("arbitrary",)),
        name="attn_sample",
    )(q, cache_k, cache_v, new_k, new_v, bias, lam_vecs, sub_g)


def _mixer_out_kernel(x_ref, m_ref, a_ref, gt_ref, w_ref, o_ref):
    half = m_ref.shape[-1]
    y = jnp.dot(m_ref[...], w_ref[:half, :], preferred_element_type=F32)
    y = y + jnp.dot(a_ref[...], w_ref[half:, :], preferred_element_type=F32)
    nb, r, d = x_ref.shape
    o_ref[...] = x_ref[...] + gt_ref[...] * y.reshape(nb, r, d)


def _mixer_out(x, m, a, gate, w_out_b, *, nb_blk, r_blk):
    nbat, r, d = x.shape
    nr = r // r_blk
    rows = nb_blk * r_blk
    half = m.shape[-1]
    return pl.pallas_call(
        _mixer_out_kernel,
        grid=((nbat // nb_blk) * nr,),
        in_specs=[
            pl.BlockSpec((nb_blk, r_blk, d), lambda i: (i // nr, i % nr, 0)),
            pl.BlockSpec((rows, half), lambda i: (i, 0)),
            pl.BlockSpec((rows, half), lambda i: (i, 0)),
            pl.BlockSpec((nb_blk, 1, d), lambda i: (i // nr, 0, 0)),
            pl.BlockSpec((d, d), lambda i: (0, 0)),
        ],
        out_specs=pl.BlockSpec((nb_blk, r_blk, d), lambda i: (i // nr, i % nr, 0)),
        out_shape=jax.ShapeDtypeStruct(x.shape, F32),
        compiler_params=_params(("arbitrary",)),
        name="mixer_out",
    )(x, m, a, gate[:, None, :], w_out_b)


def _ffn_kernel(x_ref, sh_ref, sc_ref, gt_ref, g_ref, wg_ref, wu_ref, wo_ref, gf_ref, shf_ref, scf_ref,
                o_ref, h_s, acc_s):
    f = pl.program_id(1)
    rows = h_s.shape[0]

    @pl.when(f == 0)
    def _():
        x = x_ref[...]
        y = x * lax.rsqrt(jnp.mean(x * x, axis=-1, keepdims=True) + EPS)
        hm = (y * g_ref[...]) * (1.0 + sc_ref[...]) + sh_ref[...]
        h_s[...] = hm.reshape(rows, hm.shape[-1]).astype(BF16)
        acc_s[...] = jnp.zeros_like(acc_s)

    h = h_s[...]
    zg = jnp.dot(h, wg_ref[...], preferred_element_type=F32)
    zu = jnp.dot(h, wu_ref[...], preferred_element_type=F32)
    act = (zg * jax.nn.sigmoid(zg) * zu).astype(BF16)
    acc_s[...] += jnp.dot(act, wo_ref[...], preferred_element_type=F32)

    @pl.when(f == pl.num_programs(1) - 1)
    def _():
        nb, r, d = x_ref.shape
        x2 = x_ref[...] + gt_ref[...] * acc_s[...].reshape(nb, r, d)
        y = x2 * lax.rsqrt(jnp.mean(x2 * x2, axis=-1, keepdims=True) + EPS)
        o_ref[...] = (y * gf_ref[...]) * (1.0 + scf_ref[...]) + shf_ref[...]


def _ffn(x, shift, scale, gate, g_ffn, w_in_b, w_out_b, g_final, shift_f, scale_f, *, nb_blk, r_blk, tf):
    nbat, r, d = x.shape
    d_ff = w_out_b.shape[0]
    nf = d_ff // tf
    nr = r // r_blk
    rows = nb_blk * r_blk
    x_spec = pl.BlockSpec((nb_blk, r_blk, d), lambda i, f: (i // nr, i % nr, 0))
    vec_spec = pl.BlockSpec((nb_blk, 1, d), lambda i, f: (i // nr, 0, 0))
    par_spec = pl.BlockSpec((1, 1, d), lambda i, f: (0, 0, 0))
    return pl.pallas_call(
        _ffn_kernel,
        grid=((nbat // nb_blk) * nr, nf),
        in_specs=[
            x_spec, vec_spec, vec_spec, vec_spec, par_spec,
            pl.BlockSpec((d, tf), lambda i, f: (0, f)),
            pl.BlockSpec((d, tf), lambda i, f: (0, f + nf)),
            pl.BlockSpec((tf, d), lambda i, f: (f, 0)),
            par_spec, vec_spec, vec_spec,
        ],
        out_specs=x_spec,
        out_shape=jax.ShapeDtypeStruct(x.shape, F32),
        scratch_shapes=[pltpu.VMEM((rows, d), BF16), pltpu.VMEM((rows, d), F32)],
        compiler_params=_params(("arbitrary", "arbitrary")),
        name="ffn",
    )(x, shift[:, None, :], scale[:, None, :], gate[:, None, :], g_ffn.reshape(1, 1, d),
      w_in_b, w_in_b, w_out_b, g_final.reshape(1, 1, d), shift_f[:, None, :], scale_f[:, None, :])


ATTN_Q_BLOCK = 1024
ATTN_K_BEAT = 512
PROMPT_ROWS = 512
SAMPLE_BATCH_BLOCK = 8
FFN_TILE = 512


def kernel(x_prompt, x_sample, cache_k, cache_v, c_prompt, c_sample, rel_bias, w_ada, b_ada, w_ada_final,
           b_ada_final, g_mix, g_ffn, g_final, w_in, mlp_ln_g, mlp_ln_b, w_s, b_s, lambda_q1, lambda_k1,
           lambda_q2, lambda_k2, sub_g, w_out, w_ffn_in, w_ffn_out):
    B, S, D = x_prompt.shape
    DB, T, _ = x_sample.shape
    depth = w_in.shape[0]
    past = cache_k.shape[2]
    width = N_HEADS * V_HEAD_DIM
    mlp_chunk = w_s.shape[-1]

    c_all = jnp.concatenate([c_prompt, c_sample], axis=0)
    mod_f = _adaln(c_all, w_ada_final, b_ada_final)
    bias_p = _rel_bias_tiles(rel_bias, ATTN_Q_BLOCK, ATTN_Q_BLOCK + ATTN_K_BEAT, ATTN_K_BEAT, 0, True, True)
    bias_s = _rel_bias_tiles(rel_bias, T, past + T, past, 0, False, False)

    assert depth == 1, "the final adaLN norm is fused into the single layer's FFN kernel"
    lam_init = 0.8 - 0.6 * math.exp(-0.3 * 0)
    lam_vecs = jnp.stack([lambda_q1[0], lambda_k1[0], lambda_q2[0], lambda_k2[0]])
    subg = sub_g.reshape(1, V_HEAD_DIM)
    mod = _adaln(c_all, w_ada[0], b_ada[0])
    sh1, sc1, gt1, sh2, sc2, gt2 = jnp.split(mod, 6, axis=-1)
    shf, scf = jnp.split(mod_f, 2, axis=-1)
    w_in_b = w_in[0].astype(BF16)
    w_out_b = w_out[0].astype(BF16)
    w_f_in_b = w_ffn_in[0].astype(BF16)
    w_f_out_b = w_ffn_out[0].astype(BF16)
    mixer_w = (g_mix[0], w_in_b, mlp_ln_g[0], mlp_ln_b[0], w_s[0], b_s[0])

    m, q, kp, kb, vp, vt = _mixer_in(x_prompt, sh1[:B], sc1[:B], *mixer_w, nb_blk=1, r_blk=PROMPT_ROWS,
                                     t_chunk=mlp_chunk, emit_gv=False, v_transposed=True)
    a = _attn_prompt(q.reshape(B, S, width), kb.reshape(B, S, width), vt, bias_p, lam_vecs,
                     subg.reshape(V_HEAD_DIM, 1), bq=ATTN_Q_BLOCK, bk=ATTN_K_BEAT, lam_init=lam_init)
    xp = _mixer_out(x_prompt, m, a.reshape(B * S, width), gt1[:B], w_out_b, nb_blk=1, r_blk=PROMPT_ROWS)
    yp = _ffn(xp, sh2[:B], sc2[:B], gt2[:B], g_ffn[0], w_f_in_b, w_f_out_b, g_final, shf[:B], scf[:B],
              nb_blk=1, r_blk=PROMPT_ROWS, tf=FFN_TILE)

    m, q, ks, kb, vs, vb, gvs = _mixer_in(x_sample, sh1[B:], sc1[B:], *mixer_w, nb_blk=SAMPLE_BATCH_BLOCK,
                                          r_blk=T, t_chunk=T, emit_gv=True, v_transposed=False)
    a = _attn_sample(q.reshape(DB, T, width), cache_k.reshape(DB, past * N_HEADS, LANES),
                     cache_v.reshape(DB, past * N_HEADS, LANES), kb.reshape(DB, T, width),
                     vb.reshape(DB, T, width), bias_s, lam_vecs, subg, lam_init=lam_init)
    xs = _mixer_out(x_sample, m, a.reshape(DB * T, width), gt1[B:], w_out_b, nb_blk=SAMPLE_BATCH_BLOCK, r_blk=T)
    ys = _ffn(xs, sh2[B:], sc2[B:], gt2[B:], g_ffn[0], w_f_in_b, w_f_out_b, g_final, shf[B:], scf[B:],
              nb_blk=SAMPLE_BATCH_BLOCK, r_blk=T, tf=FFN_TILE)

    head_shape = (N_HEADS, V_HEAD_DIM)
    return (yp, ys, kp.reshape(1, B, S, *head_shape), vp.reshape(1, B, S, *head_shape),
            ks.reshape(1, DB, T, *head_shape), vs.reshape(1, DB, T, *head_shape),
            gvs.reshape(1, DB, T, MLP_GROUPS, MLP_GROUP_DIM))
```

```python
import functools
import math

import jax
import jax.numpy as jnp
from jax import lax
from jax.experimental import pallas as pl
from jax.experimental.pallas import tpu as pltpu

LANES = 128
SUBLANES = 8
VMEM_LIMIT_BYTES = 56 * 1024 * 1024
MXU_WIDTH = 256

ATTN_STRIP = MXU_WIDTH
ONES_ROWS = SUBLANES

CHUNK = 64
N_HEADS = 8
HEAD_DIM = 64
V_HEAD_DIM = 128
MLP_GROUPS = 8
MLP_GROUP_DIM = 128
N_BUCKETS = 32
MAX_DISTANCE = 128
EPS = 1e-6
MASK_VALUE = -1e30
LOG2E = math.log2(math.e)

BF16 = jnp.bfloat16
F32 = jnp.float32


def _params(sem):
    return pltpu.CompilerParams(dimension_semantics=sem, vmem_limit_bytes=VMEM_LIMIT_BYTES)


def _adaln_kernel(c_ref, w_ref, b_ref, o_ref):
    c = c_ref[...]
    a = c * jax.nn.sigmoid(c)
    o_ref[...] = jnp.dot(a, w_ref[...], preferred_element_type=F32) + b_ref[...]


def _adaln(c, w, b, tn=1024):
    rows, d = c.shape
    n = w.shape[1]
    return pl.pallas_call(
        _adaln_kernel,
        grid=(n // tn,),
        in_specs=[
            pl.BlockSpec((rows, d), lambda j: (0, 0)),
            pl.BlockSpec((d, tn), lambda j: (0, j)),
            pl.BlockSpec((1, tn), lambda j: (0, j)),
        ],
        out_specs=pl.BlockSpec((rows, tn), lambda j: (0, j)),
        out_shape=jax.ShapeDtypeStruct((rows, n), F32),
        compiler_params=_params(("arbitrary",)),
        name="adaln",
    )(c, w, b.reshape(1, n))


def _rel_bias_kernel(tab_ref, o_ref, *, nq, nk, q_start, k_start, shift_far, keys_on_rows):
    h = pl.program_id(0)
    shape, q_axis, k_axis = ((nk, nq), 1, 0) if keys_on_rows else ((nq, nk), 0, 1)
    q_pos = q_start + lax.broadcasted_iota(jnp.int32, shape, q_axis)
    k_pos = k_start + lax.broadcasted_iota(jnp.int32, shape, k_axis)
    rel = k_pos - q_pos
    nb = N_BUCKETS // 2
    max_exact = nb // 2
    ret = jnp.where(rel > 0, nb, 0)
    n = jnp.abs(rel)
    nf = jnp.maximum(n, 1).astype(F32)
    large = max_exact + (jnp.log(nf / max_exact) / math.log(MAX_DISTANCE / max_exact)
                         * (nb - max_exact)).astype(jnp.int32)
    large = jnp.minimum(large, nb - 1)
    bucket = ret + jnp.where(n < max_exact, n, large)
    bias = jnp.zeros(shape, F32)
    for bkt in range(N_BUCKETS):
        bias = jnp.where(bucket == bkt, tab_ref[bkt, h], bias)
    if shift_far:
        bias = bias - tab_ref[nb - 1, h]
    allowed = (k_pos // CHUNK) <= (q_pos // CHUNK)
    o_ref[0] = jnp.where(allowed, bias * LOG2E, MASK_VALUE)


def _rel_bias_tiles(rel_bias, nq, nk, q_start, k_start, shift_far, keys_on_rows):
    out_tile = (nk, nq) if keys_on_rows else (nq, nk)
    return pl.pallas_call(
        functools.partial(_rel_bias_kernel, nq=nq, nk=nk, q_start=q_start, k_start=k_start,
                          shift_far=shift_far, keys_on_rows=keys_on_rows),
        grid=(N_HEADS,),
        in_specs=[pl.BlockSpec(memory_space=pltpu.SMEM)],
        out_specs=pl.BlockSpec((1,) + out_tile, lambda h: (h, 0, 0)),
        out_shape=jax.ShapeDtypeStruct((N_HEADS,) + out_tile, F32),
        compiler_params=_params(("arbitrary",)),
        name="rel_bias",
    )(rel_bias)


def _mixer_in_kernel(x_ref, sh_ref, sc_ref, g_ref, w_ref, lng_ref, lnb_ref, ws_ref, bs_ref,
                     *refs, t_chunk, emit_gv, v_transposed):
    if emit_gv:
        m_ref, q_ref, k_ref, kb_ref, v_ref, vb_ref, gv_ref, h_s, u_s = refs
    else:
        m_ref, q_ref, k_ref, kb_ref, v_ref, vb_ref, h_s, u_s = refs
        gv_ref = None
    j = pl.program_id(1)
    rows = h_s.shape[0]

    def store_per_head(ref, val):
        for hd in range(N_HEADS):
            ref[pl.ds(hd, rows, stride=N_HEADS), :] = val[:, hd * LANES:(hd + 1) * LANES]

    @pl.when(j == 0)
    def _():
        x = x_ref[...]
        y = x * lax.rsqrt(jnp.mean(x * x, axis=-1, keepdims=True) + EPS)
        y = y * g_ref[...]
        hm = y * (1.0 + sc_ref[...]) + sh_ref[...]
        h_s[...] = hm.reshape(rows, hm.shape[-1]).astype(BF16)

    z = jnp.dot(h_s[...], w_ref[...], preferred_element_type=F32)

    @pl.when(j == 0)
    def _():
        u_s[...] = jax.nn.gelu(z)

    @pl.when(j == 1)
    def _():
        g = jax.nn.gelu(z)
        mu = jnp.mean(g, axis=-1, keepdims=True)
        var = jnp.mean(jnp.square(g - mu), axis=-1, keepdims=True)
        gv = (g - mu) * lax.rsqrt(var + EPS) * lng_ref[...] + lnb_ref[...]
        if emit_gv:
            store_per_head(gv_ref, gv)
        gvb = gv.astype(BF16)
        ii = lax.broadcasted_iota(jnp.int32, (t_chunk, t_chunk), 0)
        jj = lax.broadcasted_iota(jnp.int32, (t_chunk, t_chunk), 1)
        mask = (jj // CHUNK) <= (ii // CHUNK)
        for grp in range(MLP_GROUPS):
            wg = jnp.where(mask, ws_ref[grp], 0.0).astype(BF16)
            bg = bs_ref[grp]
            cs = slice(grp * MLP_GROUP_DIM, (grp + 1) * MLP_GROUP_DIM)
            for c in range(rows // t_chunk):
                rs = slice(c * t_chunk, (c + 1) * t_chunk)
                mixed = jnp.dot(wg, gvb[rs, cs], preferred_element_type=F32) + bg
                m_ref[rs, cs] = (u_s[rs, cs] * mixed).astype(BF16)

    @pl.when(j == 2)
    def _():
        q_ref[...] = (z * (HEAD_DIM ** -0.5 * LOG2E)).astype(BF16)

    @pl.when(j == 3)
    def _():
        store_per_head(k_ref, z)
        kb_ref[...] = z.astype(BF16)

    @pl.when(j == 4)
    def _():
        store_per_head(v_ref, z)
        if v_transposed:
            vb_ref[0] = z.T.astype(BF16)
        else:
            vb_ref[...] = z.astype(BF16)


def _mixer_in(x, shift, scale, g_mix, w_in_b, ln_g, ln_b, w_s, b_s, *, nb_blk, r_blk, t_chunk, emit_gv,
              v_transposed):
    nbat, r, d = x.shape
    width = 1024
    nr = r // r_blk
    n_tiles = (nbat // nb_blk) * nr
    rows = nb_blk * r_blk
    tokens = nbat * r
    row_idx = lambda i, j: (i, 0)
    flat = jax.ShapeDtypeStruct((tokens, width), BF16)
    per_head = jax.ShapeDtypeStruct((tokens * N_HEADS, LANES), F32)
    flat_spec = pl.BlockSpec((rows, width), row_idx)
    per_head_spec = pl.BlockSpec((rows * N_HEADS, LANES), row_idx)
    out_shape = [flat, flat, per_head, flat, per_head, flat]
    out_specs = [flat_spec, flat_spec, per_head_spec, flat_spec, per_head_spec, flat_spec]
    if v_transposed:
        assert nb_blk == 1
        out_shape[5] = jax.ShapeDtypeStruct((nbat, width, r), BF16)
        out_specs[5] = pl.BlockSpec((1, width, r_blk), lambda i, j: (i // nr, 0, i % nr))
    if emit_gv:
        out_shape.append(per_head)
        out_specs.append(per_head_spec)
    ws_t = w_s[:, :t_chunk, :t_chunk]
    bs_t = b_s[:, :t_chunk, None]
    return pl.pallas_call(
        functools.partial(_mixer_in_kernel, t_chunk=t_chunk, emit_gv=emit_gv, v_transposed=v_transposed),
        grid=(n_tiles, 5),
        in_specs=[
            pl.BlockSpec((nb_blk, r_blk, d), lambda i, j: (i // nr, i % nr, 0)),
            pl.BlockSpec((nb_blk, 1, d), lambda i, j: (i // nr, 0, 0)),
            pl.BlockSpec((nb_blk, 1, d), lambda i, j: (i // nr, 0, 0)),
            pl.BlockSpec((1, 1, d), lambda i, j: (0, 0, 0)),
            pl.BlockSpec((d, width), lambda i, j: (0, j)),
            pl.BlockSpec((1, width), lambda i, j: (0, 0)),
            pl.BlockSpec((1, width), lambda i, j: (0, 0)),
            pl.BlockSpec((MLP_GROUPS, t_chunk, t_chunk), lambda i, j: (0, 0, 0)),
            pl.BlockSpec((MLP_GROUPS, t_chunk, 1), lambda i, j: (0, 0, 0)),
        ],
        out_specs=out_specs,
        out_shape=out_shape,
        scratch_shapes=[pltpu.VMEM((rows, d), BF16), pltpu.VMEM((rows, width), F32)],
        compiler_params=_params(("arbitrary", "arbitrary")),
        name="mixer_in",
    )(x, shift[:, None, :], scale[:, None, :], g_mix.reshape(1, 1, d), w_in_b,
      ln_g.reshape(1, width), ln_b.reshape(1, width), ws_t, bs_t)


def _split_q(q):
    lane = lax.broadcasted_iota(jnp.int32, q.shape, 1)
    zero = jnp.zeros_like(q)
    return jnp.concatenate([jnp.where(lane < HEAD_DIM, q, zero), jnp.where(lane >= HEAD_DIM, q, zero)], axis=0)


def _softmax_step(s, v, m, l, acc):
    m_new = jnp.maximum(m, jnp.max(s, axis=-1, keepdims=True))
    alpha = jnp.exp2(m - m_new)
    p = jnp.exp2(s - m_new)
    l_new = alpha * l + jnp.sum(p, axis=-1, keepdims=True)
    acc_new = alpha * acc + jnp.dot(p.astype(BF16), v, preferred_element_type=F32)
    return m_new, l_new, acc_new


def _diff_lambda(lam_ref, lam_init):
    lv = lam_ref[...]
    s1 = jnp.sum(lv[0:1] * lv[1:2], axis=-1, keepdims=True)
    s2 = jnp.sum(lv[2:3] * lv[3:4], axis=-1, keepdims=True)
    return jnp.exp(s1) - jnp.exp(s2) + lam_init


def _finish_heads(l, acc, lam, subg, lam_init, n):
    o = acc / l
    a = o[:n] - lam * o[n:]
    a = a * lax.rsqrt(jnp.mean(a * a, axis=-1, keepdims=True) + EPS)
    return a * subg * (1.0 - lam_init)


def _qk(qs, k):
    return lax.dot_general(qs, k, (((1,), (1,)), ((), ())), preferred_element_type=F32)


def _attn_prompt_kernel(q_ref, k_ref, vt_ref, bias_ref, lam_ref, subg_ref, o_ref,
                        p_s, alpha_s, r_s, acc_s, *, bq, bk, lam_init):
    qi = pl.program_id(2)
    halves = bq // ATTN_STRIP
    strips = [(c, h) for c in range(2) for h in range(halves)]
    n_near = bq // bk + 1
    q = q_ref[0]
    lane = lax.broadcasted_iota(jnp.int32, q.shape, 1)
    zero = jnp.zeros_like(q)
    qs = (jnp.where(lane < HEAD_DIM, q, zero), jnp.where(lane >= HEAD_DIM, q, zero))

    ones = jnp.ones((ONES_ROWS, bk), BF16)

    def keys(j):
        return k_ref[0, pl.ds(pl.multiple_of(j * bk, bk), bk), :]

    def values(j):
        vt = vt_ref[0, :, pl.ds(pl.multiple_of(j * bk, bk), bk)]
        return jnp.concatenate([vt, ones], axis=0)

    def logits(k, n, bias):
        c, h = strips[n]
        cols = slice(h * ATTN_STRIP, (h + 1) * ATTN_STRIP)
        s = lax.dot_general(k, qs[c][cols], (((1,), (1,)), ((), ())), preferred_element_type=F32)
        if bias is not None:
            s = s + (bias if bias.ndim == 0 else bias[:, cols])
        return s

    def beat(new=None, old=None):
        if new is not None:
            k = keys(new[0])
        if old is not None:
            vt1 = values(old[0])
        for n in range(len(strips)):
            if new is not None:
                _, slot, bias = new
                s = logits(k, n, bias)
                r = r_s[n]
                p_s[slot, n] = jnp.exp2(s - r).astype(BF16)
                r_new = jnp.maximum(r, jnp.max(s, axis=0, keepdims=True))
                alpha_s[slot, n] = jnp.exp2(r - r_new)
                r_s[n] = r_new
            if old is not None:
                slot = old[1]
                pv = jnp.dot(vt1, p_s[slot, n], preferred_element_type=F32)
                acc_s[n] = (acc_s[n] + pv) * alpha_s[slot, n]

    n_far = jnp.maximum((qi * bq - MAX_DISTANCE) // bk, 0)
    near_bias = []
    for i in range(n_near):
        shifted = bias_ref[0, (i + 1) * bk:(i + 2) * bk, :] if i + 1 < n_near else MASK_VALUE
        near_bias.append(jnp.where(qi == 0, shifted, bias_ref[0, i * bk:(i + 1) * bk, :]))

    k0 = keys(0)
    bias0 = jnp.where(qi == 0, near_bias[0], 0.0)
    for n in range(len(strips)):
        r_s[n] = jnp.max(logits(k0, n, bias0), axis=0, keepdims=True)
    acc_s[...] = jnp.zeros(acc_s.shape, F32)
    alpha_s[1] = jnp.ones(alpha_s.shape[1:], F32)
    p_s[1] = jnp.zeros(p_s.shape[1:], BF16)

    n_pairs = n_far // 2

    @pl.loop(0, n_pairs)
    def _(t):
        j = 2 * t
        beat(new=(j, 0, None), old=(jnp.maximum(j - 1, 0), 1))
        beat(new=(j + 1, 1, None), old=(j, 0))

    x = 2 * n_pairs
    tail = [(x, jnp.where(n_far % 2 == 1, 0.0, MASK_VALUE))] + [(n_far + i, b) for i, b in enumerate(near_bias)]
    for t, (j, bias_t) in enumerate(tail):
        prev = (jnp.maximum(x - 1, 0), 1) if t == 0 else (tail[t - 1][0], (t - 1) % 2)
        beat(new=(j, t % 2, bias_t), old=prev)
    beat(old=(tail[-1][0], (len(tail) - 1) % 2))

    lam = _diff_lambda(lam_ref, lam_init)

    def emit_output():
        for h in range(halves):
            o = []
            for c in range(2):
                acc = acc_s[c * halves + h]
                o.append(acc[:V_HEAD_DIM] / acc[V_HEAD_DIM:V_HEAD_DIM + 1])
            a = o[0] - lam * o[1]
            a = a * lax.rsqrt(jnp.mean(a * a, axis=0, keepdims=True) + EPS)
            a = a * subg_ref[...] * (1.0 - lam_init)
            o_ref[0, h * ATTN_STRIP:(h + 1) * ATTN_STRIP, :] = a.T.astype(o_ref.dtype)

    emit_output()

    acc_all = acc_s[...]
    bad = jnp.where(jnp.abs(acc_all) < jnp.inf, 0.0, 1.0)
    bad = jnp.maximum(bad, jnp.where(acc_all[:, V_HEAD_DIM:V_HEAD_DIM + 1] > 0.0, 0.0, 1.0))
    bad = jnp.max(jnp.max(bad, axis=0), axis=0, keepdims=True)
    bad = jnp.max(bad, axis=1, keepdims=True)[0, 0]

    @pl.when(bad > 0.0)
    def _():
        r_s[...] = jnp.full(r_s.shape, MASK_VALUE, F32)
        acc_s[...] = jnp.zeros(acc_s.shape, F32)
        tile_rows = bq + bk

        @pl.loop(0, n_far + n_near)
        def _(j):
            start = (j - n_far + jnp.where(qi == 0, 1, 0)) * bk
            in_tile = jnp.logical_and(j >= n_far, start + bk <= tile_rows)
            start = pl.multiple_of(jnp.clip(start, 0, tile_rows - bk), bk)
            bias = jnp.where(j < n_far, 0.0, jnp.where(in_tile, bias_ref[0, pl.ds(start, bk), :], MASK_VALUE))
            k = keys(j)
            vt1 = values(j)
            for n in range(len(strips)):
                s = logits(k, n, bias)
                m_old = r_s[n]
                m_new = jnp.maximum(m_old, jnp.max(s, axis=0, keepdims=True))
                p = jnp.exp2(s - m_new).astype(BF16)
                acc_s[n] = jnp.exp2(m_old - m_new) * acc_s[n] + jnp.dot(vt1, p, preferred_element_type=F32)
                r_s[n] = m_new

        emit_output()


def _attn_prompt(q, k, vt, bias, lam_vecs, sub_g_col, *, bq, bk, lam_init):
    b, s, _ = q.shape
    n_strips = 2 * (bq // ATTN_STRIP)
    assert bq % bk == 0 and bk >= MAX_DISTANCE and s >= bq + bk
    return pl.pallas_call(
        functools.partial(_attn_prompt_kernel, bq=bq, bk=bk, lam_init=lam_init),
        grid=(b, N_HEADS, s // bq),
        in_specs=[
            pl.BlockSpec((1, bq, LANES), lambda bi, h, qi: (bi, qi, h)),
            pl.BlockSpec((1, s, LANES), lambda bi, h, qi: (bi, 0, h)),
            pl.BlockSpec((1, V_HEAD_DIM, s), lambda bi, h, qi: (bi, h, 0)),
            pl.BlockSpec((1, bq + bk, bq), lambda bi, h, qi: (h, 0, 0)),
            pl.BlockSpec((4, HEAD_DIM), lambda bi, h, qi: (0, 0)),
            pl.BlockSpec((V_HEAD_DIM, 1), lambda bi, h, qi: (0, 0)),
        ],
        out_specs=pl.BlockSpec((1, bq, LANES), lambda bi, h, qi: (bi, qi, h)),
        out_shape=jax.ShapeDtypeStruct((b, s, N_HEADS * V_HEAD_DIM), BF16),
        scratch_shapes=[
            pltpu.VMEM((2, n_strips, bk, ATTN_STRIP), BF16),
            pltpu.VMEM((2, n_strips, 1, ATTN_STRIP), F32),
            pltpu.VMEM((n_strips, 1, ATTN_STRIP), F32),
            pltpu.VMEM((n_strips, V_HEAD_DIM + ONES_ROWS, ATTN_STRIP), F32),
        ],
        compiler_params=_params(("arbitrary", "arbitrary", "arbitrary")),
        name="attn_prompt",
    )(q, k, vt, bias, lam_vecs, sub_g_col)


def _attn_sample_kernel(q_ref, ck_ref, cv_ref, nk_ref, nv_ref, bias_ref, lam_ref, subg_ref, o_ref,
                        *, past, lam_init):
    t = q_ref.shape[1]
    lam = _diff_lambda(lam_ref, lam_init)
    for hd in range(N_HEADS):
        cs = slice(hd * LANES, (hd + 1) * LANES)
        qs = _split_q(q_ref[0, :, cs])
        bias = bias_ref[hd]
        bias2 = jnp.concatenate([bias, bias], axis=0)
        carry = (jnp.full((2 * t, 1), MASK_VALUE, F32), jnp.zeros((2 * t, 1), F32),
                 jnp.zeros((2 * t, V_HEAD_DIM), F32))
        kc = ck_ref[0, pl.ds(hd, past, stride=N_HEADS), :].astype(BF16)
        vc = cv_ref[0, pl.ds(hd, past, stride=N_HEADS), :].astype(BF16)
        carry = _softmax_step(_qk(qs, kc) + bias2[:, :past], vc, *carry)
        m, l, acc = _softmax_step(_qk(qs, nk_ref[0, :, cs]) + bias2[:, past:], nv_ref[0, :, cs], *carry)
        o_ref[0, :, cs] = _finish_heads(l, acc, lam, subg_ref[...], lam_init, t).astype(o_ref.dtype)


def _attn_sample(q, cache_k, cache_v, new_k, new_v, bias, lam_vecs, sub_g, *, lam_init):
    b, t, width = q.shape
    past = cache_k.shape[1] // N_HEADS
    flat_spec = pl.BlockSpec((1, t, width), lambda bi: (bi, 0, 0))
    cache_spec = pl.BlockSpec((1, past * N_HEADS, LANES), lambda bi: (bi, 0, 0))
    return pl.pallas_call(
        functools.partial(_attn_sample_kernel, past=past, lam_init=lam_init),
        grid=(b,),
        in_specs=[
            flat_spec, cache_spec, cache_spec, flat_spec, flat_spec,
            pl.BlockSpec((N_HEADS, t, past + t), lambda bi: (0, 0, 0)),
            pl.BlockSpec((4, HEAD_DIM), lambda bi: (0, 0)),
            pl.BlockSpec((1, V_HEAD_DIM), lambda bi: (0, 0)),
        ],
        out_specs=flat_spec,
        out_shape=jax.ShapeDtypeStruct((b, t, width), BF16),
        compiler_params=_params(("arbitrary",)),
        name="attn_sample",
    )(q, cache_k, cache_v, new_k, new_v, bias, lam_vecs, sub_g)


def _mixer_out_kernel(x_ref, m_ref, a_ref, gt_ref, w_ref, o_ref):
    half = m_ref.shape[-1]
    y = jnp.dot(m_ref[...], w_ref[:half, :], preferred_element_type=F32)
    y = y + jnp.dot(a_ref[...], w_ref[half:, :], preferred_element_type=F32)
    nb, r, d = x_ref.shape
    o_ref[...] = x_ref[...] + gt_ref[...] * y.reshape(nb, r, d)


def _mixer_out(x, m, a, gate, w_out_b, *, nb_blk, r_blk):
    nbat, r, d = x.shape
    nr = r // r_blk
    rows = nb_blk * r_blk
    half = m.shape[-1]
    return pl.pallas_call(
        _mixer_out_kernel,
        grid=((nbat // nb_blk) * nr,),
        in_specs=[
            pl.BlockSpec((nb_blk, r_blk, d), lambda i: (i // nr, i % nr, 0)),
            pl.BlockSpec((rows, half), lambda i: (i, 0)),
            pl.BlockSpec((rows, half), lambda i: (i, 0)),
            pl.BlockSpec((nb_blk, 1, d), lambda i: (i // nr, 0, 0)),
            pl.BlockSpec((d, d), lambda i: (0, 0)),
        ],
        out_specs=pl.BlockSpec((nb_blk, r_blk, d), lambda i: (i // nr, i % nr, 0)),
        out_shape=jax.ShapeDtypeStruct(x.shape, F32),
        compiler_params=_params(("arbitrary",)),
        name="mixer_out",
    )(x, m, a, gate[:, None, :], w_out_b)


def _ffn_kernel(x_ref, sh_ref, sc_ref, gt_ref, g_ref, wg_ref, wu_ref, wo_ref, gf_ref, shf_ref, scf_ref,
                o_ref, h_s, acc_s):
    f = pl.program_id(1)
    rows = h_s.shape[0]

    @pl.when(f == 0)
    def _():
        x = x_ref[...]
        y = x * lax.rsqrt(jnp.mean(x * x, axis=-1, keepdims=True) + EPS)
        hm = (y * g_ref[...]) * (1.0 + sc_ref[...]) + sh_ref[...]
        h_s[...] = hm.reshape(rows, hm.shape[-1]).astype(BF16)
        acc_s[...] = jnp.zeros_like(acc_s)

    h = h_s[...]
    zg = jnp.dot(h, wg_ref[...], preferred_element_type=F32)
    zu = jnp.dot(h, wu_ref[...], preferred_element_type=F32)
    act = (zg * jax.nn.sigmoid(zg) * zu).astype(BF16)
    acc_s[...] += jnp.dot(act, wo_ref[...], preferred_element_type=F32)

    @pl.when(f == pl.num_programs(1) - 1)
    def _():
        nb, r, d = x_ref.shape
        x2 = x_ref[...] + gt_ref[...] * acc_s[...].reshape(nb, r, d)
        y = x2 * lax.rsqrt(jnp.mean(x2 * x2, axis=-1, keepdims=True) + EPS)
        o_ref[...] = (y * gf_ref[...]) * (1.0 + scf_ref[...]) + shf_ref[...]


def _ffn(x, shift, scale, gate, g_ffn, w_in_b, w_out_b, g_final, shift_f, scale_f, *, nb_blk, r_blk, tf):
    nbat, r, d = x.shape
    d_ff = w_out_b.shape[0]
    nf = d_ff // tf
    nr = r // r_blk
    rows = nb_blk * r_blk
    x_spec = pl.BlockSpec((nb_blk, r_blk, d), lambda i, f: (i // nr, i % nr, 0))
    vec_spec = pl.BlockSpec((nb_blk, 1, d), lambda i, f: (i // nr, 0, 0))
    par_spec = pl.BlockSpec((1, 1, d), lambda i, f: (0, 0, 0))
    return pl.pallas_call(
        _ffn_kernel,
        grid=((nbat // nb_blk) * nr, nf),
        in_specs=[
            x_spec, vec_spec, vec_spec, vec_spec, par_spec,
            pl.BlockSpec((d, tf), lambda i, f: (0, f)),
            pl.BlockSpec((d, tf), lambda i, f: (0, f + nf)),
            pl.BlockSpec((tf, d), lambda i, f: (f, 0)),
            par_spec, vec_spec, vec_spec,
        ],
        out_specs=x_spec,
        out_shape=jax.ShapeDtypeStruct(x.shape, F32),
        scratch_shapes=[pltpu.VMEM((rows, d), BF16), pltpu.VMEM((rows, d), F32)],
        compiler_params=_params(("arbitrary", "arbitrary")),
        name="ffn",
    )(x, shift[:, None, :], scale[:, None, :], gate[:, None, :], g_ffn.reshape(1, 1, d),
      w_in_b, w_in_b, w_out_b, g_final.reshape(1, 1, d), shift_f[:, None, :], scale_f[:, None, :])


ATTN_Q_BLOCK = 1024
ATTN_K_BEAT = 512
PROMPT_ROWS = 512
SAMPLE_BATCH_BLOCK = 8
FFN_TILE = 512


def kernel(x_prompt, x_sample, cache_k, cache_v, c_prompt, c_sample, rel_bias, w_ada, b_ada, w_ada_final,
           b_ada_final, g_mix, g_ffn, g_final, w_in, mlp_ln_g, mlp_ln_b, w_s, b_s, lambda_q1, lambda_k1,
           lambda_q2, lambda_k2, sub_g, w_out, w_ffn_in, w_ffn_out):
    B, S, D = x_prompt.shape
    DB, T, _ = x_sample.shape
    depth = w_in.shape[0]
    past = cache_k.shape[2]
    width = N_HEADS * V_HEAD_DIM
    mlp_chunk = w_s.shape[-1]

    c_all = jnp.concatenate([c_prompt, c_sample], axis=0)
    mod_f = _adaln(c_all, w_ada_final, b_ada_final)
    bias_p = _rel_bias_tiles(rel_bias, ATTN_Q_BLOCK, ATTN_Q_BLOCK + ATTN_K_BEAT, ATTN_K_BEAT, 0, True, True)
    bias_s = _rel_bias_tiles(rel_bias, T, past + T, past, 0, False, False)

    assert depth == 1, "the final adaLN norm is fused into the single layer's FFN kernel"
    lam_init = 0.8 - 0.6 * math.exp(-0.3 * 0)
    lam_vecs = jnp.stack([lambda_q1[0], lambda_k1[0], lambda_q2[0], lambda_k2[0]])
    subg = sub_g.reshape(1, V_HEAD_DIM)
    mod = _adaln(c_all, w_ada[0], b_ada[0])
    sh1, sc1, gt1, sh2, sc2, gt2 = jnp.split(mod, 6, axis=-1)
    shf, scf = jnp.split(mod_f, 2, axis=-1)
    w_in_b = w_in[0].astype(BF16)
    w_out_b = w_out[0].astype(BF16)
    w_f_in_b = w_ffn_in[0].astype(BF16)
    w_f_out_b = w_ffn_out[0].astype(BF16)
    mixer_w = (g_mix[0], w_in_b, mlp_ln_g[0], mlp_ln_b[0], w_s[0], b_s[0])

    m, q, kp, kb, vp, vt = _mixer_in(x_prompt, sh1[:B], sc1[:B], *mixer_w, nb_blk=1, r_blk=PROMPT_ROWS,
                                     t_chunk=mlp_chunk, emit_gv=False, v_transposed=True)
    a = _attn_prompt(q.reshape(B, S, width), kb.reshape(B, S, width), vt, bias_p, lam_vecs,
                     subg.reshape(V_HEAD_DIM, 1), bq=ATTN_Q_BLOCK, bk=ATTN_K_BEAT, lam_init=lam_init)
    xp = _mixer_out(x_prompt, m, a.reshape(B * S, width), gt1[:B], w_out_b, nb_blk=1, r_blk=PROMPT_ROWS)
    yp = _ffn(xp, sh2[:B], sc2[:B], gt2[:B], g_ffn[0], w_f_in_b, w_f_out_b, g_final, shf[:B], scf[:B],
              nb_blk=1, r_blk=PROMPT_ROWS, tf=FFN_TILE)

    m, q, ks, kb, vs, vb, gvs = _mixer_in(x_sample, sh1[B:], sc1[B:], *mixer_w, nb_blk=SAMPLE_BATCH_BLOCK,
                                          r_blk=T, t_chunk=T, emit_gv=True, v_transposed=False)
    a = _attn_sample(q.reshape(DB, T, width), cache_k.reshape(DB, past * N_HEADS, LANES),
                     cache_v.reshape(DB, past * N_HEADS, LANES), kb.reshape(DB, T, width),
                     vb.reshape(DB, T, width), bias_s, lam_vecs, subg, lam_init=lam_init)
    xs = _mixer_out(x_sample, m, a.reshape(DB * T, width), gt1[B:], w_out_b, nb_blk=SAMPLE_BATCH_BLOCK, r_blk=T)
    ys = _ffn(xs, sh2[B:], sc2[B:], gt2[B:], g_ffn[0], w_f_in_b, w_f_out_b, g_final, shf[B:], scf[B:],
              nb_blk=SAMPLE_BATCH_BLOCK, r_blk=T, tf=FFN_TILE)

    head_shape = (N_HEADS, V_HEAD_DIM)
    return (yp, ys, kp.reshape(1, B, S, *head_shape), vp.reshape(1, B, S, *head_shape),
            ks.reshape(1, DB, T, *head_shape), vs.reshape(1, DB, T, *head_shape),
            gvs.reshape(1, DB, T, MLP_GROUPS, MLP_GROUP_DIM))
```

```python
import functools
import math

import jax
import jax.numpy as jnp
from jax import lax
from jax.experimental import pallas as pl
from jax.experimental.pallas import tpu as pltpu

LANES = 128
SUBLANES = 8
VMEM_LIMIT_BYTES = 56 * 1024 * 1024
MXU_WIDTH = 256

ATTN_STRIP = MXU_WIDTH
ONES_ROWS = SUBLANES
R_INIT_KEYS = 128

CHUNK = 64
N_HEADS = 8
HEAD_DIM = 64
V_HEAD_DIM = 128
MLP_GROUPS = 8
MLP_GROUP_DIM = 128
N_BUCKETS = 32
MAX_DISTANCE = 128
EPS = 1e-6
MASK_VALUE = -1e30
LOG2E = math.log2(math.e)

BF16 = jnp.bfloat16
F32 = jnp.float32


def _params(sem):
    return pltpu.CompilerParams(dimension_semantics=sem, vmem_limit_bytes=VMEM_LIMIT_BYTES)


def _adaln_kernel(c_ref, w_ref, b_ref, o_ref):
    c = c_ref[...]
    a = c * jax.nn.sigmoid(c)
    o_ref[...] = jnp.dot(a, w_ref[...], preferred_element_type=F32) + b_ref[...]


def _adaln(c, w, b, tn=1024):
    rows, d = c.shape
    n = w.shape[1]
    return pl.pallas_call(
        _adaln_kernel,
        grid=(n // tn,),
        in_specs=[
            pl.BlockSpec((rows, d), lambda j: (0, 0)),
            pl.BlockSpec((d, tn), lambda j: (0, j)),
            pl.BlockSpec((1, tn), lambda j: (0, j)),
        ],
        out_specs=pl.BlockSpec((rows, tn), lambda j: (0, j)),
        out_shape=jax.ShapeDtypeStruct((rows, n), F32),
        compiler_params=_params(("arbitrary",)),
        name="adaln",
    )(c, w, b.reshape(1, n))


def _rel_bias_kernel(tab_ref, o_ref, *, nq, nk, q_start, k_start, shift_far, keys_on_rows):
    h = pl.program_id(0)
    nb = N_BUCKETS // 2
    max_exact = nb // 2
    q_axis, k_axis = (1, 0) if keys_on_rows else (0, 1)
    rows, cols = (nk, nq) if keys_on_rows else (nq, nk)
    shift = tab_ref[nb - 1, h] if shift_far else 0.0

    def block(shape, q0, k0):
        q_pos = q0 + lax.broadcasted_iota(jnp.int32, shape, q_axis)
        k_pos = k0 + lax.broadcasted_iota(jnp.int32, shape, k_axis)
        rel = k_pos - q_pos
        ret = jnp.where(rel > 0, nb, 0)
        n = jnp.abs(rel)
        nf = jnp.maximum(n, 1).astype(F32)
        large = max_exact + (jnp.log(nf / max_exact) / math.log(MAX_DISTANCE / max_exact)
                             * (nb - max_exact)).astype(jnp.int32)
        large = jnp.minimum(large, nb - 1)
        bucket = ret + jnp.where(n < max_exact, n, large)
        bias = jnp.zeros(shape, F32)
        for bkt in range(N_BUCKETS):
            bias = jnp.where(bucket == bkt, tab_ref[bkt, h], bias)
        allowed = (k_pos // CHUNK) <= (q_pos // CHUNK)
        return jnp.where(allowed, (bias - shift) * LOG2E, MASK_VALUE)

    if rows % LANES or cols % LANES:
        o_ref[0] = block((rows, cols), q_start, k_start)
        return
    far_value = (tab_ref[nb - 1, h] - shift) * LOG2E
    for rb in range(rows // LANES):
        for cb in range(cols // LANES):
            k0 = k_start + LANES * (rb if keys_on_rows else cb)
            q0 = q_start + LANES * (cb if keys_on_rows else rb)
            sl = (0, slice(rb * LANES, (rb + 1) * LANES), slice(cb * LANES, (cb + 1) * LANES))
            if (k0 - q0) + (LANES - 1) <= -MAX_DISTANCE:
                o_ref[sl] = jnp.full((LANES, LANES), far_value, F32)
            elif (k0 - q0) - (LANES - 1) >= CHUNK:
                o_ref[sl] = jnp.full((LANES, LANES), MASK_VALUE, F32)
            else:
                o_ref[sl] = block((LANES, LANES), q0, k0)


def _rel_bias_tiles(rel_bias, nq, nk, q_start, k_start, shift_far, keys_on_rows):
    out_tile = (nk, nq) if keys_on_rows else (nq, nk)
    return pl.pallas_call(
        functools.partial(_rel_bias_kernel, nq=nq, nk=nk, q_start=q_start, k_start=k_start,
                          shift_far=shift_far, keys_on_rows=keys_on_rows),
        grid=(N_HEADS,),
        in_specs=[pl.BlockSpec(memory_space=pltpu.SMEM)],
        out_specs=pl.BlockSpec((1,) + out_tile, lambda h: (h, 0, 0)),
        out_shape=jax.ShapeDtypeStruct((N_HEADS,) + out_tile, F32),
        compiler_params=_params(("arbitrary",)),
        name="rel_bias",
    )(rel_bias)


def _mixer_in_kernel(x_ref, sh_ref, sc_ref, g_ref, w_ref, lng_ref, lnb_ref, ws_ref, bs_ref,
                     *refs, t_chunk, emit_gv, v_transposed):
    if emit_gv:
        m_ref, q_ref, k_ref, kb_ref, v_ref, vb_ref, gv_ref, h_s, u_s, z_s = refs
    else:
        m_ref, q_ref, k_ref, kb_ref, v_ref, vb_ref, h_s, u_s, z_s = refs
        gv_ref = None
    j = pl.program_id(1)
    rows = h_s.shape[0]

    def store_per_head(ref, val):
        for hd in range(N_HEADS):
            ref[pl.ds(hd, rows, stride=N_HEADS), :] = val[:, hd * LANES:(hd + 1) * LANES]

    def normalize():
        x = x_ref[...]
        y = x * lax.rsqrt(jnp.mean(x * x, axis=-1, keepdims=True) + EPS)
        y = y * g_ref[...]
        hm = y * (1.0 + sc_ref[...]) + sh_ref[...]
        h_s[...] = hm.reshape(rows, hm.shape[-1]).astype(BF16)

    def finish_u(z):
        u_s[...] = jax.nn.gelu(z)

    def finish_gate(z):
        g = jax.nn.gelu(z)
        mu = jnp.mean(g, axis=-1, keepdims=True)
        var = jnp.mean(jnp.square(g - mu), axis=-1, keepdims=True)
        gv = (g - mu) * lax.rsqrt(var + EPS) * lng_ref[...] + lnb_ref[...]
        if emit_gv:
            store_per_head(gv_ref, gv)
        gvb = gv.astype(BF16)
        ii = lax.broadcasted_iota(jnp.int32, (t_chunk, t_chunk), 0)
        jj = lax.broadcasted_iota(jnp.int32, (t_chunk, t_chunk), 1)
        mask = (jj // CHUNK) <= (ii // CHUNK)
        for grp in range(MLP_GROUPS):
            wg = jnp.where(mask, ws_ref[grp], 0.0).astype(BF16)
            bg = bs_ref[grp]
            cs = slice(grp * MLP_GROUP_DIM, (grp + 1) * MLP_GROUP_DIM)
            for c in range(rows // t_chunk):
                rs = slice(c * t_chunk, (c + 1) * t_chunk)
                mixed = jnp.dot(wg, gvb[rs, cs], preferred_element_type=F32) + bg
                m_ref[rs, cs] = (u_s[rs, cs] * mixed).astype(BF16)

    def finish_q(z):
        q_ref[...] = (z * (HEAD_DIM ** -0.5 * LOG2E)).astype(BF16)

    def finish_k(z):
        store_per_head(k_ref, z)
        kb_ref[...] = z.astype(BF16)

    def finish_v(z):
        store_per_head(v_ref, z)
        if v_transposed:
            vb_ref[0] = z.T.astype(BF16)
        else:
            vb_ref[...] = z.astype(BF16)

    finish = (finish_u, finish_gate, finish_q, finish_k, finish_v)
    for step in range(len(finish) + 1):
        @pl.when(j == step)
        def _(step=step):
            if step == 0:
                normalize()
            if step < len(finish):
                z_s[step % 2] = jnp.dot(h_s[...], w_ref[...], preferred_element_type=F32)
            if step > 0:
                finish[step - 1](z_s[(step - 1) % 2])


def _mixer_in(x, shift, scale, g_mix, w_in_b, ln_g, ln_b, w_s, b_s, *, nb_blk, r_blk, t_chunk, emit_gv,
              v_transposed):
    nbat, r, d = x.shape
    width = 1024
    n_seg = w_in_b.shape[1] // width
    nr = r // r_blk
    n_tiles = (nbat // nb_blk) * nr
    rows = nb_blk * r_blk
    tokens = nbat * r
    row_idx = lambda i, j: (i, 0)
    flat = jax.ShapeDtypeStruct((tokens, width), BF16)
    per_head = jax.ShapeDtypeStruct((tokens * N_HEADS, LANES), F32)
    flat_spec = pl.BlockSpec((rows, width), row_idx)
    per_head_spec = pl.BlockSpec((rows * N_HEADS, LANES), row_idx)
    out_shape = [flat, flat, per_head, flat, per_head, flat]
    out_specs = [flat_spec, flat_spec, per_head_spec, flat_spec, per_head_spec, flat_spec]
    if v_transposed:
        assert nb_blk == 1
        out_shape[5] = jax.ShapeDtypeStruct((nbat, width, r), BF16)
        out_specs[5] = pl.BlockSpec((1, width, r_blk), lambda i, j: (i // nr, 0, i % nr))
    if emit_gv:
        out_shape.append(per_head)
        out_specs.append(per_head_spec)
    ws_t = w_s[:, :t_chunk, :t_chunk]
    bs_t = b_s[:, :t_chunk, None]
    return pl.pallas_call(
        functools.partial(_mixer_in_kernel, t_chunk=t_chunk, emit_gv=emit_gv, v_transposed=v_transposed),
        grid=(n_tiles, n_seg + 1),
        in_specs=[
            pl.BlockSpec((nb_blk, r_blk, d), lambda i, j: (i // nr, i % nr, 0)),
            pl.BlockSpec((nb_blk, 1, d), lambda i, j: (i // nr, 0, 0)),
            pl.BlockSpec((nb_blk, 1, d), lambda i, j: (i // nr, 0, 0)),
            pl.BlockSpec((1, 1, d), lambda i, j: (0, 0, 0)),
            pl.BlockSpec((d, width), lambda i, j: (0, jnp.minimum(j, n_seg - 1))),
            pl.BlockSpec((1, width), lambda i, j: (0, 0)),
            pl.BlockSpec((1, width), lambda i, j: (0, 0)),
            pl.BlockSpec((MLP_GROUPS, t_chunk, t_chunk), lambda i, j: (0, 0, 0)),
            pl.BlockSpec((MLP_GROUPS, t_chunk, 1), lambda i, j: (0, 0, 0)),
        ],
        out_specs=out_specs,
        out_shape=out_shape,
        scratch_shapes=[pltpu.VMEM((rows, d), BF16), pltpu.VMEM((rows, width), F32),
                        pltpu.VMEM((2, rows, width), F32)],
        compiler_params=_params(("arbitrary", "arbitrary")),
        name="mixer_in",
    )(x, shift[:, None, :], scale[:, None, :], g_mix.reshape(1, 1, d), w_in_b,
      ln_g.reshape(1, width), ln_b.reshape(1, width), ws_t, bs_t)


def _split_q(q):
    lane = lax.broadcasted_iota(jnp.int32, q.shape, 1)
    zero = jnp.zeros_like(q)
    return jnp.concatenate([jnp.where(lane < HEAD_DIM, q, zero), jnp.where(lane >= HEAD_DIM, q, zero)], axis=0)


def _softmax_step(s, v, m, l, acc):
    m_new = jnp.maximum(m, jnp.max(s, axis=-1, keepdims=True))
    alpha = jnp.exp2(m - m_new)
    p = jnp.exp2(s - m_new)
    l_new = alpha * l + jnp.sum(p, axis=-1, keepdims=True)
    acc_new = alpha * acc + jnp.dot(p.astype(BF16), v, preferred_element_type=F32)
    return m_new, l_new, acc_new


def _diff_lambda(lam_ref, lam_init):
    lv = lam_ref[...]
    s1 = jnp.sum(lv[0:1] * lv[1:2], axis=-1, keepdims=True)
    s2 = jnp.sum(lv[2:3] * lv[3:4], axis=-1, keepdims=True)
    return jnp.exp(s1) - jnp.exp(s2) + lam_init


def _finish_heads(l, acc, lam, subg, lam_init, n):
    o = acc / l
    a = o[:n] - lam * o[n:]
    a = a * lax.rsqrt(jnp.mean(a * a, axis=-1, keepdims=True) + EPS)
    return a * subg * (1.0 - lam_init)


def _qk(qs, k):
    return lax.dot_general(qs, k, (((1,), (1,)), ((), ())), preferred_element_type=F32)


def _attn_prompt_kernel(q_ref, k_ref, vt_ref, bias_ref, lam_ref, subg_ref, o_ref,
                        p_s, alpha_s, r_s, acc_s, *, bq, bk, lam_init):
    qi = pl.program_id(2)
    halves = bq // ATTN_STRIP
    strips = [(c, h) for c in range(2) for h in range(halves)]
    n_near = bq // bk + 1
    q = q_ref[0]
    lane = lax.broadcasted_iota(jnp.int32, q.shape, 1)
    zero = jnp.zeros_like(q)
    qs = (jnp.where(lane < HEAD_DIM, q, zero), jnp.where(lane >= HEAD_DIM, q, zero))

    ones = jnp.ones((ONES_ROWS, bk), BF16)

    def keys(j):
        return k_ref[0, pl.ds(pl.multiple_of(j * bk, bk), bk), :]

    def values(j):
        vt = vt_ref[0, :, pl.ds(pl.multiple_of(j * bk, bk), bk)]
        return jnp.concatenate([vt, ones], axis=0)

    def logits(k, n, bias):
        c, h = strips[n]
        cols = slice(h * ATTN_STRIP, (h + 1) * ATTN_STRIP)
        s = lax.dot_general(k, qs[c][cols], (((1,), (1,)), ((), ())), preferred_element_type=F32)
        if bias is not None:
            s = s + (bias if bias.ndim == 0 else bias[:, cols])
        return s

    def beat(new=None, old=None):
        if new is not None:
            k = keys(new[0])
        if old is not None:
            vt1 = values(old[0])
        for n in range(len(strips)):
            if new is not None and n not in new[3]:
                _, slot, bias, _ = new
                s = logits(k, n, bias)
                r = r_s[n]
                p_s[slot, n] = jnp.exp2(s - r).astype(BF16)
                r_new = jnp.maximum(r, jnp.max(s, axis=0, keepdims=True))
                alpha_s[slot, n] = jnp.exp2(r - r_new)
                r_s[n] = r_new
            if old is not None and n not in old[2]:
                slot = old[1]
                pv = jnp.dot(vt1, p_s[slot, n], preferred_element_type=F32)
                acc_s[n] = (acc_s[n] + pv) * alpha_s[slot, n]

    n_far = jnp.maximum((qi * bq - MAX_DISTANCE) // bk, 0)
    near_bias = []
    for i in range(n_near):
        shifted = bias_ref[0, (i + 1) * bk:(i + 2) * bk, :] if i + 1 < n_near else MASK_VALUE
        near_bias.append(jnp.where(qi == 0, shifted, bias_ref[0, i * bk:(i + 1) * bk, :]))

    k0 = k_ref[0, :R_INIT_KEYS, :]
    bias0 = jnp.where(qi == 0, near_bias[0][:R_INIT_KEYS], 0.0)
    for n in range(len(strips)):
        r_s[n] = jnp.max(logits(k0, n, bias0), axis=0, keepdims=True)
    acc_s[...] = jnp.zeros(acc_s.shape, F32)
    alpha_s[1] = jnp.ones(alpha_s.shape[1:], F32)
    p_s[1] = jnp.zeros(p_s.shape[1:], BF16)

    n_pairs = n_far // 2

    @pl.loop(0, n_pairs)
    def _(t):
        j = 2 * t
        beat(new=(j, 0, None, ()), old=(jnp.maximum(j - 1, 0), 1, ()))
        beat(new=(j + 1, 1, None, ()), old=(j, 0, ()))

    x = 2 * n_pairs
    tail = [(x, jnp.where(n_far % 2 == 1, 0.0, MASK_VALUE), ())]
    for i, bias_i in enumerate(near_bias):
        skip = tuple(n for n, (_, h) in enumerate(strips) if (i - 1) * bk >= (h + 1) * ATTN_STRIP)
        tail.append((n_far + i, bias_i, skip))
    for t, (j, bias_t, skip) in enumerate(tail):
        prev = (jnp.maximum(x - 1, 0), 1, ()) if t == 0 else (tail[t - 1][0], (t - 1) % 2, tail[t - 1][2])
        beat(new=(j, t % 2, bias_t, skip), old=prev)
    beat(old=(tail[-1][0], (len(tail) - 1) % 2, tail[-1][2]))

    lam = _diff_lambda(lam_ref, lam_init)

    def emit_output():
        for h in range(halves):
            o = []
            for c in range(2):
                acc = acc_s[c * halves + h]
                o.append(acc[:V_HEAD_DIM] / acc[V_HEAD_DIM:V_HEAD_DIM + 1])
            a = o[0] - lam * o[1]
            a = a * lax.rsqrt(jnp.mean(a * a, axis=0, keepdims=True) + EPS)
            a = a * subg_ref[...] * (1.0 - lam_init)
            o_ref[0, h * ATTN_STRIP:(h + 1) * ATTN_STRIP, :] = a.T.astype(o_ref.dtype)

    emit_output()

    acc_all = acc_s[...]
    bad = jnp.where(jnp.abs(acc_all) < jnp.inf, 0.0, 1.0)
    bad = jnp.maximum(bad, jnp.where(acc_all[:, V_HEAD_DIM:V_HEAD_DIM + 1] > 0.0, 0.0, 1.0))
    bad = jnp.max(jnp.max(bad, axis=0), axis=0, keepdims=True)
    bad = jnp.max(bad, axis=1, keepdims=True)[0, 0]

    @pl.when(bad > 0.0)
    def _():
        r_s[...] = jnp.full(r_s.shape, MASK_VALUE, F32)
        acc_s[...] = jnp.zeros(acc_s.shape, F32)
        tile_rows = bq + bk

        @pl.loop(0, n_far + n_near)
        def _(j):
            start = (j - n_far + jnp.where(qi == 0, 1, 0)) * bk
            in_tile = jnp.logical_and(j >= n_far, start + bk <= tile_rows)
            start = pl.multiple_of(jnp.clip(start, 0, tile_rows - bk), bk)
            bias = jnp.where(j < n_far, 0.0, jnp.where(in_tile, bias_ref[0, pl.ds(start, bk), :], MASK_VALUE))
            k = keys(j)
            vt1 = values(j)
            for n in range(len(strips)):
                s = logits(k, n, bias)
                m_old = r_s[n]
                m_new = jnp.maximum(m_old, jnp.max(s, axis=0, keepdims=True))
                p = jnp.exp2(s - m_new).astype(BF16)
                acc_s[n] = jnp.exp2(m_old - m_new) * acc_s[n] + jnp.dot(vt1, p, preferred_element_type=F32)
                r_s[n] = m_new

        emit_output()


def _attn_prompt(q, k, vt, bias, lam_vecs, sub_g_col, *, bq, bk, lam_init):
    b, s, _ = q.shape
    n_strips = 2 * (bq // ATTN_STRIP)
    assert bq % bk == 0 and bk >= MAX_DISTANCE and s >= bq + bk
    return pl.pallas_call(
        functools.partial(_attn_prompt_kernel, bq=bq, bk=bk, lam_init=lam_init),
        grid=(b, N_HEADS, s // bq),
        in_specs=[
            pl.BlockSpec((1, bq, LANES), lambda bi, h, qi: (bi, qi, h)),
            pl.BlockSpec((1, s, LANES), lambda bi, h, qi: (bi, 0, h)),
            pl.BlockSpec((1, V_HEAD_DIM, s), lambda bi, h, qi: (bi, h, 0)),
            pl.BlockSpec((1, bq + bk, bq), lambda bi, h, qi: (h, 0, 0)),
            pl.BlockSpec((4, HEAD_DIM), lambda bi, h, qi: (0, 0)),
            pl.BlockSpec((V_HEAD_DIM, 1), lambda bi, h, qi: (0, 0)),
        ],
        out_specs=pl.BlockSpec((1, bq, LANES), lambda bi, h, qi: (bi, qi, h)),
        out_shape=jax.ShapeDtypeStruct((b, s, N_HEADS * V_HEAD_DIM), BF16),
        scratch_shapes=[
            pltpu.VMEM((2, n_strips, bk, ATTN_STRIP), BF16),
            pltpu.VMEM((2, n_strips, 1, ATTN_STRIP), F32),
            pltpu.VMEM((n_strips, 1, ATTN_STRIP), F32),
            pltpu.VMEM((n_strips, V_HEAD_DIM + ONES_ROWS, ATTN_STRIP), F32),
        ],
        compiler_params=_params(("arbitrary", "arbitrary", "arbitrary")),
        name="attn_prompt",
    )(q, k, vt, bias, lam_vecs, sub_g_col)


def _attn_sample_kernel(q_ref, ck_ref, cv_ref, nk_ref, nv_ref, bias_ref, lam_ref, subg_ref, o_ref,
                        *, past, lam_init):
    t = q_ref.shape[1]
    lam = _diff_lambda(lam_ref, lam_init)
    for hd in range(N_HEADS):
        cs = slice(hd * LANES, (hd + 1) * LANES)
        qs = _split_q(q_ref[0, :, cs])
        bias = bias_ref[hd]
        bias2 = jnp.concatenate([bias, bias], axis=0)
        carry = (jnp.full((2 * t, 1), MASK_VALUE, F32), jnp.zeros((2 * t, 1), F32),
                 jnp.zeros((2 * t, V_HEAD_DIM), F32))
        kc = ck_ref[0, pl.ds(hd, past, stride=N_HEADS), :].astype(BF16)
        vc = cv_ref[0, pl.ds(hd, past, stride=N_HEADS), :].astype(BF16)
        carry = _softmax_step(_qk(qs, kc) + bias2[:, :past], vc, *carry)
        m, l, acc = _softmax_step(_qk(qs, nk_ref[0, :, cs]) + bias2[:, past:], nv_ref[0, :, cs], *carry)
        o_ref[0, :, cs] = _finish_heads(l, acc, lam, subg_ref[...], lam_init, t).astype(o_ref.dtype)


def _attn_sample(q, cache_k, cache_v, new_k, new_v, bias, lam_vecs, sub_g, *, lam_init):
    b, t, width = q.shape
    past = cache_k.shape[1] // N_HEADS
    flat_spec = pl.BlockSpec((1, t, width), lambda bi: (bi, 0, 0))
    cache_spec = pl.BlockSpec((1, past * N_HEADS, LANES), lambda bi: (bi, 0, 0))
    return pl.pallas_call(
        functools.partial(_attn_sample_kernel, past=past, lam_init=lam_init),
        grid=(b,),
        in_specs=[
            flat_spec, cache_spec, cache_spec, flat_spec, flat_spec,
            pl.BlockSpec((N_HEADS, t, past + t), lambda bi: (0, 0, 0)),
            pl.BlockSpec((4, HEAD_DIM), lambda bi: (0, 0)),
            pl.BlockSpec((1, V_HEAD_DIM), lambda bi: (0, 0)),
        ],
        out_specs=flat_spec,
        out_shape=jax.ShapeDtypeStruct((b, t, width), BF16),
        compiler_params=_params(("arbitrary",)),
        name="attn_sample",
    )(q, cache_k, cache_v, new_k, new_v, bias, lam_vecs, sub_g)


def _mixer_out_kernel(x_ref, m_ref, a_ref, gt_ref, w_ref, o_ref):
    half = m_ref.shape[-1]
    y = jnp.dot(m_ref[...], w_ref[:half, :], preferred_element_type=F32)
    y = y + jnp.dot(a_ref[...], w_ref[half:, :], preferred_element_type=F32)
    nb, r, d = x_ref.shape
    o_ref[...] = x_ref[...] + gt_ref[...] * y.reshape(nb, r, d)


def _mixer_out(x, m, a, gate, w_out_b, *, nb_blk, r_blk):
    nbat, r, d = x.shape
    nr = r // r_blk
    rows = nb_blk * r_blk
    half = m.shape[-1]
    return pl.pallas_call(
        _mixer_out_kernel,
        grid=((nbat // nb_blk) * nr,),
        in_specs=[
            pl.BlockSpec((nb_blk, r_blk, d), lambda i: (i // nr, i % nr, 0)),
            pl.BlockSpec((rows, half), lambda i: (i, 0)),
            pl.BlockSpec((rows, half), lambda i: (i, 0)),
            pl.BlockSpec((nb_blk, 1, d), lambda i: (i // nr, 0, 0)),
            pl.BlockSpec((d, d), lambda i: (0, 0)),
        ],
        out_specs=pl.BlockSpec((nb_blk, r_blk, d), lambda i: (i // nr, i % nr, 0)),
        out_shape=jax.ShapeDtypeStruct(x.shape, F32),
        compiler_params=_params(("arbitrary",)),
        name="mixer_out",
    )(x, m, a, gate[:, None, :], w_out_b)


def _ffn_kernel(x_ref, sh_ref, sc_ref, gt_ref, g_ref, wg_ref, wu_ref, wo_ref, gf_ref, shf_ref, scf_ref,
                o_ref, h_s, acc_s):
    f = pl.program_id(1)
    rows = h_s.shape[0]

    @pl.when(f == 0)
    def _():
        x = x_ref[...]
        y = x * lax.rsqrt(jnp.mean(x * x, axis=-1, keepdims=True) + EPS)
        hm = (y * g_ref[...]) * (1.0 + sc_ref[...]) + sh_ref[...]
        h_s[...] = hm.reshape(rows, hm.shape[-1]).astype(BF16)
        acc_s[...] = jnp.zeros_like(acc_s)

    h = h_s[...]
    zg = jnp.dot(h, wg_ref[...], preferred_element_type=F32)
    zu = jnp.dot(h, wu_ref[...], preferred_element_type=F32)
    act = (zg * jax.nn.sigmoid(zg) * zu).astype(BF16)
    acc_s[...] += jnp.dot(act, wo_ref[...], preferred_element_type=F32)

    @pl.when(f == pl.num_programs(1) - 1)
    def _():
        nb, r, d = x_ref.shape
        x2 = x_ref[...] + gt_ref[...] * acc_s[...].reshape(nb, r, d)
        y = x2 * lax.rsqrt(jnp.mean(x2 * x2, axis=-1, keepdims=True) + EPS)
        o_ref[...] = (y * gf_ref[...]) * (1.0 + scf_ref[...]) + shf_ref[...]


def _ffn(x, shift, scale, gate, g_ffn, w_in_b, w_out_b, g_final, shift_f, scale_f, *, nb_blk, r_blk, tf):
    nbat, r, d = x.shape
    d_ff = w_out_b.shape[0]
    nf = d_ff // tf
    nr = r // r_blk
    rows = nb_blk * r_blk
    x_spec = pl.BlockSpec((nb_blk, r_blk, d), lambda i, f: (i // nr, i % nr, 0))
    vec_spec = pl.BlockSpec((nb_blk, 1, d), lambda i, f: (i // nr, 0, 0))
    par_spec = pl.BlockSpec((1, 1, d), lambda i, f: (0, 0, 0))
    return pl.pallas_call(
        _ffn_kernel,
        grid=((nbat // nb_blk) * nr, nf),
        in_specs=[
            x_spec, vec_spec, vec_spec, vec_spec, par_spec,
            pl.BlockSpec((d, tf), lambda i, f: (0, f)),
            pl.BlockSpec((d, tf), lambda i, f: (0, f + nf)),
            pl.BlockSpec((tf, d), lambda i, f: (f, 0)),
            par_spec, vec_spec, vec_spec,
        ],
        out_specs=x_spec,
        out_shape=jax.ShapeDtypeStruct(x.shape, F32),
        scratch_shapes=[pltpu.VMEM((rows, d), BF16), pltpu.VMEM((rows, d), F32)],
        compiler_params=_params(("arbitrary", "arbitrary")),
        name="ffn",
    )(x, shift[:, None, :], scale[:, None, :], gate[:, None, :], g_ffn.reshape(1, 1, d),
      w_in_b, w_in_b, w_out_b, g_final.reshape(1, 1, d), shift_f[:, None, :], scale_f[:, None, :])


ATTN_Q_BLOCK = 1024
ATTN_K_BEAT = 512
PROMPT_ROWS = 512
SAMPLE_BATCH_BLOCK = 8
FFN_TILE = 512


def kernel(x_prompt, x_sample, cache_k, cache_v, c_prompt, c_sample, rel_bias, w_ada, b_ada, w_ada_final,
           b_ada_final, g_mix, g_ffn, g_final, w_in, mlp_ln_g, mlp_ln_b, w_s, b_s, lambda_q1, lambda_k1,
           lambda_q2, lambda_k2, sub_g, w_out, w_ffn_in, w_ffn_out):
    B, S, D = x_prompt.shape
    DB, T, _ = x_sample.shape
    depth = w_in.shape[0]
    past = cache_k.shape[2]
    width = N_HEADS * V_HEAD_DIM
    mlp_chunk = w_s.shape[-1]

    c_all = jnp.concatenate([c_prompt, c_sample], axis=0)
    mod_f = _adaln(c_all, w_ada_final, b_ada_final)
    bias_p = _rel_bias_tiles(rel_bias, ATTN_Q_BLOCK, ATTN_Q_BLOCK + ATTN_K_BEAT, ATTN_K_BEAT, 0, True, True)
    bias_s = _rel_bias_tiles(rel_bias, T, past + T, past, 0, False, False)

    assert depth == 1, "the final adaLN norm is fused into the single layer's FFN kernel"
    lam_init = 0.8 - 0.6 * math.exp(-0.3 * 0)
    lam_vecs = jnp.stack([lambda_q1[0], lambda_k1[0], lambda_q2[0], lambda_k2[0]])
    subg = sub_g.reshape(1, V_HEAD_DIM)
    mod = _adaln(c_all, w_ada[0], b_ada[0])
    sh1, sc1, gt1, sh2, sc2, gt2 = jnp.split(mod, 6, axis=-1)
    shf, scf = jnp.split(mod_f, 2, axis=-1)
    w_in_b = w_in[0].astype(BF16)
    w_out_b = w_out[0].astype(BF16)
    w_f_in_b = w_ffn_in[0].astype(BF16)
    w_f_out_b = w_ffn_out[0].astype(BF16)
    mixer_w = (g_mix[0], w_in_b, mlp_ln_g[0], mlp_ln_b[0], w_s[0], b_s[0])

    m, q, kp, kb, vp, vt = _mixer_in(x_prompt, sh1[:B], sc1[:B], *mixer_w, nb_blk=1, r_blk=PROMPT_ROWS,
                                     t_chunk=mlp_chunk, emit_gv=False, v_transposed=True)
    a = _attn_prompt(q.reshape(B, S, width), kb.reshape(B, S, width), vt, bias_p, lam_vecs,
                     subg.reshape(V_HEAD_DIM, 1), bq=ATTN_Q_BLOCK, bk=ATTN_K_BEAT, lam_init=lam_init)
    xp = _mixer_out(x_prompt, m, a.reshape(B * S, width), gt1[:B], w_out_b, nb_blk=1, r_blk=PROMPT_ROWS)
    yp = _ffn(xp, sh2[:B], sc2[:B], gt2[:B], g_ffn[0], w_f_in_b, w_f_out_b, g_final, shf[:B], scf[:B],
              nb_blk=1, r_blk=PROMPT_ROWS, tf=FFN_TILE)

    m, q, ks, kb, vs, vb, gvs = _mixer_in(x_sample, sh1[B:], sc1[B:], *mixer_w, nb_blk=SAMPLE_BATCH_BLOCK,
                                          r_blk=T, t_chunk=T, emit_gv=True, v_transposed=False)
    a = _attn_sample(q.reshape(DB, T, width), cache_k.reshape(DB, past * N_HEADS, LANES),
                     cache_v.reshape(DB, past * N_HEADS, LANES), kb.reshape(DB, T, width),
                     vb.reshape(DB, T, width), bias_s, lam_vecs, subg, lam_init=lam_init)
    xs = _mixer_out(x_sample, m, a.reshape(DB * T, width), gt1[B:], w_out_b, nb_blk=SAMPLE_BATCH_BLOCK, r_blk=T)
    ys = _ffn(xs, sh2[B:], sc2[B:], gt2[B:], g_ffn[0], w_f_in_b, w_f_out_b, g_final, shf[B:], scf[B:],
              nb_blk=SAMPLE_BATCH_BLOCK, r_blk=T, tf=FFN_TILE)

    head_shape = (N_HEADS, V_HEAD_DIM)
    return (yp, ys, kp.reshape(1, B, S, *head_shape), vp.reshape(1, B, S, *head_shape),
            ks.reshape(1, DB, T, *head_shape), vs.reshape(1, DB, T, *head_shape),
            gvs.reshape(1, DB, T, MLP_GROUPS, MLP_GROUP_DIM))
```

```python
import functools
import math

import jax
import jax.numpy as jnp
from jax import lax
from jax.experimental import pallas as pl
from jax.experimental.pallas import tpu as pltpu

LANES = 128
SUBLANES = 8
VMEM_LIMIT_BYTES = 56 * 1024 * 1024
MXU_WIDTH = 256

ATTN_STRIP = MXU_WIDTH
ONES_ROWS = SUBLANES
R_INIT_KEYS = 128

CHUNK = 64
N_HEADS = 8
HEAD_DIM = 64
V_HEAD_DIM = 128
MLP_GROUPS = 8
MLP_GROUP_DIM = 128
N_BUCKETS = 32
MAX_DISTANCE = 128
EPS = 1e-6
MASK_VALUE = -1e30
LOG2E = math.log2(math.e)

BF16 = jnp.bfloat16
F32 = jnp.float32


def _params(sem):
    return pltpu.CompilerParams(dimension_semantics=sem, vmem_limit_bytes=VMEM_LIMIT_BYTES)


def _adaln_kernel(c_ref, w_ref, b_ref, o_ref):
    c = c_ref[...]
    a = c * jax.nn.sigmoid(c)
    o_ref[...] = jnp.dot(a, w_ref[...], preferred_element_type=F32) + b_ref[...]


def _adaln(c, w, b, tn=1024):
    rows, d = c.shape
    n = w.shape[1]
    return pl.pallas_call(
        _adaln_kernel,
        grid=(n // tn,),
        in_specs=[
            pl.BlockSpec((rows, d), lambda j: (0, 0)),
            pl.BlockSpec((d, tn), lambda j: (0, j)),
            pl.BlockSpec((1, tn), lambda j: (0, j)),
        ],
        out_specs=pl.BlockSpec((rows, tn), lambda j: (0, j)),
        out_shape=jax.ShapeDtypeStruct((rows, n), F32),
        compiler_params=_params(("arbitrary",)),
        name="adaln",
    )(c, w, b.reshape(1, n))


def _rel_bias_kernel(tab_ref, o_ref, *, nq, nk, q_start, k_start, shift_far, keys_on_rows):
    h = pl.program_id(0)
    nb = N_BUCKETS // 2
    max_exact = nb // 2
    q_axis, k_axis = (1, 0) if keys_on_rows else (0, 1)
    rows, cols = (nk, nq) if keys_on_rows else (nq, nk)
    shift = tab_ref[nb - 1, h] if shift_far else 0.0

    def block(shape, q0, k0):
        q_pos = q0 + lax.broadcasted_iota(jnp.int32, shape, q_axis)
        k_pos = k0 + lax.broadcasted_iota(jnp.int32, shape, k_axis)
        rel = k_pos - q_pos
        ret = jnp.where(rel > 0, nb, 0)
        n = jnp.abs(rel)
        nf = jnp.maximum(n, 1).astype(F32)
        large = max_exact + (jnp.log(nf / max_exact) / math.log(MAX_DISTANCE / max_exact)
                             * (nb - max_exact)).astype(jnp.int32)
        large = jnp.minimum(large, nb - 1)
        bucket = ret + jnp.where(n < max_exact, n, large)
        bias = jnp.zeros(shape, F32)
        for bkt in range(N_BUCKETS):
            bias = jnp.where(bucket == bkt, tab_ref[bkt, h], bias)
        allowed = (k_pos // CHUNK) <= (q_pos // CHUNK)
        return jnp.where(allowed, (bias - shift) * LOG2E, MASK_VALUE)

    if rows % LANES or cols % LANES:
        o_ref[0] = block((rows, cols), q_start, k_start)
        return
    far_value = (tab_ref[nb - 1, h] - shift) * LOG2E
    for rb in range(rows // LANES):
        for cb in range(cols // LANES):
            k0 = k_start + LANES * (rb if keys_on_rows else cb)
            q0 = q_start + LANES * (cb if keys_on_rows else rb)
            sl = (0, slice(rb * LANES, (rb + 1) * LANES), slice(cb * LANES, (cb + 1) * LANES))
            if (k0 - q0) + (LANES - 1) <= -MAX_DISTANCE:
                o_ref[sl] = jnp.full((LANES, LANES), far_value, F32)
            elif (k0 - q0) - (LANES - 1) >= CHUNK:
                o_ref[sl] = jnp.full((LANES, LANES), MASK_VALUE, F32)
            else:
                o_ref[sl] = block((LANES, LANES), q0, k0)


def _rel_bias_tiles(rel_bias, nq, nk, q_start, k_start, shift_far, keys_on_rows):
    out_tile = (nk, nq) if keys_on_rows else (nq, nk)
    return pl.pallas_call(
        functools.partial(_rel_bias_kernel, nq=nq, nk=nk, q_start=q_start, k_start=k_start,
                          shift_far=shift_far, keys_on_rows=keys_on_rows),
        grid=(N_HEADS,),
        in_specs=[pl.BlockSpec(memory_space=pltpu.SMEM)],
        out_specs=pl.BlockSpec((1,) + out_tile, lambda h: (h, 0, 0)),
        out_shape=jax.ShapeDtypeStruct((N_HEADS,) + out_tile, F32),
        compiler_params=_params(("arbitrary",)),
        name="rel_bias",
    )(rel_bias)


def _mixer_in_kernel(x_ref, sh_ref, sc_ref, g_ref, w_ref, lng_ref, lnb_ref, ws_ref, bs_ref,
                     *refs, t_chunk, emit_gv, v_transposed):
    if emit_gv:
        m_ref, q_ref, k_ref, kb_ref, v_ref, vb_ref, gv_ref, h_s, u_s = refs
    else:
        m_ref, q_ref, k_ref, kb_ref, v_ref, vb_ref, h_s, u_s = refs
        gv_ref = None
    j = pl.program_id(1)
    rows = h_s.shape[0]

    def store_per_head(ref, val):
        for hd in range(N_HEADS):
            ref[pl.ds(hd, rows, stride=N_HEADS), :] = val[:, hd * LANES:(hd + 1) * LANES]

    @pl.when(j == 0)
    def _():
        x = x_ref[...]
        y = x * lax.rsqrt(jnp.mean(x * x, axis=-1, keepdims=True) + EPS)
        y = y * g_ref[...]
        hm = y * (1.0 + sc_ref[...]) + sh_ref[...]
        h_s[...] = hm.reshape(rows, hm.shape[-1]).astype(BF16)

    z = jnp.dot(h_s[...], w_ref[...], preferred_element_type=F32)

    @pl.when(j == 0)
    def _():
        u_s[...] = jax.nn.gelu(z)

    @pl.when(j == 1)
    def _():
        g = jax.nn.gelu(z)
        mu = jnp.mean(g, axis=-1, keepdims=True)
        var = jnp.mean(jnp.square(g - mu), axis=-1, keepdims=True)
        gv = (g - mu) * lax.rsqrt(var + EPS) * lng_ref[...] + lnb_ref[...]
        if emit_gv:
            store_per_head(gv_ref, gv)
        gvb = gv.astype(BF16)
        ii = lax.broadcasted_iota(jnp.int32, (t_chunk, t_chunk), 0)
        jj = lax.broadcasted_iota(jnp.int32, (t_chunk, t_chunk), 1)
        mask = (jj // CHUNK) <= (ii // CHUNK)
        for grp in range(MLP_GROUPS):
            wg = jnp.where(mask, ws_ref[grp], 0.0).astype(BF16)
            bg = bs_ref[grp]
            cs = slice(grp * MLP_GROUP_DIM, (grp + 1) * MLP_GROUP_DIM)
            for c in range(rows // t_chunk):
                rs = slice(c * t_chunk, (c + 1) * t_chunk)
                mixed = jnp.dot(wg, gvb[rs, cs], preferred_element_type=F32) + bg
                m_ref[rs, cs] = (u_s[rs, cs] * mixed).astype(BF16)

    @pl.when(j == 2)
    def _():
        q_ref[...] = (z * (HEAD_DIM ** -0.5 * LOG2E)).astype(BF16)

    @pl.when(j == 3)
    def _():
        store_per_head(k_ref, z)
        kb_ref[...] = z.astype(BF16)

    @pl.when(j == 4)
    def _():
        store_per_head(v_ref, z)
        if v_transposed:
            vb_ref[0] = z.T.astype(BF16)
        else:
            vb_ref[...] = z.astype(BF16)


def _mixer_in(x, shift, scale, g_mix, w_in_b, ln_g, ln_b, w_s, b_s, *, nb_blk, r_blk, t_chunk, emit_gv,
              v_transposed):
    nbat, r, d = x.shape
    width = 1024
    nr = r // r_blk
    n_tiles = (nbat // nb_blk) * nr
    rows = nb_blk * r_blk
    tokens = nbat * r
    row_idx = lambda i, j: (i, 0)
    flat = jax.ShapeDtypeStruct((tokens, width), BF16)
    per_head = jax.ShapeDtypeStruct((tokens * N_HEADS, LANES), F32)
    flat_spec = pl.BlockSpec((rows, width), row_idx)
    per_head_spec = pl.BlockSpec((rows * N_HEADS, LANES), row_idx)
    out_shape = [flat, flat, per_head, flat, per_head, flat]
    out_specs = [flat_spec, flat_spec, per_head_spec, flat_spec, per_head_spec, flat_spec]
    if v_transposed:
        assert nb_blk == 1
        out_shape[5] = jax.ShapeDtypeStruct((nbat, width, r), BF16)
        out_specs[5] = pl.BlockSpec((1, width, r_blk), lambda i, j: (i // nr, 0, i % nr))
    if emit_gv:
        out_shape.append(per_head)
        out_specs.append(per_head_spec)
    ws_t = w_s[:, :t_chunk, :t_chunk]
    bs_t = b_s[:, :t_chunk, None]
    return pl.pallas_call(
        functools.partial(_mixer_in_kernel, t_chunk=t_chunk, emit_gv=emit_gv, v_transposed=v_transposed),
        grid=(n_tiles, 5),
        in_specs=[
            pl.BlockSpec((nb_blk, r_blk, d), lambda i, j: (i // nr, i % nr, 0)),
            pl.BlockSpec((nb_blk, 1, d), lambda i, j: (i // nr, 0, 0)),
            pl.BlockSpec((nb_blk, 1, d), lambda i, j: (i // nr, 0, 0)),
            pl.BlockSpec((1, 1, d), lambda i, j: (0, 0, 0)),
            pl.BlockSpec((d, width), lambda i, j: (0, j)),
            pl.BlockSpec((1, width), lambda i, j: (0, 0)),
            pl.BlockSpec((1, width), lambda i, j: (0, 0)),
            pl.BlockSpec((MLP_GROUPS, t_chunk, t_chunk), lambda i, j: (0, 0, 0)),
            pl.BlockSpec((MLP_GROUPS, t_chunk, 1), lambda i, j: (0, 0, 0)),
        ],
        out_specs=out_specs,
        out_shape=out_shape,
        scratch_shapes=[pltpu.VMEM((rows, d), BF16), pltpu.VMEM((rows, width), F32)],
        compiler_params=_params(("arbitrary", "arbitrary")),
        name="mixer_in",
    )(x, shift[:, None, :], scale[:, None, :], g_mix.reshape(1, 1, d), w_in_b,
      ln_g.reshape(1, width), ln_b.reshape(1, width), ws_t, bs_t)


def _split_q(q):
    lane = lax.broadcasted_iota(jnp.int32, q.shape, 1)
    zero = jnp.zeros_like(q)
    return jnp.concatenate([jnp.where(lane < HEAD_DIM, q, zero), jnp.where(lane >= HEAD_DIM, q, zero)], axis=0)


def _softmax_step(s, v, m, l, acc):
    m_new = jnp.maximum(m, jnp.max(s, axis=-1, keepdims=True))
    alpha = jnp.exp2(m - m_new)
    p = jnp.exp2(s - m_new)
    l_new = alpha * l + jnp.sum(p, axis=-1, keepdims=True)
    acc_new = alpha * acc + jnp.dot(p.astype(BF16), v, preferred_element_type=F32)
    return m_new, l_new, acc_new


def _diff_lambda(lam_ref, lam_init):
    lv = lam_ref[...]
    s1 = jnp.sum(lv[0:1] * lv[1:2], axis=-1, keepdims=True)
    s2 = jnp.sum(lv[2:3] * lv[3:4], axis=-1, keepdims=True)
    return jnp.exp(s1) - jnp.exp(s2) + lam_init


def _finish_heads(l, acc, lam, subg, lam_init, n):
    o = acc / l
    a = o[:n] - lam * o[n:]
    a = a * lax.rsqrt(jnp.mean(a * a, axis=-1, keepdims=True) + EPS)
    return a * subg * (1.0 - lam_init)


def _qk(qs, k):
    return lax.dot_general(qs, k, (((1,), (1,)), ((), ())), preferred_element_type=F32)


def _attn_prompt_kernel(q_ref, k_ref, vt_ref, bias_ref, lam_ref, subg_ref, o_ref,
                        p_s, alpha_s, r_s, acc_s, *, bq, bk, lam_init):
    qi = pl.program_id(2)
    halves = bq // ATTN_STRIP
    strips = [(c, h) for c in range(2) for h in range(halves)]
    n_near = bq // bk + 1
    q = q_ref[0]
    lane = lax.broadcasted_iota(jnp.int32, q.shape, 1)
    zero = jnp.zeros_like(q)
    qs = (jnp.where(lane < HEAD_DIM, q, zero), jnp.where(lane >= HEAD_DIM, q, zero))

    ones = jnp.ones((ONES_ROWS, bk), BF16)

    def keys(j):
        return k_ref[0, pl.ds(pl.multiple_of(j * bk, bk), bk), :]

    def values(j):
        vt = vt_ref[0, :, pl.ds(pl.multiple_of(j * bk, bk), bk)]
        return jnp.concatenate([vt, ones], axis=0)

    def logits(k, n, bias):
        c, h = strips[n]
        cols = slice(h * ATTN_STRIP, (h + 1) * ATTN_STRIP)
        s = lax.dot_general(k, qs[c][cols], (((1,), (1,)), ((), ())), preferred_element_type=F32)
        if bias is not None:
            s = s + (bias if bias.ndim == 0 else bias[:, cols])
        return s

    def beat(new=None, old=None):
        if new is not None:
            k = keys(new[0])
        if old is not None:
            vt1 = values(old[0])
        for n in range(len(strips)):
            if new is not None and n not in new[3]:
                _, slot, bias, _ = new
                s = logits(k, n, bias)
                r = r_s[n]
                p_s[slot, n] = jnp.exp2(s - r).astype(BF16)
                r_new = jnp.maximum(r, jnp.max(s, axis=0, keepdims=True))
                alpha_s[slot, n] = jnp.exp2(r - r_new)
                r_s[n] = r_new
            if old is not None and n not in old[2]:
                slot = old[1]
                pv = jnp.dot(vt1, p_s[slot, n], preferred_element_type=F32)
                acc_s[n] = (acc_s[n] + pv) * alpha_s[slot, n]

    n_far = jnp.maximum((qi * bq - MAX_DISTANCE) // bk, 0)
    near_bias = []
    for i in range(n_near):
        shifted = bias_ref[0, (i + 1) * bk:(i + 2) * bk, :] if i + 1 < n_near else MASK_VALUE
        near_bias.append(jnp.where(qi == 0, shifted, bias_ref[0, i * bk:(i + 1) * bk, :]))

    k0 = k_ref[0, :R_INIT_KEYS, :]
    bias0 = jnp.where(qi == 0, near_bias[0][:R_INIT_KEYS], 0.0)
    for n in range(len(strips)):
        r_s[n] = jnp.max(logits(k0, n, bias0), axis=0, keepdims=True)
    acc_s[...] = jnp.zeros(acc_s.shape, F32)
    alpha_s[1] = jnp.ones(alpha_s.shape[1:], F32)
    p_s[1] = jnp.zeros(p_s.shape[1:], BF16)

    n_pairs = n_far // 2

    def pair(j):
        beat(new=(j, 0, None, ()), old=(jnp.maximum(j - 1, 0), 1, ()))
        beat(new=(j + 1, 1, None, ()), old=(j, 0, ()))

    n_quads = n_far // 4

    @pl.loop(0, n_quads)
    def _(t):
        pair(4 * t)
        pair(4 * t + 2)

    @pl.loop(2 * n_quads, n_pairs)
    def _(t):
        pair(2 * t)

    x = 2 * n_pairs
    tail = [(x, jnp.where(n_far % 2 == 1, 0.0, MASK_VALUE), ())]
    for i, bias_i in enumerate(near_bias):
        skip = tuple(n for n, (_, h) in enumerate(strips) if (i - 1) * bk >= (h + 1) * ATTN_STRIP)
        tail.append((n_far + i, bias_i, skip))
    for t, (j, bias_t, skip) in enumerate(tail):
        prev = (jnp.maximum(x - 1, 0), 1, ()) if t == 0 else (tail[t - 1][0], (t - 1) % 2, tail[t - 1][2])
        beat(new=(j, t % 2, bias_t, skip), old=prev)
    beat(old=(tail[-1][0], (len(tail) - 1) % 2, tail[-1][2]))

    lam = _diff_lambda(lam_ref, lam_init)

    def emit_output():
        for h in range(halves):
            o = []
            for c in range(2):
                acc = acc_s[c * halves + h]
                o.append(acc[:V_HEAD_DIM] / acc[V_HEAD_DIM:V_HEAD_DIM + 1])
            a = o[0] - lam * o[1]
            a = a * lax.rsqrt(jnp.mean(a * a, axis=0, keepdims=True) + EPS)
            a = a * subg_ref[...] * (1.0 - lam_init)
            o_ref[0, h * ATTN_STRIP:(h + 1) * ATTN_STRIP, :] = a.T.astype(o_ref.dtype)

    emit_output()

    acc_all = acc_s[...]
    bad = jnp.where(jnp.abs(acc_all) < jnp.inf, 0.0, 1.0)
    bad = jnp.maximum(bad, jnp.where(acc_all[:, V_HEAD_DIM:V_HEAD_DIM + 1] > 0.0, 0.0, 1.0))
    bad = jnp.max(jnp.max(bad, axis=0), axis=0, keepdims=True)
    bad = jnp.max(bad, axis=1, keepdims=True)[0, 0]

    @pl.when(bad > 0.0)
    def _():
        r_s[...] = jnp.full(r_s.shape, MASK_VALUE, F32)
        acc_s[...] = jnp.zeros(acc_s.shape, F32)
        tile_rows = bq + bk

        @pl.loop(0, n_far + n_near)
        def _(j):
            start = (j - n_far + jnp.where(qi == 0, 1, 0)) * bk
            in_tile = jnp.logical_and(j >= n_far, start + bk <= tile_rows)
            start = pl.multiple_of(jnp.clip(start, 0, tile_rows - bk), bk)
            bias = jnp.where(j < n_far, 0.0, jnp.where(in_tile, bias_ref[0, pl.ds(start, bk), :], MASK_VALUE))
            k = keys(j)
            vt1 = values(j)
            for n in range(len(strips)):
                s = logits(k, n, bias)
                m_old = r_s[n]
                m_new = jnp.maximum(m_old, jnp.max(s, axis=0, keepdims=True))
                p = jnp.exp2(s - m_new).astype(BF16)
                acc_s[n] = jnp.exp2(m_old - m_new) * acc_s[n] + jnp.dot(vt1, p, preferred_element_type=F32)
                r_s[n] = m_new

        emit_output()


def _attn_prompt(q, k, vt, bias, lam_vecs, sub_g_col, *, bq, bk, lam_init):
    b, s, _ = q.shape
    n_strips = 2 * (bq // ATTN_STRIP)
    assert bq % bk == 0 and bk >= MAX_DISTANCE and s >= bq + bk
    return pl.pallas_call(
        functools.partial(_attn_prompt_kernel, bq=bq, bk=bk, lam_init=lam_init),
        grid=(b, N_HEADS, s // bq),
        in_specs=[
            pl.BlockSpec((1, bq, LANES), lambda bi, h, qi: (bi, qi, h)),
            pl.BlockSpec((1, s, LANES), lambda bi, h, qi: (bi, 0, h)),
            pl.BlockSpec((1, V_HEAD_DIM, s), lambda bi, h, qi: (bi, h, 0)),
            pl.BlockSpec((1, bq + bk, bq), lambda bi, h, qi: (h, 0, 0)),
            pl.BlockSpec((4, HEAD_DIM), lambda bi, h, qi: (0, 0)),
            pl.BlockSpec((V_HEAD_DIM, 1), lambda bi, h, qi: (0, 0)),
        ],
        out_specs=pl.BlockSpec((1, bq, LANES), lambda bi, h, qi: (bi, qi, h)),
        out_shape=jax.ShapeDtypeStruct((b, s, N_HEADS * V_HEAD_DIM), BF16),
        scratch_shapes=[
            pltpu.VMEM((2, n_strips, bk, ATTN_STRIP), BF16),
            pltpu.VMEM((2, n_strips, 1, ATTN_STRIP), F32),
            pltpu.VMEM((n_strips, 1, ATTN_STRIP), F32),
            pltpu.VMEM((n_strips, V_HEAD_DIM + ONES_ROWS, ATTN_STRIP), F32),
        ],
        compiler_params=_params(("arbitrary", "arbitrary", "arbitrary")),
        name="attn_prompt",
    )(q, k, vt, bias, lam_vecs, sub_g_col)


def _attn_sample_kernel(q_ref, ck_ref, cv_ref, nk_ref, nv_ref, bias_ref, lam_ref, subg_ref, o_ref,
                        *, past, lam_init):
    t = q_ref.shape[1]
    lam = _diff_lambda(lam_ref, lam_init)
    for hd in range(N_HEADS):
        cs = slice(hd * LANES, (hd + 1) * LANES)
        qs = _split_q(q_ref[0, :, cs])
        bias = bias_ref[hd]
        bias2 = jnp.concatenate([bias, bias], axis=0)
        carry = (jnp.full((2 * t, 1), MASK_VALUE, F32), jnp.zeros((2 * t, 1), F32),
                 jnp.zeros((2 * t, V_HEAD_DIM), F32))
        kc = ck_ref[0, pl.ds(hd, past, stride=N_HEADS), :].astype(BF16)
        vc = cv_ref[0, pl.ds(hd, past, stride=N_HEADS), :].astype(BF16)
        carry = _softmax_step(_qk(qs, kc) + bias2[:, :past], vc, *carry)
        m, l, acc = _softmax_step(_qk(qs, nk_ref[0, :, cs]) + bias2[:, past:], nv_ref[0, :, cs], *carry)
        o_ref[0, :, cs] = _finish_heads(l, acc, lam, subg_ref[...], lam_init, t).astype(o_ref.dtype)


def _attn_sample(q, cache_k, cache_v, new_k, new_v, bias, lam_vecs, sub_g, *, lam_init):
    b, t, width = q.shape
    past = cache_k.shape[1] // N_HEADS
    flat_spec = pl.BlockSpec((1, t, width), lambda bi: (bi, 0, 0))
    cache_spec = pl.BlockSpec((1, past * N_HEADS, LANES), lambda bi: (bi, 0, 0))
    return pl.pallas_call(
        functools.partial(_attn_sample_kernel, past=past, lam_init=lam_init),
        grid=(b,),
        in_specs=[
            flat_spec, cache_spec, cache_spec, flat_spec, flat_spec,
            pl.BlockSpec((N_HEADS, t, past + t), lambda bi: (0, 0, 0)),
            pl.BlockSpec((4, HEAD_DIM), lambda bi: (0, 0)),
            pl.BlockSpec((1, V_HEAD_DIM), lambda bi: (0, 0)),
        ],
        out_specs=flat_spec,
        out_shape=jax.ShapeDtypeStruct((b, t, width), BF16),
        compiler_params=_params(("arbitrary",)),
        name="attn_sample",
    )(q, cache_k, cache_v, new_k, new_v, bias, lam_vecs, sub_g)


def _mixer_out_kernel(x_ref, m_ref, a_ref, gt_ref, w_ref, o_ref):
    half = m_ref.shape[-1]
    y = jnp.dot(m_ref[...], w_ref[:half, :], preferred_element_type=F32)
    y = y + jnp.dot(a_ref[...], w_ref[half:, :], preferred_element_type=F32)
    nb, r, d = x_ref.shape
    o_ref[...] = x_ref[...] + gt_ref[...] * y.reshape(nb, r, d)


def _mixer_out(x, m, a, gate, w_out_b, *, nb_blk, r_blk):
    nbat, r, d = x.shape
    nr = r // r_blk
    rows = nb_blk * r_blk
    half = m.shape[-1]
    return pl.pallas_call(
        _mixer_out_kernel,
        grid=((nbat // nb_blk) * nr,),
        in_specs=[
            pl.BlockSpec((nb_blk, r_blk, d), lambda i: (i // nr, i % nr, 0)),
            pl.BlockSpec((rows, half), lambda i: (i, 0)),
            pl.BlockSpec((rows, half), lambda i: (i, 0)),
            pl.BlockSpec((nb_blk, 1, d), lambda i: (i // nr, 0, 0)),
            pl.BlockSpec((d, d), lambda i: (0, 0)),
        ],
        out_specs=pl.BlockSpec((nb_blk, r_blk, d), lambda i: (i // nr, i % nr, 0)),
        out_shape=jax.ShapeDtypeStruct(x.shape, F32),
        compiler_params=_params(("arbitrary",)),
        name="mixer_out",
    )(x, m, a, gate[:, None, :], w_out_b)


def _ffn_kernel(x_ref, sh_ref, sc_ref, gt_ref, g_ref, wg_ref, wu_ref, wo_ref, gf_ref, shf_ref, scf_ref,
                o_ref, h_s, acc_s):
    f = pl.program_id(1)
    rows = h_s.shape[0]

    @pl.when(f == 0)
    def _():
        x = x_ref[...]
        y = x * lax.rsqrt(jnp.mean(x * x, axis=-1, keepdims=True) + EPS)
        hm = (y * g_ref[...]) * (1.0 + sc_ref[...]) + sh_ref[...]
        h_s[...] = hm.reshape(rows, hm.shape[-1]).astype(BF16)
        acc_s[...] = jnp.zeros_like(acc_s)

    h = h_s[...]
    zg = jnp.dot(h, wg_ref[...], preferred_element_type=F32)
    zu = jnp.dot(h, wu_ref[...], preferred_element_type=F32)
    act = (zg * jax.nn.sigmoid(zg) * zu).astype(BF16)
    acc_s[...] += jnp.dot(act, wo_ref[...], preferred_element_type=F32)

    @pl.when(f == pl.num_programs(1) - 1)
    def _():
        nb, r, d = x_ref.shape
        x2 = x_ref[...] + gt_ref[...] * acc_s[...].reshape(nb, r, d)
        y = x2 * lax.rsqrt(jnp.mean(x2 * x2, axis=-1, keepdims=True) + EPS)
        o_ref[...] = (y * gf_ref[...]) * (1.0 + scf_ref[...]) + shf_ref[...]


def _ffn(x, shift, scale, gate, g_ffn, w_in_b, w_out_b, g_final, shift_f, scale_f, *, nb_blk, r_blk, tf):
    nbat, r, d = x.shape
    d_ff = w_out_b.shape[0]
    nf = d_ff // tf
    nr = r // r_blk
    rows = nb_blk * r_blk
    x_spec = pl.BlockSpec((nb_blk, r_blk, d), lambda i, f: (i // nr, i % nr, 0))
    vec_spec = pl.BlockSpec((nb_blk, 1, d), lambda i, f: (i // nr, 0, 0))
    par_spec = pl.BlockSpec((1, 1, d), lambda i, f: (0, 0, 0))
    return pl.pallas_call(
        _ffn_kernel,
        grid=((nbat // nb_blk) * nr, nf),
        in_specs=[
            x_spec, vec_spec, vec_spec, vec_spec, par_spec,
            pl.BlockSpec((d, tf), lambda i, f: (0, f)),
            pl.BlockSpec((d, tf), lambda i, f: (0, f + nf)),
            pl.BlockSpec((tf, d), lambda i, f: (f, 0)),
            par_spec, vec_spec, vec_spec,
        ],
        out_specs=x_spec,
        out_shape=jax.ShapeDtypeStruct(x.shape, F32),
        scratch_shapes=[pltpu.VMEM((rows, d), BF16), pltpu.VMEM((rows, d), F32)],
        compiler_params=_params(("arbitrary", "arbitrary")),
        name="ffn",
    )(x, shift[:, None, :], scale[:, None, :], gate[:, None, :], g_ffn.reshape(1, 1, d),
      w_in_b, w_in_b, w_out_b, g_final.reshape(1, 1, d), shift_f[:, None, :], scale_f[:, None, :])


ATTN_Q_BLOCK = 1024
ATTN_K_BEAT = 512
PROMPT_ROWS = 512
SAMPLE_BATCH_BLOCK = 8
FFN_TILE = 512


def kernel(x_prompt, x_sample, cache_k, cache_v, c_prompt, c_sample, rel_bias, w_ada, b_ada, w_ada_final,
           b_ada_final, g_mix, g_ffn, g_final, w_in, mlp_ln_g, mlp_ln_b, w_s, b_s, lambda_q1, lambda_k1,
           lambda_q2, lambda_k2, sub_g, w_out, w_ffn_in, w_ffn_out):
    B, S, D = x_prompt.shape
    DB, T, _ = x_sample.shape
    depth = w_in.shape[0]
    past = cache_k.shape[2]
    width = N_HEADS * V_HEAD_DIM
    mlp_chunk = w_s.shape[-1]

    c_all = jnp.concatenate([c_prompt, c_sample], axis=0)
    mod_f = _adaln(c_all, w_ada_final, b_ada_final)
    bias_p = _rel_bias_tiles(rel_bias, ATTN_Q_BLOCK, ATTN_Q_BLOCK + ATTN_K_BEAT, ATTN_K_BEAT, 0, True, True)
    bias_s = _rel_bias_tiles(rel_bias, T, past + T, past, 0, False, False)

    assert depth == 1, "the final adaLN norm is fused into the single layer's FFN kernel"
    lam_init = 0.8 - 0.6 * math.exp(-0.3 * 0)
    lam_vecs = jnp.stack([lambda_q1[0], lambda_k1[0], lambda_q2[0], lambda_k2[0]])
    subg = sub_g.reshape(1, V_HEAD_DIM)
    mod = _adaln(c_all, w_ada[0], b_ada[0])
    sh1, sc1, gt1, sh2, sc2, gt2 = jnp.split(mod, 6, axis=-1)
    shf, scf = jnp.split(mod_f, 2, axis=-1)
    w_in_b = w_in[0].astype(BF16)
    w_out_b = w_out[0].astype(BF16)
    w_f_in_b = w_ffn_in[0].astype(BF16)
    w_f_out_b = w_ffn_out[0].astype(BF16)
    mixer_w = (g_mix[0], w_in_b, mlp_ln_g[0], mlp_ln_b[0], w_s[0], b_s[0])

    m, q, kp, kb, vp, vt = _mixer_in(x_prompt, sh1[:B], sc1[:B], *mixer_w, nb_blk=1, r_blk=PROMPT_ROWS,
                                     t_chunk=mlp_chunk, emit_gv=False, v_transposed=True)
    a = _attn_prompt(q.reshape(B, S, width), kb.reshape(B, S, width), vt, bias_p, lam_vecs,
                     subg.reshape(V_HEAD_DIM, 1), bq=ATTN_Q_BLOCK, bk=ATTN_K_BEAT, lam_init=lam_init)
    xp = _mixer_out(x_prompt, m, a.reshape(B * S, width), gt1[:B], w_out_b, nb_blk=1, r_blk=PROMPT_ROWS)
    yp = _ffn(xp, sh2[:B], sc2[:B], gt2[:B], g_ffn[0], w_f_in_b, w_f_out_b, g_final, shf[:B], scf[:B],
              nb_blk=1, r_blk=PROMPT_ROWS, tf=FFN_TILE)

    m, q, ks, kb, vs, vb, gvs = _mixer_in(x_sample, sh1[B:], sc1[B:], *mixer_w, nb_blk=SAMPLE_BATCH_BLOCK,
                                          r_blk=T, t_chunk=T, emit_gv=True, v_transposed=False)
    a = _attn_sample(q.reshape(DB, T, width), cache_k.reshape(DB, past * N_HEADS, LANES),
                     cache_v.reshape(DB, past * N_HEADS, LANES), kb.reshape(DB, T, width),
                     vb.reshape(DB, T, width), bias_s, lam_vecs, subg, lam_init=lam_init)
    xs = _mixer_out(x_sample, m, a.reshape(DB * T, width), gt1[B:], w_out_b, nb_blk=SAMPLE_BATCH_BLOCK, r_blk=T)
    ys = _ffn(xs, sh2[B:], sc2[B:], gt2[B:], g_ffn[0], w_f_in_b, w_f_out_b, g_final, shf[B:], scf[B:],
              nb_blk=SAMPLE_BATCH_BLOCK, r_blk=T, tf=FFN_TILE)

    head_shape = (N_HEADS, V_HEAD_DIM)
    return (yp, ys, kp.reshape(1, B, S, *head_shape), vp.reshape(1, B, S, *head_shape),
            ks.reshape(1, DB, T, *head_shape), vs.reshape(1, DB, T, *head_shape),
            gvs.reshape(1, DB, T, MLP_GROUPS, MLP_GROUP_DIM))
```

```python
import functools
import math

import jax
import jax.numpy as jnp
from jax import lax
from jax.experimental import pallas as pl
from jax.experimental.pallas import tpu as pltpu

LANES = 128
SUBLANES = 8
VMEM_LIMIT_BYTES = 56 * 1024 * 1024
MXU_WIDTH = 256

ATTN_STRIP = MXU_WIDTH
ONES_ROWS = SUBLANES
R_INIT_KEYS = 128

CHUNK = 64
N_HEADS = 8
HEAD_DIM = 64
V_HEAD_DIM = 128
MLP_GROUPS = 8
MLP_GROUP_DIM = 128
N_BUCKETS = 32
MAX_DISTANCE = 128
EPS = 1e-6
MASK_VALUE = -1e30
LOG2E = math.log2(math.e)

BF16 = jnp.bfloat16
F32 = jnp.float32


def _params(sem):
    return pltpu.CompilerParams(dimension_semantics=sem, vmem_limit_bytes=VMEM_LIMIT_BYTES)


def _adaln_kernel(c_ref, w_ref, b_ref, o_ref):
    c = c_ref[...]
    a = c * jax.nn.sigmoid(c)
    o_ref[...] = jnp.dot(a, w_ref[...], preferred_element_type=F32) + b_ref[...]


def _adaln(c, w, b, tn=1024):
    rows, d = c.shape
    n = w.shape[1]
    return pl.pallas_call(
        _adaln_kernel,
        grid=(n // tn,),
        in_specs=[
            pl.BlockSpec((rows, d), lambda j: (0, 0)),
            pl.BlockSpec((d, tn), lambda j: (0, j)),
            pl.BlockSpec((1, tn), lambda j: (0, j)),
        ],
        out_specs=pl.BlockSpec((rows, tn), lambda j: (0, j)),
        out_shape=jax.ShapeDtypeStruct((rows, n), F32),
        compiler_params=_params(("arbitrary",)),
        name="adaln",
    )(c, w, b.reshape(1, n))


def _rel_bias_kernel(tab_ref, o_ref, *, nq, nk, q_start, k_start, shift_far, keys_on_rows):
    h = pl.program_id(0)
    nb = N_BUCKETS // 2
    max_exact = nb // 2
    q_axis, k_axis = (1, 0) if keys_on_rows else (0, 1)
    rows, cols = (nk, nq) if keys_on_rows else (nq, nk)
    shift = tab_ref[nb - 1, h] if shift_far else 0.0

    def block(shape, q0, k0):
        q_pos = q0 + lax.broadcasted_iota(jnp.int32, shape, q_axis)
        k_pos = k0 + lax.broadcasted_iota(jnp.int32, shape, k_axis)
        rel = k_pos - q_pos
        ret = jnp.where(rel > 0, nb, 0)
        n = jnp.abs(rel)
        nf = jnp.maximum(n, 1).astype(F32)
        large = max_exact + (jnp.log(nf / max_exact) / math.log(MAX_DISTANCE / max_exact)
                             * (nb - max_exact)).astype(jnp.int32)
        large = jnp.minimum(large, nb - 1)
        bucket = ret + jnp.where(n < max_exact, n, large)
        bias = jnp.zeros(shape, F32)
        for bkt in range(N_BUCKETS):
            bias = jnp.where(bucket == bkt, tab_ref[bkt, h], bias)
        allowed = (k_pos // CHUNK) <= (q_pos // CHUNK)
        return jnp.where(allowed, (bias - shift) * LOG2E, MASK_VALUE)

    if rows % LANES or cols % LANES:
        o_ref[0] = block((rows, cols), q_start, k_start)
        return
    far_value = (tab_ref[nb - 1, h] - shift) * LOG2E
    for rb in range(rows // LANES):
        for cb in range(cols // LANES):
            k0 = k_start + LANES * (rb if keys_on_rows else cb)
            q0 = q_start + LANES * (cb if keys_on_rows else rb)
            sl = (0, slice(rb * LANES, (rb + 1) * LANES), slice(cb * LANES, (cb + 1) * LANES))
            if (k0 - q0) + (LANES - 1) <= -MAX_DISTANCE:
                o_ref[sl] = jnp.full((LANES, LANES), far_value, F32)
            elif (k0 - q0) - (LANES - 1) >= CHUNK:
                o_ref[sl] = jnp.full((LANES, LANES), MASK_VALUE, F32)
            else:
                o_ref[sl] = block((LANES, LANES), q0, k0)


def _rel_bias_tiles(rel_bias, nq, nk, q_start, k_start, shift_far, keys_on_rows):
    out_tile = (nk, nq) if keys_on_rows else (nq, nk)
    return pl.pallas_call(
        functools.partial(_rel_bias_kernel, nq=nq, nk=nk, q_start=q_start, k_start=k_start,
                          shift_far=shift_far, keys_on_rows=keys_on_rows),
        grid=(N_HEADS,),
        in_specs=[pl.BlockSpec(memory_space=pltpu.SMEM)],
        out_specs=pl.BlockSpec((1,) + out_tile, lambda h: (h, 0, 0)),
        out_shape=jax.ShapeDtypeStruct((N_HEADS,) + out_tile, F32),
        compiler_params=_params(("arbitrary",)),
        name="rel_bias",
    )(rel_bias)


def _mixer_in_kernel(x_ref, sh_ref, sc_ref, g_ref, w_ref, lng_ref, lnb_ref, ws_ref, bs_ref,
                     *refs, t_chunk, emit_gv, v_transposed):
    if emit_gv:
        m_ref, q_ref, k_ref, kb_ref, v_ref, vb_ref, gv_ref, u_s = refs
    else:
        m_ref, q_ref, k_ref, kb_ref, v_ref, vb_ref, u_s = refs
        gv_ref = None
    rows = u_s.shape[0]
    width = u_s.shape[1]

    def store_per_head(ref, val):
        for hd in range(N_HEADS):
            ref[pl.ds(hd, rows, stride=N_HEADS), :] = val[:, hd * LANES:(hd + 1) * LANES]

    x = x_ref[...]
    y = x * lax.rsqrt(jnp.mean(x * x, axis=-1, keepdims=True) + EPS)
    y = y * g_ref[...]
    hm = y * (1.0 + sc_ref[...]) + sh_ref[...]
    h = hm.reshape(rows, hm.shape[-1]).astype(BF16)

    def project(seg):
        return jnp.dot(h, w_ref[:, seg * width:(seg + 1) * width], preferred_element_type=F32)

    u_s[...] = jax.nn.gelu(project(0))

    g = jax.nn.gelu(project(1))
    mu = jnp.mean(g, axis=-1, keepdims=True)
    var = jnp.mean(jnp.square(g - mu), axis=-1, keepdims=True)
    gv = (g - mu) * lax.rsqrt(var + EPS) * lng_ref[...] + lnb_ref[...]
    if emit_gv:
        store_per_head(gv_ref, gv)
    gvb = gv.astype(BF16)
    ii = lax.broadcasted_iota(jnp.int32, (t_chunk, t_chunk), 0)
    jj = lax.broadcasted_iota(jnp.int32, (t_chunk, t_chunk), 1)
    mask = (jj // CHUNK) <= (ii // CHUNK)
    for grp in range(MLP_GROUPS):
        wg = jnp.where(mask, ws_ref[grp], 0.0).astype(BF16)
        bg = bs_ref[grp]
        cs = slice(grp * MLP_GROUP_DIM, (grp + 1) * MLP_GROUP_DIM)
        for c in range(rows // t_chunk):
            rs = slice(c * t_chunk, (c + 1) * t_chunk)
            mixed = jnp.dot(wg, gvb[rs, cs], preferred_element_type=F32) + bg
            m_ref[rs, cs] = (u_s[rs, cs] * mixed).astype(BF16)

    q_ref[...] = (project(2) * (HEAD_DIM ** -0.5 * LOG2E)).astype(BF16)

    z = project(3)
    store_per_head(k_ref, z)
    kb_ref[...] = z.astype(BF16)

    z = project(4)
    store_per_head(v_ref, z)
    if v_transposed:
        vb_ref[0] = z.T.astype(BF16)
    else:
        vb_ref[...] = z.astype(BF16)


def _mixer_in(x, shift, scale, g_mix, w_in_b, ln_g, ln_b, w_s, b_s, *, nb_blk, r_blk, t_chunk, emit_gv,
              v_transposed):
    nbat, r, d = x.shape
    width = 1024
    nr = r // r_blk
    n_tiles = (nbat // nb_blk) * nr
    rows = nb_blk * r_blk
    tokens = nbat * r
    row_idx = lambda i: (i, 0)
    flat = jax.ShapeDtypeStruct((tokens, width), BF16)
    per_head = jax.ShapeDtypeStruct((tokens * N_HEADS, LANES), F32)
    flat_spec = pl.BlockSpec((rows, width), row_idx)
    per_head_spec = pl.BlockSpec((rows * N_HEADS, LANES), row_idx)
    out_shape = [flat, flat, per_head, flat, per_head, flat]
    out_specs = [flat_spec, flat_spec, per_head_spec, flat_spec, per_head_spec, flat_spec]
    if v_transposed:
        assert nb_blk == 1
        out_shape[5] = jax.ShapeDtypeStruct((nbat, width, r), BF16)
        out_specs[5] = pl.BlockSpec((1, width, r_blk), lambda i: (i // nr, 0, i % nr))
    if emit_gv:
        out_shape.append(per_head)
        out_specs.append(per_head_spec)
    ws_t = w_s[:, :t_chunk, :t_chunk]
    bs_t = b_s[:, :t_chunk, None]
    once = pl.Buffered(1)
    return pl.pallas_call(
        functools.partial(_mixer_in_kernel, t_chunk=t_chunk, emit_gv=emit_gv, v_transposed=v_transposed),
        grid=(n_tiles,),
        in_specs=[
            pl.BlockSpec((nb_blk, r_blk, d), lambda i: (i // nr, i % nr, 0)),
            pl.BlockSpec((nb_blk, 1, d), lambda i: (i // nr, 0, 0)),
            pl.BlockSpec((nb_blk, 1, d), lambda i: (i // nr, 0, 0)),
            pl.BlockSpec((1, 1, d), lambda i: (0, 0, 0)),
            pl.BlockSpec(w_in_b.shape, lambda i: (0, 0), pipeline_mode=once),
            pl.BlockSpec((1, width), lambda i: (0, 0)),
            pl.BlockSpec((1, width), lambda i: (0, 0)),
            pl.BlockSpec((MLP_GROUPS, t_chunk, t_chunk), lambda i: (0, 0, 0)),
            pl.BlockSpec((MLP_GROUPS, t_chunk, 1), lambda i: (0, 0, 0)),
        ],
        out_specs=out_specs,
        out_shape=out_shape,
        scratch_shapes=[pltpu.VMEM((rows, width), F32)],
        compiler_params=_params(("arbitrary",)),
        name="mixer_in",
    )(x, shift[:, None, :], scale[:, None, :], g_mix.reshape(1, 1, d), w_in_b,
      ln_g.reshape(1, width), ln_b.reshape(1, width), ws_t, bs_t)


def _split_q(q):
    lane = lax.broadcasted_iota(jnp.int32, q.shape, 1)
    zero = jnp.zeros_like(q)
    return jnp.concatenate([jnp.where(lane < HEAD_DIM, q, zero), jnp.where(lane >= HEAD_DIM, q, zero)], axis=0)


def _softmax_step(s, v, m, l, acc):
    m_new = jnp.maximum(m, jnp.max(s, axis=-1, keepdims=True))
    alpha = jnp.exp2(m - m_new)
    p = jnp.exp2(s - m_new)
    l_new = alpha * l + jnp.sum(p, axis=-1, keepdims=True)
    acc_new = alpha * acc + jnp.dot(p.astype(BF16), v, preferred_element_type=F32)
    return m_new, l_new, acc_new


def _diff_lambda(lam_ref, lam_init):
    lv = lam_ref[...]
    s1 = jnp.sum(lv[0:1] * lv[1:2], axis=-1, keepdims=True)
    s2 = jnp.sum(lv[2:3] * lv[3:4], axis=-1, keepdims=True)
    return jnp.exp(s1) - jnp.exp(s2) + lam_init


def _finish_heads(l, acc, lam, subg, lam_init, n):
    o = acc / l
    a = o[:n] - lam * o[n:]
    a = a * lax.rsqrt(jnp.mean(a * a, axis=-1, keepdims=True) + EPS)
    return a * subg * (1.0 - lam_init)


def _qk(qs, k):
    return lax.dot_general(qs, k, (((1,), (1,)), ((), ())), preferred_element_type=F32)


def _attn_prompt_kernel(q_ref, k_ref, vt_ref, bias_ref, lam_ref, subg_ref, o_ref,
                        p_s, alpha_s, r_s, acc_s, *, bq, bk, lam_init):
    qi = pl.program_id(2)
    halves = bq // ATTN_STRIP
    strips = [(c, h) for c in range(2) for h in range(halves)]
    n_near = bq // bk + 1
    q = q_ref[0]
    lane = lax.broadcasted_iota(jnp.int32, q.shape, 1)
    zero = jnp.zeros_like(q)
    qs = (jnp.where(lane < HEAD_DIM, q, zero), jnp.where(lane >= HEAD_DIM, q, zero))

    ones = jnp.ones((ONES_ROWS, bk), BF16)

    def keys(j):
        return k_ref[0, pl.ds(pl.multiple_of(j * bk, bk), bk), :]

    def values(j):
        vt = vt_ref[0, :, pl.ds(pl.multiple_of(j * bk, bk), bk)]
        return jnp.concatenate([vt, ones], axis=0)

    def logits(k, n, bias):
        c, h = strips[n]
        cols = slice(h * ATTN_STRIP, (h + 1) * ATTN_STRIP)
        s = lax.dot_general(k, qs[c][cols], (((1,), (1,)), ((), ())), preferred_element_type=F32)
        if bias is not None:
            s = s + (bias if bias.ndim == 0 else bias[:, cols])
        return s

    def beat(new=None, old=None):
        if new is not None:
            k = keys(new[0])
        if old is not None:
            vt1 = values(old[0])
        for n in range(len(strips)):
            if new is not None and n not in new[3]:
                _, slot, bias, _ = new
                s = logits(k, n, bias)
                r = r_s[n]
                p_s[slot, n] = jnp.exp2(s - r).astype(BF16)
                r_new = jnp.maximum(r, jnp.max(s, axis=0, keepdims=True))
                alpha_s[slot, n] = jnp.exp2(r - r_new)
                r_s[n] = r_new
            if old is not None and n not in old[2]:
                slot = old[1]
                pv = jnp.dot(vt1, p_s[slot, n], preferred_element_type=F32)
                acc_s[n] = (acc_s[n] + pv) * alpha_s[slot, n]

    n_far = jnp.maximum((qi * bq - MAX_DISTANCE) // bk, 0)
    near_bias = []
    for i in range(n_near):
        shifted = bias_ref[0, (i + 1) * bk:(i + 2) * bk, :] if i + 1 < n_near else MASK_VALUE
        near_bias.append(jnp.where(qi == 0, shifted, bias_ref[0, i * bk:(i + 1) * bk, :]))

    k0 = k_ref[0, :R_INIT_KEYS, :]
    bias0 = jnp.where(qi == 0, near_bias[0][:R_INIT_KEYS], 0.0)
    for n in range(len(strips)):
        r_s[n] = jnp.max(logits(k0, n, bias0), axis=0, keepdims=True)
    acc_s[...] = jnp.zeros(acc_s.shape, F32)
    alpha_s[1] = jnp.ones(alpha_s.shape[1:], F32)
    p_s[1] = jnp.zeros(p_s.shape[1:], BF16)

    n_pairs = n_far // 2

    def pair(j):
        beat(new=(j, 0, None, ()), old=(jnp.maximum(j - 1, 0), 1, ()))
        beat(new=(j + 1, 1, None, ()), old=(j, 0, ()))

    n_quads = n_far // 4

    @pl.loop(0, n_quads)
    def _(t):
        pair(4 * t)
        pair(4 * t + 2)

    @pl.loop(2 * n_quads, n_pairs)
    def _(t):
        pair(2 * t)

    x = 2 * n_pairs
    tail = [(x, jnp.where(n_far % 2 == 1, 0.0, MASK_VALUE), ())]
    for i, bias_i in enumerate(near_bias):
        skip = tuple(n for n, (_, h) in enumerate(strips) if (i - 1) * bk >= (h + 1) * ATTN_STRIP)
        tail.append((n_far + i, bias_i, skip))
    for t, (j, bias_t, skip) in enumerate(tail):
        prev = (jnp.maximum(x - 1, 0), 1, ()) if t == 0 else (tail[t - 1][0], (t - 1) % 2, tail[t - 1][2])
        beat(new=(j, t % 2, bias_t, skip), old=prev)
    beat(old=(tail[-1][0], (len(tail) - 1) % 2, tail[-1][2]))

    lam = _diff_lambda(lam_ref, lam_init)

    def emit_output():
        for h in range(halves):
            o = []
            for c in range(2):
                acc = acc_s[c * halves + h]
                o.append(acc[:V_HEAD_DIM] / acc[V_HEAD_DIM:V_HEAD_DIM + 1])
            a = o[0] - lam * o[1]
            a = a * lax.rsqrt(jnp.mean(a * a, axis=0, keepdims=True) + EPS)
            a = a * subg_ref[...] * (1.0 - lam_init)
            o_ref[0, h * ATTN_STRIP:(h + 1) * ATTN_STRIP, :] = a.T.astype(o_ref.dtype)

    emit_output()

    acc_all = acc_s[...]
    bad = jnp.where(jnp.abs(acc_all) < jnp.inf, 0.0, 1.0)
    bad = jnp.maximum(bad, jnp.where(acc_all[:, V_HEAD_DIM:V_HEAD_DIM + 1] > 0.0, 0.0, 1.0))
    bad = jnp.max(jnp.max(bad, axis=0), axis=0, keepdims=True)
    bad = jnp.max(bad, axis=1, keepdims=True)[0, 0]

    @pl.when(bad > 0.0)
    def _():
        r_s[...] = jnp.full(r_s.shape, MASK_VALUE, F32)
        acc_s[...] = jnp.zeros(acc_s.shape, F32)
        tile_rows = bq + bk

        @pl.loop(0, n_far + n_near)
        def _(j):
            start = (j - n_far + jnp.where(qi == 0, 1, 0)) * bk
            in_tile = jnp.logical_and(j >= n_far, start + bk <= tile_rows)
            start = pl.multiple_of(jnp.clip(start, 0, tile_rows - bk), bk)
            bias = jnp.where(j < n_far, 0.0, jnp.where(in_tile, bias_ref[0, pl.ds(start, bk), :], MASK_VALUE))
            k = keys(j)
            vt1 = values(j)
            for n in range(len(strips)):
                s = logits(k, n, bias)
                m_old = r_s[n]
                m_new = jnp.maximum(m_old, jnp.max(s, axis=0, keepdims=True))
                p = jnp.exp2(s - m_new).astype(BF16)
                acc_s[n] = jnp.exp2(m_old - m_new) * acc_s[n] + jnp.dot(vt1, p, preferred_element_type=F32)
                r_s[n] = m_new

        emit_output()


def _attn_prompt(q, k, vt, bias, lam_vecs, sub_g_col, *, bq, bk, lam_init):
    b, s, _ = q.shape
    n_strips = 2 * (bq // ATTN_STRIP)
    assert bq % bk == 0 and bk >= MAX_DISTANCE and s >= bq + bk
    return pl.pallas_call(
        functools.partial(_attn_prompt_kernel, bq=bq, bk=bk, lam_init=lam_init),
        grid=(b, N_HEADS, s // bq),
        in_specs=[
            pl.BlockSpec((1, bq, LANES), lambda bi, h, qi: (bi, qi, h)),
            pl.BlockSpec((1, s, LANES), lambda bi, h, qi: (bi, 0, h)),
            pl.BlockSpec((1, V_HEAD_DIM, s), lambda bi, h, qi: (bi, h, 0)),
            pl.BlockSpec((1, bq + bk, bq), lambda bi, h, qi: (h, 0, 0)),
            pl.BlockSpec((4, HEAD_DIM), lambda bi, h, qi: (0, 0)),
            pl.BlockSpec((V_HEAD_DIM, 1), lambda bi, h, qi: (0, 0)),
        ],
        out_specs=pl.BlockSpec((1, bq, LANES), lambda bi, h, qi: (bi, qi, h)),
        out_shape=jax.ShapeDtypeStruct((b, s, N_HEADS * V_HEAD_DIM), BF16),
        scratch_shapes=[
            pltpu.VMEM((2, n_strips, bk, ATTN_STRIP), BF16),
            pltpu.VMEM((2, n_strips, 1, ATTN_STRIP), F32),
            pltpu.VMEM((n_strips, 1, ATTN_STRIP), F32),
            pltpu.VMEM((n_strips, V_HEAD_DIM + ONES_ROWS, ATTN_STRIP), F32),
        ],
        compiler_params=_params(("arbitrary", "arbitrary", "arbitrary")),
        name="attn_prompt",
    )(q, k, vt, bias, lam_vecs, sub_g_col)


def _attn_sample_kernel(q_ref, ck_ref, cv_ref, nk_ref, nv_ref, bias_ref, lam_ref, subg_ref, o_ref,
                        *, past, lam_init):
    t = q_ref.shape[1]
    lam = _diff_lambda(lam_ref, lam_init)
    for hd in range(N_HEADS):
        cs = slice(hd * LANES, (hd + 1) * LANES)
        qs = _split_q(q_ref[0, :, cs])
        bias = bias_ref[hd]
        bias2 = jnp.concatenate([bias, bias], axis=0)
        carry = (jnp.full((2 * t, 1), MASK_VALUE, F32), jnp.zeros((2 * t, 1), F32),
                 jnp.zeros((2 * t, V_HEAD_DIM), F32))
        kc = ck_ref[0, pl.ds(hd, past, stride=N_HEADS), :].astype(BF16)
        vc = cv_ref[0, pl.ds(hd, past, stride=N_HEADS), :].astype(BF16)
        carry = _softmax_step(_qk(qs, kc) + bias2[:, :past], vc, *carry)
        m, l, acc = _softmax_step(_qk(qs, nk_ref[0, :, cs]) + bias2[:, past:], nv_ref[0, :, cs], *carry)
        o_ref[0, :, cs] = _finish_heads(l, acc, lam, subg_ref[...], lam_init, t).astype(o_ref.dtype)


def _attn_sample(q, cache_k, cache_v, new_k, new_v, bias, lam_vecs, sub_g, *, lam_init):
    b, t, width = q.shape
    past = cache_k.shape[1] // N_HEADS
    flat_spec = pl.BlockSpec((1, t, width), lambda bi: (bi, 0, 0))
    cache_spec = pl.BlockSpec((1, past * N_HEADS, LANES), lambda bi: (bi, 0, 0))
    return pl.pallas_call(
        functools.partial(_attn_sample_kernel, past=past, lam_init=lam_init),
        grid=(b,),
        in_specs=[
            flat_spec, cache_spec, cache_spec, flat_spec, flat_spec,
            pl.BlockSpec((N_HEADS, t, past + t), lambda bi: (0, 0, 0)),
            pl.BlockSpec((4, HEAD_DIM), lambda bi: (0, 0)),
            pl.BlockSpec((1, V_HEAD_DIM), lambda bi: (0, 0)),
        ],
        out_specs=flat_spec,
        out_shape=jax.ShapeDtypeStruct((b, t, width), BF16),
        compiler_params=_params(("arbitrary",)),
        name="attn_sample",
    )(q, cache_k, cache_v, new_k, new_v, bias, lam_vecs, sub_g)


def _mixer_out_kernel(x_ref, m_ref, a_ref, gt_ref, w_ref, o_ref):
    half = m_ref.shape[-1]
    y = jnp.dot(m_ref[...], w_ref[:half, :], preferred_element_type=F32)
    y = y + jnp.dot(a_ref[...], w_ref[half:, :], preferred_element_type=F32)
    nb, r, d = x_ref.shape
    o_ref[...] = x_ref[...] + gt_ref[...] * y.reshape(nb, r, d)


def _mixer_out(x, m, a, gate, w_out_b, *, nb_blk, r_blk):
    nbat, r, d = x.shape
    nr = r // r_blk
    rows = nb_blk * r_blk
    half = m.shape[-1]
    return pl.pallas_call(
        _mixer_out_kernel,
        grid=((nbat // nb_blk) * nr,),
        in_specs=[
            pl.BlockSpec((nb_blk, r_blk, d), lambda i: (i // nr, i % nr, 0)),
            pl.BlockSpec((rows, half), lambda i: (i, 0)),
            pl.BlockSpec((rows, half), lambda i: (i, 0)),
            pl.BlockSpec((nb_blk, 1, d), lambda i: (i // nr, 0, 0)),
            pl.BlockSpec((d, d), lambda i: (0, 0)),
        ],
        out_specs=pl.BlockSpec((nb_blk, r_blk, d), lambda i: (i // nr, i % nr, 0)),
        out_shape=jax.ShapeDtypeStruct(x.shape, F32),
        compiler_params=_params(("arbitrary",)),
        name="mixer_out",
    )(x, m, a, gate[:, None, :], w_out_b)


def _ffn_kernel(x_ref, sh_ref, sc_ref, gt_ref, g_ref, wg_ref, wu_ref, wo_ref, gf_ref, shf_ref, scf_ref,
                o_ref, h_s, acc_s):
    f = pl.program_id(1)
    rows = h_s.shape[0]

    @pl.when(f == 0)
    def _():
        x = x_ref[...]
        y = x * lax.rsqrt(jnp.mean(x * x, axis=-1, keepdims=True) + EPS)
        hm = (y * g_ref[...]) * (1.0 + sc_ref[...]) + sh_ref[...]
        h_s[...] = hm.reshape(rows, hm.shape[-1]).astype(BF16)
        acc_s[...] = jnp.zeros_like(acc_s)

    h = h_s[...]
    zg = jnp.dot(h, wg_ref[...], preferred_element_type=F32)
    zu = jnp.dot(h, wu_ref[...], preferred_element_type=F32)
    act = (zg * jax.nn.sigmoid(zg) * zu).astype(BF16)
    acc_s[...] += jnp.dot(act, wo_ref[...], preferred_element_type=F32)

    @pl.when(f == pl.num_programs(1) - 1)
    def _():
        nb, r, d = x_ref.shape
        x2 = x_ref[...] + gt_ref[...] * acc_s[...].reshape(nb, r, d)
        y = x2 * lax.rsqrt(jnp.mean(x2 * x2, axis=-1, keepdims=True) + EPS)
        o_ref[...] = (y * gf_ref[...]) * (1.0 + scf_ref[...]) + shf_ref[...]


def _ffn(x, shift, scale, gate, g_ffn, w_in_b, w_out_b, g_final, shift_f, scale_f, *, nb_blk, r_blk, tf):
    nbat, r, d = x.shape
    d_ff = w_out_b.shape[0]
    nf = d_ff // tf
    nr = r // r_blk
    rows = nb_blk * r_blk
    x_spec = pl.BlockSpec((nb_blk, r_blk, d), lambda i, f: (i // nr, i % nr, 0))
    vec_spec = pl.BlockSpec((nb_blk, 1, d), lambda i, f: (i // nr, 0, 0))
    par_spec = pl.BlockSpec((1, 1, d), lambda i, f: (0, 0, 0))
    return pl.pallas_call(
        _ffn_kernel,
        grid=((nbat // nb_blk) * nr, nf),
        in_specs=[
            x_spec, vec_spec, vec_spec, vec_spec, par_spec,
            pl.BlockSpec((d, tf), lambda i, f: (0, f)),
            pl.BlockSpec((d, tf), lambda i, f: (0, f + nf)),
            pl.BlockSpec((tf, d), lambda i, f: (f, 0)),
            par_spec, vec_spec, vec_spec,
        ],
        out_specs=x_spec,
        out_shape=jax.ShapeDtypeStruct(x.shape, F32),
        scratch_shapes=[pltpu.VMEM((rows, d), BF16), pltpu.VMEM((rows, d), F32)],
        compiler_params=_params(("arbitrary", "arbitrary")),
        name="ffn",
    )(x, shift[:, None, :], scale[:, None, :], gate[:, None, :], g_ffn.reshape(1, 1, d),
      w_in_b, w_in_b, w_out_b, g_final.reshape(1, 1, d), shift_f[:, None, :], scale_f[:, None, :])


ATTN_Q_BLOCK = 1024
ATTN_K_BEAT = 512
PROMPT_ROWS = 512
SAMPLE_BATCH_BLOCK = 8
FFN_TILE = 512


def kernel(x_prompt, x_sample, cache_k, cache_v, c_prompt, c_sample, rel_bias, w_ada, b_ada, w_ada_final,
           b_ada_final, g_mix, g_ffn, g_final, w_in, mlp_ln_g, mlp_ln_b, w_s, b_s, lambda_q1, lambda_k1,
           lambda_q2, lambda_k2, sub_g, w_out, w_ffn_in, w_ffn_out):
    B, S, D = x_prompt.shape
    DB, T, _ = x_sample.shape
    depth = w_in.shape[0]
    past = cache_k.shape[2]
    width = N_HEADS * V_HEAD_DIM
    mlp_chunk = w_s.shape[-1]

    c_all = jnp.concatenate([c_prompt, c_sample], axis=0)
    mod_f = _adaln(c_all, w_ada_final, b_ada_final)
    bias_p = _rel_bias_tiles(rel_bias, ATTN_Q_BLOCK, ATTN_Q_BLOCK + ATTN_K_BEAT, ATTN_K_BEAT, 0, True, True)
    bias_s = _rel_bias_tiles(rel_bias, T, past + T, past, 0, False, False)

    assert depth == 1, "the final adaLN norm is fused into the single layer's FFN kernel"
    lam_init = 0.8 - 0.6 * math.exp(-0.3 * 0)
    lam_vecs = jnp.stack([lambda_q1[0], lambda_k1[0], lambda_q2[0], lambda_k2[0]])
    subg = sub_g.reshape(1, V_HEAD_DIM)
    mod = _adaln(c_all, w_ada[0], b_ada[0])
    sh1, sc1, gt1, sh2, sc2, gt2 = jnp.split(mod, 6, axis=-1)
    shf, scf = jnp.split(mod_f, 2, axis=-1)
    w_in_b = w_in[0].astype(BF16)
    w_out_b = w_out[0].astype(BF16)
    w_f_in_b = w_ffn_in[0].astype(BF16)
    w_f_out_b = w_ffn_out[0].astype(BF16)
    mixer_w = (g_mix[0], w_in_b, mlp_ln_g[0], mlp_ln_b[0], w_s[0], b_s[0])

    m, q, kp, kb, vp, vt = _mixer_in(x_prompt, sh1[:B], sc1[:B], *mixer_w, nb_blk=1, r_blk=PROMPT_ROWS,
                                     t_chunk=mlp_chunk, emit_gv=False, v_transposed=True)
    a = _attn_prompt(q.reshape(B, S, width), kb.reshape(B, S, width), vt, bias_p, lam_vecs,
                     subg.reshape(V_HEAD_DIM, 1), bq=ATTN_Q_BLOCK, bk=ATTN_K_BEAT, lam_init=lam_init)
    xp = _mixer_out(x_prompt, m, a.reshape(B * S, width), gt1[:B], w_out_b, nb_blk=1, r_blk=PROMPT_ROWS)
    yp = _ffn(xp, sh2[:B], sc2[:B], gt2[:B], g_ffn[0], w_f_in_b, w_f_out_b, g_final, shf[:B], scf[:B],
              nb_blk=1, r_blk=PROMPT_ROWS, tf=FFN_TILE)

    m, q, ks, kb, vs, vb, gvs = _mixer_in(x_sample, sh1[B:], sc1[B:], *mixer_w, nb_blk=SAMPLE_BATCH_BLOCK,
                                          r_blk=T, t_chunk=T, emit_gv=True, v_transposed=False)
    a = _attn_sample(q.reshape(DB, T, width), cache_k.reshape(DB, past * N_HEADS, LANES),
                     cache_v.reshape(DB, past * N_HEADS, LANES), kb.reshape(DB, T, width),
                     vb.reshape(DB, T, width), bias_s, lam_vecs, subg, lam_init=lam_init)
    xs = _mixer_out(x_sample, m, a.reshape(DB * T, width), gt1[B:], w_out_b, nb_blk=SAMPLE_BATCH_BLOCK, r_blk=T)
    ys = _ffn(xs, sh2[B:], sc2[B:], gt2[B:], g_ffn[0], w_f_in_b, w_f_out_b, g_final, shf[B:], scf[B:],
              nb_blk=SAMPLE_BATCH_BLOCK, r_blk=T, tf=FFN_TILE)

    head_shape = (N_HEADS, V_HEAD_DIM)
    return (yp, ys, kp.reshape(1, B, S, *head_shape), vp.reshape(1, B, S, *head_shape),
            ks.reshape(1, DB, T, *head_shape), vs.reshape(1, DB, T, *head_shape),
            gvs.reshape(1, DB, T, MLP_GROUPS, MLP_GROUP_DIM))
```

```python
import functools
import math

import jax
import jax.numpy as jnp
from jax import lax
from jax.experimental import pallas as pl
from jax.experimental.pallas import tpu as pltpu

LANES = 128
SUBLANES = 8
VMEM_LIMIT_BYTES = 56 * 1024 * 1024
MXU_WIDTH = 256

ATTN_STRIP = MXU_WIDTH
ONES_ROWS = SUBLANES
R_INIT_KEYS = 128

CHUNK = 64
N_HEADS = 8
HEAD_DIM = 64
V_HEAD_DIM = 128
MLP_GROUPS = 8
MLP_GROUP_DIM = 128
N_BUCKETS = 32
MAX_DISTANCE = 128
EPS = 1e-6
MASK_VALUE = -1e30
LOG2E = math.log2(math.e)

BF16 = jnp.bfloat16
F32 = jnp.float32


def _params(sem):
    return pltpu.CompilerParams(dimension_semantics=sem, vmem_limit_bytes=VMEM_LIMIT_BYTES)


def _adaln_kernel(c_ref, w_ref, b_ref, o_ref):
    c = c_ref[...]
    a = c * jax.nn.sigmoid(c)
    o_ref[...] = jnp.dot(a, w_ref[...], preferred_element_type=F32) + b_ref[...]


def _adaln(c, w, b, tn=1024):
    rows, d = c.shape
    n = w.shape[1]
    return pl.pallas_call(
        _adaln_kernel,
        grid=(n // tn,),
        in_specs=[
            pl.BlockSpec((rows, d), lambda j: (0, 0)),
            pl.BlockSpec((d, tn), lambda j: (0, j)),
            pl.BlockSpec((1, tn), lambda j: (0, j)),
        ],
        out_specs=pl.BlockSpec((rows, tn), lambda j: (0, j)),
        out_shape=jax.ShapeDtypeStruct((rows, n), F32),
        compiler_params=_params(("arbitrary",)),
        name="adaln",
    )(c, w, b.reshape(1, n))


def _rel_bias_kernel(tab_ref, o_ref, *, nq, nk, q_start, k_start, shift_far, keys_on_rows):
    h = pl.program_id(0)
    nb = N_BUCKETS // 2
    max_exact = nb // 2
    q_axis, k_axis = (1, 0) if keys_on_rows else (0, 1)
    rows, cols = (nk, nq) if keys_on_rows else (nq, nk)
    shift = tab_ref[nb - 1, h] if shift_far else 0.0

    def block(shape, q0, k0):
        q_pos = q0 + lax.broadcasted_iota(jnp.int32, shape, q_axis)
        k_pos = k0 + lax.broadcasted_iota(jnp.int32, shape, k_axis)
        rel = k_pos - q_pos
        ret = jnp.where(rel > 0, nb, 0)
        n = jnp.abs(rel)
        nf = jnp.maximum(n, 1).astype(F32)
        large = max_exact + (jnp.log(nf / max_exact) / math.log(MAX_DISTANCE / max_exact)
                             * (nb - max_exact)).astype(jnp.int32)
        large = jnp.minimum(large, nb - 1)
        bucket = ret + jnp.where(n < max_exact, n, large)
        bias = jnp.zeros(shape, F32)
        for bkt in range(N_BUCKETS):
            bias = jnp.where(bucket == bkt, tab_ref[bkt, h], bias)
        allowed = (k_pos // CHUNK) <= (q_pos // CHUNK)
        return jnp.where(allowed, (bias - shift) * LOG2E, MASK_VALUE)

    if rows % LANES or cols % LANES:
        o_ref[0] = block((rows, cols), q_start, k_start)
        return
    far_value = (tab_ref[nb - 1, h] - shift) * LOG2E
    for rb in range(rows // LANES):
        for cb in range(cols // LANES):
            k0 = k_start + LANES * (rb if keys_on_rows else cb)
            q0 = q_start + LANES * (cb if keys_on_rows else rb)
            sl = (0, slice(rb * LANES, (rb + 1) * LANES), slice(cb * LANES, (cb + 1) * LANES))
            if (k0 - q0) + (LANES - 1) <= -MAX_DISTANCE:
                o_ref[sl] = jnp.full((LANES, LANES), far_value, F32)
            elif (k0 - q0) - (LANES - 1) >= CHUNK:
                o_ref[sl] = jnp.full((LANES, LANES), MASK_VALUE, F32)
            else:
                o_ref[sl] = block((LANES, LANES), q0, k0)


def _rel_bias_tiles(rel_bias, nq, nk, q_start, k_start, shift_far, keys_on_rows):
    out_tile = (nk, nq) if keys_on_rows else (nq, nk)
    return pl.pallas_call(
        functools.partial(_rel_bias_kernel, nq=nq, nk=nk, q_start=q_start, k_start=k_start,
                          shift_far=shift_far, keys_on_rows=keys_on_rows),
        grid=(N_HEADS,),
        in_specs=[pl.BlockSpec(memory_space=pltpu.SMEM)],
        out_specs=pl.BlockSpec((1,) + out_tile, lambda h: (h, 0, 0)),
        out_shape=jax.ShapeDtypeStruct((N_HEADS,) + out_tile, F32),
        compiler_params=_params(("arbitrary",)),
        name="rel_bias",
    )(rel_bias)


def _mixer_in_kernel(x_ref, sh_ref, sc_ref, g_ref, w_ref, lng_ref, lnb_ref, ws_ref, bs_ref,
                     *refs, t_chunk, emit_gv, v_transposed):
    if emit_gv:
        m_ref, q_ref, k_ref, kb_ref, v_ref, vb_ref, gv_ref, u_s = refs
    else:
        m_ref, q_ref, k_ref, kb_ref, v_ref, vb_ref, u_s = refs
        gv_ref = None
    rows = u_s.shape[0]
    width = u_s.shape[1]

    def store_per_head(ref, val):
        for hd in range(N_HEADS):
            ref[pl.ds(hd, rows, stride=N_HEADS), :] = val[:, hd * LANES:(hd + 1) * LANES]

    x = x_ref[...]
    y = x * lax.rsqrt(jnp.mean(x * x, axis=-1, keepdims=True) + EPS)
    y = y * g_ref[...]
    hm = y * (1.0 + sc_ref[...]) + sh_ref[...]
    h = hm.reshape(rows, hm.shape[-1]).astype(BF16)

    def project(seg):
        return jnp.dot(h, w_ref[:, seg * width:(seg + 1) * width], preferred_element_type=F32)

    u_s[...] = jax.nn.gelu(project(0))

    g = jax.nn.gelu(project(1))
    mu = jnp.mean(g, axis=-1, keepdims=True)
    var = jnp.mean(jnp.square(g - mu), axis=-1, keepdims=True)
    gv = (g - mu) * lax.rsqrt(var + EPS) * lng_ref[...] + lnb_ref[...]
    if emit_gv:
        store_per_head(gv_ref, gv)
    gvb = gv.astype(BF16)
    ii = lax.broadcasted_iota(jnp.int32, (t_chunk, t_chunk), 0)
    jj = lax.broadcasted_iota(jnp.int32, (t_chunk, t_chunk), 1)
    mask = (jj // CHUNK) <= (ii // CHUNK)
    for grp in range(MLP_GROUPS):
        wg = jnp.where(mask, ws_ref[grp], 0.0).astype(BF16)
        bg = bs_ref[grp]
        cs = slice(grp * MLP_GROUP_DIM, (grp + 1) * MLP_GROUP_DIM)
        for c in range(rows // t_chunk):
            rs = slice(c * t_chunk, (c + 1) * t_chunk)
            mixed = jnp.dot(wg, gvb[rs, cs], preferred_element_type=F32) + bg
            m_ref[rs, cs] = (u_s[rs, cs] * mixed).astype(BF16)

    q_ref[...] = (project(2) * (HEAD_DIM ** -0.5 * LOG2E)).astype(BF16)

    z = project(3)
    store_per_head(k_ref, z)
    kb_ref[...] = z.astype(BF16)

    z = project(4)
    store_per_head(v_ref, z)
    if v_transposed:
        vb_ref[0] = z.T.astype(BF16)
    else:
        vb_ref[...] = z.astype(BF16)


def _mixer_in(x, shift, scale, g_mix, w_in_b, ln_g, ln_b, w_s, b_s, *, nb_blk, r_blk, t_chunk, emit_gv,
              v_transposed):
    nbat, r, d = x.shape
    width = 1024
    nr = r // r_blk
    n_tiles = (nbat // nb_blk) * nr
    rows = nb_blk * r_blk
    tokens = nbat * r
    row_idx = lambda i: (i, 0)
    flat = jax.ShapeDtypeStruct((tokens, width), BF16)
    per_head = jax.ShapeDtypeStruct((tokens * N_HEADS, LANES), F32)
    flat_spec = pl.BlockSpec((rows, width), row_idx)
    per_head_spec = pl.BlockSpec((rows * N_HEADS, LANES), row_idx)
    out_shape = [flat, flat, per_head, flat, per_head, flat]
    out_specs = [flat_spec, flat_spec, per_head_spec, flat_spec, per_head_spec, flat_spec]
    if v_transposed:
        assert nb_blk == 1
        out_shape[5] = jax.ShapeDtypeStruct((nbat, width, r), BF16)
        out_specs[5] = pl.BlockSpec((1, width, r_blk), lambda i: (i // nr, 0, i % nr))
    if emit_gv:
        out_shape.append(per_head)
        out_specs.append(per_head_spec)
    ws_t = w_s[:, :t_chunk, :t_chunk]
    bs_t = b_s[:, :t_chunk, None]
    once = pl.Buffered(1)
    return pl.pallas_call(
        functools.partial(_mixer_in_kernel, t_chunk=t_chunk, emit_gv=emit_gv, v_transposed=v_transposed),
        grid=(n_tiles,),
        in_specs=[
            pl.BlockSpec((nb_blk, r_blk, d), lambda i: (i // nr, i % nr, 0)),
            pl.BlockSpec((nb_blk, 1, d), lambda i: (i // nr, 0, 0)),
            pl.BlockSpec((nb_blk, 1, d), lambda i: (i // nr, 0, 0)),
            pl.BlockSpec((1, 1, d), lambda i: (0, 0, 0)),
            pl.BlockSpec(w_in_b.shape, lambda i: (0, 0), pipeline_mode=once),
            pl.BlockSpec((1, width), lambda i: (0, 0)),
            pl.BlockSpec((1, width), lambda i: (0, 0)),
            pl.BlockSpec((MLP_GROUPS, t_chunk, t_chunk), lambda i: (0, 0, 0)),
            pl.BlockSpec((MLP_GROUPS, t_chunk, 1), lambda i: (0, 0, 0)),
        ],
        out_specs=out_specs,
        out_shape=out_shape,
        scratch_shapes=[pltpu.VMEM((rows, width), F32)],
        compiler_params=_params(("arbitrary",)),
        name="mixer_in",
    )(x, shift[:, None, :], scale[:, None, :], g_mix.reshape(1, 1, d), w_in_b,
      ln_g.reshape(1, width), ln_b.reshape(1, width), ws_t, bs_t)


def _split_q(q):
    lane = lax.broadcasted_iota(jnp.int32, q.shape, 1)
    zero = jnp.zeros_like(q)
    return jnp.concatenate([jnp.where(lane < HEAD_DIM, q, zero), jnp.where(lane >= HEAD_DIM, q, zero)], axis=0)


def _softmax_step(s, v, m, l, acc):
    m_new = jnp.maximum(m, jnp.max(s, axis=-1, keepdims=True))
    alpha = jnp.exp2(m - m_new)
    p = jnp.exp2(s - m_new)
    l_new = alpha * l + jnp.sum(p, axis=-1, keepdims=True)
    acc_new = alpha * acc + jnp.dot(p.astype(BF16), v, preferred_element_type=F32)
    return m_new, l_new, acc_new


def _diff_lambda(lam_ref, lam_init):
    lv = lam_ref[...]
    s1 = jnp.sum(lv[0:1] * lv[1:2], axis=-1, keepdims=True)
    s2 = jnp.sum(lv[2:3] * lv[3:4], axis=-1, keepdims=True)
    return jnp.exp(s1) - jnp.exp(s2) + lam_init


def _finish_heads(l, acc, lam, subg, lam_init, n):
    o = acc / l
    a = o[:n] - lam * o[n:]
    a = a * lax.rsqrt(jnp.mean(a * a, axis=-1, keepdims=True) + EPS)
    return a * subg * (1.0 - lam_init)


def _qk(qs, k):
    return lax.dot_general(qs, k, (((1,), (1,)), ((), ())), preferred_element_type=F32)


def _attn_prompt_kernel(q_ref, k_ref, vt_ref, bias_ref, lam_ref, subg_ref, o_ref,
                        p_s, alpha_s, r_s, acc_s, *, bq, bk, lam_init):
    qi = pl.program_id(2)
    halves = bq // ATTN_STRIP
    strips = [(c, h) for c in range(2) for h in range(halves)]
    n_near = bq // bk + 1
    q = q_ref[0]
    lane = lax.broadcasted_iota(jnp.int32, q.shape, 1)
    zero = jnp.zeros_like(q)
    qs = (jnp.where(lane < HEAD_DIM, q, zero), jnp.where(lane >= HEAD_DIM, q, zero))

    ones = jnp.ones((ONES_ROWS, bk), BF16)

    def keys(j):
        return k_ref[0, pl.ds(pl.multiple_of(j * bk, bk), bk), :]

    def values(j):
        vt = vt_ref[0, :, pl.ds(pl.multiple_of(j * bk, bk), bk)]
        return jnp.concatenate([vt, ones], axis=0)

    def logits(k, n, bias):
        c, h = strips[n]
        cols = slice(h * ATTN_STRIP, (h + 1) * ATTN_STRIP)
        s = lax.dot_general(k, qs[c][cols], (((1,), (1,)), ((), ())), preferred_element_type=F32)
        if bias is not None:
            s = s + (bias if bias.ndim == 0 else bias[:, cols])
        return s

    def beat(new=None, old=None):
        if new is not None:
            k = keys(new[0])
        if old is not None:
            vt1 = values(old[0])
        for n in range(len(strips)):
            if new is not None and n not in new[3]:
                _, slot, bias, _ = new
                s = logits(k, n, bias)
                r = r_s[n]
                p_s[slot, n] = jnp.exp2(s - r).astype(BF16)
                r_new = jnp.maximum(r, jnp.max(s, axis=0, keepdims=True))
                alpha_s[slot, n] = jnp.exp2(r - r_new)
                r_s[n] = r_new
            if old is not None and n not in old[2]:
                slot = old[1]
                pv = jnp.dot(vt1, p_s[slot, n], preferred_element_type=F32)
                acc_s[n] = (acc_s[n] + pv) * alpha_s[slot, n]

    n_far = jnp.maximum((qi * bq - MAX_DISTANCE) // bk, 0)
    near_bias = []
    for i in range(n_near):
        shifted = bias_ref[0, (i + 1) * bk:(i + 2) * bk, :] if i + 1 < n_near else MASK_VALUE
        near_bias.append(jnp.where(qi == 0, shifted, bias_ref[0, i * bk:(i + 1) * bk, :]))

    k0 = k_ref[0, :R_INIT_KEYS, :]
    bias0 = jnp.where(qi == 0, near_bias[0][:R_INIT_KEYS], 0.0)
    for n in range(len(strips)):
        r_s[n] = jnp.max(logits(k0, n, bias0), axis=0, keepdims=True)
    acc_s[...] = jnp.zeros(acc_s.shape, F32)
    alpha_s[1] = jnp.ones(alpha_s.shape[1:], F32)
    p_s[1] = jnp.zeros(p_s.shape[1:], BF16)

    n_pairs = n_far // 2

    def pair(j):
        beat(new=(j, 0, None, ()), old=(jnp.maximum(j - 1, 0), 1, ()))
        beat(new=(j + 1, 1, None, ()), old=(j, 0, ()))

    n_quads = n_far // 4

    @pl.loop(0, n_quads)
    def _(t):
        pair(4 * t)
        pair(4 * t + 2)

    @pl.loop(2 * n_quads, n_pairs)
    def _(t):
        pair(2 * t)

    x = 2 * n_pairs
    tail = [(x, jnp.where(n_far % 2 == 1, 0.0, MASK_VALUE), ())]
    for i, bias_i in enumerate(near_bias):
        skip = tuple(n for n, (_, h) in enumerate(strips) if (i - 1) * bk >= (h + 1) * ATTN_STRIP)
        tail.append((n_far + i, bias_i, skip))
    for t, (j, bias_t, skip) in enumerate(tail):
        prev = (jnp.maximum(x - 1, 0), 1, ()) if t == 0 else (tail[t - 1][0], (t - 1) % 2, tail[t - 1][2])
        beat(new=(j, t % 2, bias_t, skip), old=prev)
    beat(old=(tail[-1][0], (len(tail) - 1) % 2, tail[-1][2]))

    lam = _diff_lambda(lam_ref, lam_init)

    def emit_output():
        for h in range(halves):
            o = []
            for c in range(2):
                acc = acc_s[c * halves + h]
                o.append(acc[:V_HEAD_DIM] / acc[V_HEAD_DIM:V_HEAD_DIM + 1])
            a = o[0] - lam * o[1]
            a = a * lax.rsqrt(jnp.mean(a * a, axis=0, keepdims=True) + EPS)
            a = a * subg_ref[...] * (1.0 - lam_init)
            o_ref[0, h * ATTN_STRIP:(h + 1) * ATTN_STRIP, :] = a.T.astype(o_ref.dtype)

    emit_output()

    acc_all = acc_s[...]
    bad = jnp.where(jnp.abs(acc_all) < jnp.inf, 0.0, 1.0)
    bad = jnp.maximum(bad, jnp.where(acc_all[:, V_HEAD_DIM:V_HEAD_DIM + 1] > 0.0, 0.0, 1.0))
    bad = jnp.max(jnp.max(bad, axis=0), axis=0, keepdims=True)
    bad = jnp.max(bad, axis=1, keepdims=True)[0, 0]

    @pl.when(bad > 0.0)
    def _():
        r_s[...] = jnp.full(r_s.shape, MASK_VALUE, F32)
        acc_s[...] = jnp.zeros(acc_s.shape, F32)
        tile_rows = bq + bk

        @pl.loop(0, n_far + n_near)
        def _(j):
            start = (j - n_far + jnp.where(qi == 0, 1, 0)) * bk
            in_tile = jnp.logical_and(j >= n_far, start + bk <= tile_rows)
            start = pl.multiple_of(jnp.clip(start, 0, tile_rows - bk), bk)
            bias = jnp.where(j < n_far, 0.0, jnp.where(in_tile, bias_ref[0, pl.ds(start, bk), :], MASK_VALUE))
            k = keys(j)
            vt1 = values(j)
            for n in range(len(strips)):
                s = logits(k, n, bias)
                m_old = r_s[n]
                m_new = jnp.maximum(m_old, jnp.max(s, axis=0, keepdims=True))
                p = jnp.exp2(s - m_new).astype(BF16)
                acc_s[n] = jnp.exp2(m_old - m_new) * acc_s[n] + jnp.dot(vt1, p, preferred_element_type=F32)
                r_s[n] = m_new

        emit_output()


def _attn_prompt(q, k, vt, bias, lam_vecs, sub_g_col, *, bq, bk, lam_init):
    b, s, _ = q.shape
    n_strips = 2 * (bq // ATTN_STRIP)
    assert bq % bk == 0 and bk >= MAX_DISTANCE and s >= bq + bk
    return pl.pallas_call(
        functools.partial(_attn_prompt_kernel, bq=bq, bk=bk, lam_init=lam_init),
        grid=(b, N_HEADS, s // bq),
        in_specs=[
            pl.BlockSpec((1, bq, LANES), lambda bi, h, qi: (bi, qi, h)),
            pl.BlockSpec((1, s, LANES), lambda bi, h, qi: (bi, 0, h)),
            pl.BlockSpec((1, V_HEAD_DIM, s), lambda bi, h, qi: (bi, h, 0)),
            pl.BlockSpec((1, bq + bk, bq), lambda bi, h, qi: (h, 0, 0)),
            pl.BlockSpec((4, HEAD_DIM), lambda bi, h, qi: (0, 0)),
            pl.BlockSpec((V_HEAD_DIM, 1), lambda bi, h, qi: (0, 0)),
        ],
        out_specs=pl.BlockSpec((1, bq, LANES), lambda bi, h, qi: (bi, qi, h)),
        out_shape=jax.ShapeDtypeStruct((b, s, N_HEADS * V_HEAD_DIM), BF16),
        scratch_shapes=[
            pltpu.VMEM((2, n_strips, bk, ATTN_STRIP), BF16),
            pltpu.VMEM((2, n_strips, 1, ATTN_STRIP), F32),
            pltpu.VMEM((n_strips, 1, ATTN_STRIP), F32),
            pltpu.VMEM((n_strips, V_HEAD_DIM + ONES_ROWS, ATTN_STRIP), F32),
        ],
        compiler_params=_params(("arbitrary", "arbitrary", "arbitrary")),
        name="attn_prompt",
    )(q, k, vt, bias, lam_vecs, sub_g_col)


def _attn_sample_kernel(q_ref, ck_ref, cv_ref, nk_ref, nv_ref, bias_ref, lam_ref, subg_ref, o_ref,
                        *, past, lam_init):
    t = q_ref.shape[1]
    lam = _diff_lambda(lam_ref, lam_init)
    for hd in range(N_HEADS):
        cs = slice(hd * LANES, (hd + 1) * LANES)
        qs = _split_q(q_ref[0, :, cs])
        bias = bias_ref[hd]
        bias2 = jnp.concatenate([bias, bias], axis=0)
        carry = (jnp.full((2 * t, 1), MASK_VALUE, F32), jnp.zeros((2 * t, 1), F32),
                 jnp.zeros((2 * t, V_HEAD_DIM), F32))
        kc = ck_ref[0, pl.ds(hd, past, stride=N_HEADS), :].astype(BF16)
        vc = cv_ref[0, pl.ds(hd, past, stride=N_HEADS), :].astype(BF16)
        carry = _softmax_step(_qk(qs, kc) + bias2[:, :past], vc, *carry)
        m, l, acc = _softmax_step(_qk(qs, nk_ref[0, :, cs]) + bias2[:, past:], nv_ref[0, :, cs], *carry)
        o_ref[0, :, cs] = _finish_heads(l, acc, lam, subg_ref[...], lam_init, t).astype(o_ref.dtype)


def _attn_sample(q, cache_k, cache_v, new_k, new_v, bias, lam_vecs, sub_g, *, lam_init):
    b, t, width = q.shape
    past = cache_k.shape[1] // N_HEADS
    flat_spec = pl.BlockSpec((1, t, width), lambda bi: (bi, 0, 0))
    cache_spec = pl.BlockSpec((1, past * N_HEADS, LANES), lambda bi: (bi, 0, 0))
    return pl.pallas_call(
        functools.partial(_attn_sample_kernel, past=past, lam_init=lam_init),
        grid=(b,),
        in_specs=[
            flat_spec, cache_spec, cache_spec, flat_spec, flat_spec,
            pl.BlockSpec((N_HEADS, t, past + t), lambda bi: (0, 0, 0)),
            pl.BlockSpec((4, HEAD_DIM), lambda bi: (0, 0)),
            pl.BlockSpec((1, V_HEAD_DIM), lambda bi: (0, 0)),
        ],
        out_specs=flat_spec,
        out_shape=jax.ShapeDtypeStruct((b, t, width), BF16),
        compiler_params=_params(("arbitrary",)),
        name="attn_sample",
    )(q, cache_k, cache_v, new_k, new_v, bias, lam_vecs, sub_g)


def _mixer_out_kernel(x_ref, m_ref, a_ref, gt_ref, w_ref, o_ref):
    half = m_ref.shape[-1]
    y = jnp.dot(m_ref[...], w_ref[:half, :], preferred_element_type=F32)
    y = y + jnp.dot(a_ref[...], w_ref[half:, :], preferred_element_type=F32)
    nb, r, d = x_ref.shape
    o_ref[...] = x_ref[...] + gt_ref[...] * y.reshape(nb, r, d)


def _mixer_out(x, m, a, gate, w_out_b, *, nb_blk, r_blk):
    nbat, r, d = x.shape
    nr = r // r_blk
    rows = nb_blk * r_blk
    half = m.shape[-1]
    return pl.pallas_call(
        _mixer_out_kernel,
        grid=((nbat // nb_blk) * nr,),
        in_specs=[
            pl.BlockSpec((nb_blk, r_blk, d), lambda i: (i // nr, i % nr, 0)),
            pl.BlockSpec((rows, half), lambda i: (i, 0)),
            pl.BlockSpec((rows, half), lambda i: (i, 0)),
            pl.BlockSpec((nb_blk, 1, d), lambda i: (i // nr, 0, 0)),
            pl.BlockSpec((d, d), lambda i: (0, 0)),
        ],
        out_specs=pl.BlockSpec((nb_blk, r_blk, d), lambda i: (i // nr, i % nr, 0)),
        out_shape=jax.ShapeDtypeStruct(x.shape, F32),
        compiler_params=_params(("arbitrary",)),
        name="mixer_out",
    )(x, m, a, gate[:, None, :], w_out_b)


def _row_subtiles(nb, r, count):
    if nb >= count:
        per = nb // count
        return [(slice(s * per, (s + 1) * per), slice(None), slice(s * per * r, (s + 1) * per * r))
                for s in range(count)]
    per = r // count
    return [(slice(None), slice(s * per, (s + 1) * per), slice(s * per, (s + 1) * per)) for s in range(count)]


def _ffn_kernel(x_ref, sh_ref, sc_ref, gt_ref, g_ref, wg_ref, wu_ref, wo_ref, gf_ref, shf_ref, scf_ref,
                o_ref, h_s, acc_s):
    f = pl.program_id(1)
    last = pl.num_programs(1) - 1
    nb, r, d = x_ref.shape
    subs = _row_subtiles(nb, r, FFN_ROW_GROUPS)

    def normalize(sub):
        bsl, rsl, rows = sub
        x = x_ref[bsl, rsl, :]
        y = x * lax.rsqrt(jnp.mean(x * x, axis=-1, keepdims=True) + EPS)
        hm = (y * g_ref[...]) * (1.0 + sc_ref[bsl]) + sh_ref[bsl]
        h_s[rows, :] = hm.reshape(-1, d).astype(BF16)

    def swiglu(rows):
        h = h_s[rows, :]
        zg = jnp.dot(h, wg_ref[...], preferred_element_type=F32)
        zu = jnp.dot(h, wu_ref[...], preferred_element_type=F32)
        act = (zg * jax.nn.sigmoid(zg) * zu).astype(BF16)
        return jnp.dot(act, wo_ref[...], preferred_element_type=F32)

    def finish(sub, acc):
        bsl, rsl, _ = sub
        x = x_ref[bsl, rsl, :]
        x2 = x + gt_ref[bsl] * acc.reshape(x.shape)
        y = x2 * lax.rsqrt(jnp.mean(x2 * x2, axis=-1, keepdims=True) + EPS)
        o_ref[bsl, rsl, :] = (y * gf_ref[...]) * (1.0 + scf_ref[bsl]) + shf_ref[bsl]

    @pl.when(f == 0)
    def _():
        for sub in subs:
            normalize(sub)
            acc_s[sub[2], :] = swiglu(sub[2])

    @pl.when(jnp.logical_and(f > 0, f < last))
    def _():
        acc_s[...] += swiglu(slice(None))

    @pl.when(f == last)
    def _():
        for sub in subs:
            finish(sub, acc_s[sub[2], :] + swiglu(sub[2]))


def _ffn(x, shift, scale, gate, g_ffn, w_in_b, w_out_b, g_final, shift_f, scale_f, *, nb_blk, r_blk, tf):
    nbat, r, d = x.shape
    d_ff = w_out_b.shape[0]
    nf = d_ff // tf
    assert nf >= 2, "first and last d_ff steps are distinct code paths"
    nr = r // r_blk
    rows = nb_blk * r_blk
    x_spec = pl.BlockSpec((nb_blk, r_blk, d), lambda i, f: (i // nr, i % nr, 0))
    vec_spec = pl.BlockSpec((nb_blk, 1, d), lambda i, f: (i // nr, 0, 0))
    par_spec = pl.BlockSpec((1, 1, d), lambda i, f: (0, 0, 0))
    return pl.pallas_call(
        _ffn_kernel,
        grid=((nbat // nb_blk) * nr, nf),
        in_specs=[
            x_spec, vec_spec, vec_spec, vec_spec, par_spec,
            pl.BlockSpec((d, tf), lambda i, f: (0, f)),
            pl.BlockSpec((d, tf), lambda i, f: (0, f + nf)),
            pl.BlockSpec((tf, d), lambda i, f: (f, 0)),
            par_spec, vec_spec, vec_spec,
        ],
        out_specs=x_spec,
        out_shape=jax.ShapeDtypeStruct(x.shape, F32),
        scratch_shapes=[pltpu.VMEM((rows, d), BF16), pltpu.VMEM((rows, d), F32)],
        compiler_params=_params(("arbitrary", "arbitrary")),
        name="ffn",
    )(x, shift[:, None, :], scale[:, None, :], gate[:, None, :], g_ffn.reshape(1, 1, d),
      w_in_b, w_in_b, w_out_b, g_final.reshape(1, 1, d), shift_f[:, None, :], scale_f[:, None, :])


ATTN_Q_BLOCK = 1024
ATTN_K_BEAT = 512
PROMPT_ROWS = 512
SAMPLE_BATCH_BLOCK = 8
FFN_TILE = 512
FFN_ROW_GROUPS = 2


def kernel(x_prompt, x_sample, cache_k, cache_v, c_prompt, c_sample, rel_bias, w_ada, b_ada, w_ada_final,
           b_ada_final, g_mix, g_ffn, g_final, w_in, mlp_ln_g, mlp_ln_b, w_s, b_s, lambda_q1, lambda_k1,
           lambda_q2, lambda_k2, sub_g, w_out, w_ffn_in, w_ffn_out):
    B, S, D = x_prompt.shape
    DB, T, _ = x_sample.shape
    depth = w_in.shape[0]
    past = cache_k.shape[2]
    width = N_HEADS * V_HEAD_DIM
    mlp_chunk = w_s.shape[-1]

    c_all = jnp.concatenate([c_prompt, c_sample], axis=0)
    mod_f = _adaln(c_all, w_ada_final, b_ada_final)
    bias_p = _rel_bias_tiles(rel_bias, ATTN_Q_BLOCK, ATTN_Q_BLOCK + ATTN_K_BEAT, ATTN_K_BEAT, 0, True, True)
    bias_s = _rel_bias_tiles(rel_bias, T, past + T, past, 0, False, False)

    assert depth == 1, "the final adaLN norm is fused into the single layer's FFN kernel"
    lam_init = 0.8 - 0.6 * math.exp(-0.3 * 0)
    lam_vecs = jnp.stack([lambda_q1[0], lambda_k1[0], lambda_q2[0], lambda_k2[0]])
    subg = sub_g.reshape(1, V_HEAD_DIM)
    mod = _adaln(c_all, w_ada[0], b_ada[0])
    sh1, sc1, gt1, sh2, sc2, gt2 = jnp.split(mod, 6, axis=-1)
    shf, scf = jnp.split(mod_f, 2, axis=-1)
    w_in_b = w_in[0].astype(BF16)
    w_out_b = w_out[0].astype(BF16)
    w_f_in_b = w_ffn_in[0].astype(BF16)
    w_f_out_b = w_ffn_out[0].astype(BF16)
    mixer_w = (g_mix[0], w_in_b, mlp_ln_g[0], mlp_ln_b[0], w_s[0], b_s[0])

    m, q, kp, kb, vp, vt = _mixer_in(x_prompt, sh1[:B], sc1[:B], *mixer_w, nb_blk=1, r_blk=PROMPT_ROWS,
                                     t_chunk=mlp_chunk, emit_gv=False, v_transposed=True)
    a = _attn_prompt(q.reshape(B, S, width), kb.reshape(B, S, width), vt, bias_p, lam_vecs,
                     subg.reshape(V_HEAD_DIM, 1), bq=ATTN_Q_BLOCK, bk=ATTN_K_BEAT, lam_init=lam_init)
    xp = _mixer_out(x_prompt, m, a.reshape(B * S, width), gt1[:B], w_out_b, nb_blk=1, r_blk=PROMPT_ROWS)
    yp = _ffn(xp, sh2[:B], sc2[:B], gt2[:B], g_ffn[0], w_f_in_b, w_f_out_b, g_final, shf[:B], scf[:B],
              nb_blk=1, r_blk=PROMPT_ROWS, tf=FFN_TILE)

    m, q, ks, kb, vs, vb, gvs = _mixer_in(x_sample, sh1[B:], sc1[B:], *mixer_w, nb_blk=SAMPLE_BATCH_BLOCK,
                                          r_blk=T, t_chunk=T, emit_gv=True, v_transposed=False)
    a = _attn_sample(q.reshape(DB, T, width), cache_k.reshape(DB, past * N_HEADS, LANES),
                     cache_v.reshape(DB, past * N_HEADS, LANES), kb.reshape(DB, T, width),
                     vb.reshape(DB, T, width), bias_s, lam_vecs, subg, lam_init=lam_init)
    xs = _mixer_out(x_sample, m, a.reshape(DB * T, width), gt1[B:], w_out_b, nb_blk=SAMPLE_BATCH_BLOCK, r_blk=T)
    ys = _ffn(xs, sh2[B:], sc2[B:], gt2[B:], g_ffn[0], w_f_in_b, w_f_out_b, g_final, shf[B:], scf[B:],
              nb_blk=SAMPLE_BATCH_BLOCK, r_blk=T, tf=FFN_TILE)

    head_shape = (N_HEADS, V_HEAD_DIM)
    return (yp, ys, kp.reshape(1, B, S, *head_shape), vp.reshape(1, B, S, *head_shape),
            ks.reshape(1, DB, T, *head_shape), vs.reshape(1, DB, T, *head_shape),
            gvs.reshape(1, DB, T, MLP_GROUPS, MLP_GROUP_DIM))
```

```python
import functools
import math

import jax
import jax.numpy as jnp
from jax import lax
from jax.experimental import pallas as pl
from jax.experimental.pallas import tpu as pltpu

LANES = 128
SUBLANES = 8
VMEM_LIMIT_BYTES = 56 * 1024 * 1024
MXU_WIDTH = 256

ATTN_STRIP = MXU_WIDTH
ONES_ROWS = SUBLANES
R_INIT_KEYS = 128

CHUNK = 64
N_HEADS = 8
HEAD_DIM = 64
V_HEAD_DIM = 128
MLP_GROUPS = 8
MLP_GROUP_DIM = 128
N_BUCKETS = 32
MAX_DISTANCE = 128
EPS = 1e-6
MASK_VALUE = -1e30
LOG2E = math.log2(math.e)

BF16 = jnp.bfloat16
F32 = jnp.float32


def _params(sem):
    return pltpu.CompilerParams(dimension_semantics=sem, vmem_limit_bytes=VMEM_LIMIT_BYTES)


def _adaln_kernel(c_ref, w_ref, b_ref, o_ref):
    c = c_ref[...]
    a = c * jax.nn.sigmoid(c)
    o_ref[...] = jnp.dot(a, w_ref[...], preferred_element_type=F32) + b_ref[...]


def _adaln(c, w, b, tn=1024):
    rows, d = c.shape
    n = w.shape[1]
    return pl.pallas_call(
        _adaln_kernel,
        grid=(n // tn,),
        in_specs=[
            pl.BlockSpec((rows, d), lambda j: (0, 0)),
            pl.BlockSpec((d, tn), lambda j: (0, j)),
            pl.BlockSpec((1, tn), lambda j: (0, j)),
        ],
        out_specs=pl.BlockSpec((rows, tn), lambda j: (0, j)),
        out_shape=jax.ShapeDtypeStruct((rows, n), F32),
        compiler_params=_params(("arbitrary",)),
        name="adaln",
    )(c, w, b.reshape(1, n))


def _rel_bias_kernel(tab_ref, o_ref, *, nq, nk, q_start, k_start, shift_far, keys_on_rows):
    h = pl.program_id(0)
    nb = N_BUCKETS // 2
    max_exact = nb // 2
    q_axis, k_axis = (1, 0) if keys_on_rows else (0, 1)
    rows, cols = (nk, nq) if keys_on_rows else (nq, nk)
    shift = tab_ref[nb - 1, h] if shift_far else 0.0

    def block(shape, q0, k0):
        q_pos = q0 + lax.broadcasted_iota(jnp.int32, shape, q_axis)
        k_pos = k0 + lax.broadcasted_iota(jnp.int32, shape, k_axis)
        rel = k_pos - q_pos
        ret = jnp.where(rel > 0, nb, 0)
        n = jnp.abs(rel)
        nf = jnp.maximum(n, 1).astype(F32)
        large = max_exact + (jnp.log(nf / max_exact) / math.log(MAX_DISTANCE / max_exact)
                             * (nb - max_exact)).astype(jnp.int32)
        large = jnp.minimum(large, nb - 1)
        bucket = ret + jnp.where(n < max_exact, n, large)
        bias = jnp.zeros(shape, F32)
        for bkt in range(N_BUCKETS):
            bias = jnp.where(bucket == bkt, tab_ref[bkt, h], bias)
        allowed = (k_pos // CHUNK) <= (q_pos // CHUNK)
        return jnp.where(allowed, (bias - shift) * LOG2E, MASK_VALUE)

    if rows % LANES or cols % LANES:
        o_ref[0] = block((rows, cols), q_start, k_start)
        return
    far_value = (tab_ref[nb - 1, h] - shift) * LOG2E
    for rb in range(rows // LANES):
        for cb in range(cols // LANES):
            k0 = k_start + LANES * (rb if keys_on_rows else cb)
            q0 = q_start + LANES * (cb if keys_on_rows else rb)
            sl = (0, slice(rb * LANES, (rb + 1) * LANES), slice(cb * LANES, (cb + 1) * LANES))
            if (k0 - q0) + (LANES - 1) <= -MAX_DISTANCE:
                o_ref[sl] = jnp.full((LANES, LANES), far_value, F32)
            elif (k0 - q0) - (LANES - 1) >= CHUNK:
                o_ref[sl] = jnp.full((LANES, LANES), MASK_VALUE, F32)
            else:
                o_ref[sl] = block((LANES, LANES), q0, k0)


def _rel_bias_tiles(rel_bias, nq, nk, q_start, k_start, shift_far, keys_on_rows):
    out_tile = (nk, nq) if keys_on_rows else (nq, nk)
    return pl.pallas_call(
        functools.partial(_rel_bias_kernel, nq=nq, nk=nk, q_start=q_start, k_start=k_start,
                          shift_far=shift_far, keys_on_rows=keys_on_rows),
        grid=(N_HEADS,),
        in_specs=[pl.BlockSpec(memory_space=pltpu.SMEM)],
        out_specs=pl.BlockSpec((1,) + out_tile, lambda h: (h, 0, 0)),
        out_shape=jax.ShapeDtypeStruct((N_HEADS,) + out_tile, F32),
        compiler_params=_params(("arbitrary",)),
        name="rel_bias",
    )(rel_bias)


def _mixer_in_kernel(x_ref, sh_ref, sc_ref, g_ref, w_ref, lng_ref, lnb_ref, ws_ref, bs_ref,
                     *refs, t_chunk, emit_gv, v_transposed):
    if emit_gv:
        m_ref, q_ref, k_ref, kb_ref, v_ref, vb_ref, gv_ref, u_s = refs
    else:
        m_ref, q_ref, k_ref, kb_ref, v_ref, vb_ref, u_s = refs
        gv_ref = None
    rows = u_s.shape[0]
    width = u_s.shape[1]

    def store_per_head(ref, val):
        for hd in range(N_HEADS):
            ref[pl.ds(hd, rows, stride=N_HEADS), :] = val[:, hd * LANES:(hd + 1) * LANES]

    x = x_ref[...]
    y = x * lax.rsqrt(jnp.mean(x * x, axis=-1, keepdims=True) + EPS)
    y = y * g_ref[...]
    hm = y * (1.0 + sc_ref[...]) + sh_ref[...]
    h = hm.reshape(rows, hm.shape[-1]).astype(BF16)

    def project(seg):
        return jnp.dot(h, w_ref[:, seg * width:(seg + 1) * width], preferred_element_type=F32)

    u_s[...] = jax.nn.gelu(project(0))

    g = jax.nn.gelu(project(1))
    mu = jnp.mean(g, axis=-1, keepdims=True)
    var = jnp.mean(jnp.square(g - mu), axis=-1, keepdims=True)
    gv = (g - mu) * lax.rsqrt(var + EPS) * lng_ref[...] + lnb_ref[...]
    if emit_gv:
        store_per_head(gv_ref, gv)
    gvb = gv.astype(BF16)
    ii = lax.broadcasted_iota(jnp.int32, (t_chunk, t_chunk), 0)
    jj = lax.broadcasted_iota(jnp.int32, (t_chunk, t_chunk), 1)
    mask = (jj // CHUNK) <= (ii // CHUNK)
    for grp in range(MLP_GROUPS):
        wg = jnp.where(mask, ws_ref[grp], 0.0).astype(BF16)
        bg = bs_ref[grp]
        cs = slice(grp * MLP_GROUP_DIM, (grp + 1) * MLP_GROUP_DIM)
        for c in range(rows // t_chunk):
            rs = slice(c * t_chunk, (c + 1) * t_chunk)
            mixed = jnp.dot(wg, gvb[rs, cs], preferred_element_type=F32) + bg
            m_ref[rs, cs] = (u_s[rs, cs] * mixed).astype(BF16)

    q_ref[...] = (project(2) * (HEAD_DIM ** -0.5 * LOG2E)).astype(BF16)

    z = project(3)
    store_per_head(k_ref, z)
    kb_ref[...] = z.astype(BF16)

    z = project(4)
    store_per_head(v_ref, z)
    if v_transposed:
        vb_ref[0] = z.T.astype(BF16)
    else:
        vb_ref[...] = z.astype(BF16)


def _mixer_in(x, shift, scale, g_mix, w_in_b, ln_g, ln_b, w_s, b_s, *, nb_blk, r_blk, t_chunk, emit_gv,
              v_transposed):
    nbat, r, d = x.shape
    width = 1024
    nr = r // r_blk
    n_tiles = (nbat // nb_blk) * nr
    rows = nb_blk * r_blk
    tokens = nbat * r
    row_idx = lambda i: (i, 0)
    flat = jax.ShapeDtypeStruct((tokens, width), BF16)
    per_head = jax.ShapeDtypeStruct((tokens * N_HEADS, LANES), F32)
    flat_spec = pl.BlockSpec((rows, width), row_idx)
    per_head_spec = pl.BlockSpec((rows * N_HEADS, LANES), row_idx)
    out_shape = [flat, flat, per_head, flat, per_head, flat]
    out_specs = [flat_spec, flat_spec, per_head_spec, flat_spec, per_head_spec, flat_spec]
    if v_transposed:
        assert nb_blk == 1
        out_shape[5] = jax.ShapeDtypeStruct((nbat, width, r), BF16)
        out_specs[5] = pl.BlockSpec((1, width, r_blk), lambda i: (i // nr, 0, i % nr))
    if emit_gv:
        out_shape.append(per_head)
        out_specs.append(per_head_spec)
    ws_t = w_s[:, :t_chunk, :t_chunk]
    bs_t = b_s[:, :t_chunk, None]
    once = pl.Buffered(1)
    return pl.pallas_call(
        functools.partial(_mixer_in_kernel, t_chunk=t_chunk, emit_gv=emit_gv, v_transposed=v_transposed),
        grid=(n_tiles,),
        in_specs=[
            pl.BlockSpec((nb_blk, r_blk, d), lambda i: (i // nr, i % nr, 0)),
            pl.BlockSpec((nb_blk, 1, d), lambda i: (i // nr, 0, 0)),
            pl.BlockSpec((nb_blk, 1, d), lambda i: (i // nr, 0, 0)),
            pl.BlockSpec((1, 1, d), lambda i: (0, 0, 0)),
            pl.BlockSpec(w_in_b.shape, lambda i: (0, 0), pipeline_mode=once),
            pl.BlockSpec((1, width), lambda i: (0, 0)),
            pl.BlockSpec((1, width), lambda i: (0, 0)),
            pl.BlockSpec((MLP_GROUPS, t_chunk, t_chunk), lambda i: (0, 0, 0)),
            pl.BlockSpec((MLP_GROUPS, t_chunk, 1), lambda i: (0, 0, 0)),
        ],
        out_specs=out_specs,
        out_shape=out_shape,
        scratch_shapes=[pltpu.VMEM((rows, width), F32)],
        compiler_params=_params(("arbitrary",)),
        name="mixer_in",
    )(x, shift[:, None, :], scale[:, None, :], g_mix.reshape(1, 1, d), w_in_b,
      ln_g.reshape(1, width), ln_b.reshape(1, width), ws_t, bs_t)


def _split_q(q):
    lane = lax.broadcasted_iota(jnp.int32, q.shape, 1)
    zero = jnp.zeros_like(q)
    return jnp.concatenate([jnp.where(lane < HEAD_DIM, q, zero), jnp.where(lane >= HEAD_DIM, q, zero)], axis=0)


def _softmax_step(s, v, m, l, acc):
    m_new = jnp.maximum(m, jnp.max(s, axis=-1, keepdims=True))
    alpha = jnp.exp2(m - m_new)
    p = jnp.exp2(s - m_new)
    l_new = alpha * l + jnp.sum(p, axis=-1, keepdims=True)
    acc_new = alpha * acc + jnp.dot(p.astype(BF16), v, preferred_element_type=F32)
    return m_new, l_new, acc_new


def _diff_lambda(lam_ref, lam_init):
    lv = lam_ref[...]
    s1 = jnp.sum(lv[0:1] * lv[1:2], axis=-1, keepdims=True)
    s2 = jnp.sum(lv[2:3] * lv[3:4], axis=-1, keepdims=True)
    return jnp.exp(s1) - jnp.exp(s2) + lam_init


def _finish_heads(l, acc, lam, subg, lam_init, n):
    o = acc / l
    a = o[:n] - lam * o[n:]
    a = a * lax.rsqrt(jnp.mean(a * a, axis=-1, keepdims=True) + EPS)
    return a * subg * (1.0 - lam_init)


def _qk(qs, k):
    return lax.dot_general(qs, k, (((1,), (1,)), ((), ())), preferred_element_type=F32)


def _attn_prompt_kernel(q_ref, k_ref, vt_ref, bias_ref, lam_ref, subg_ref, o_ref,
                        p_s, alpha_s, r_s, acc_s, *, bq, bk, lam_init):
    qi = pl.program_id(2)
    halves = bq // ATTN_STRIP
    strips = [(c, h) for c in range(2) for h in range(halves)]
    n_near = bq // bk + 1
    q = q_ref[0]
    lane = lax.broadcasted_iota(jnp.int32, q.shape, 1)
    zero = jnp.zeros_like(q)
    qs = (jnp.where(lane < HEAD_DIM, q, zero), jnp.where(lane >= HEAD_DIM, q, zero))

    ones = jnp.ones((ONES_ROWS, bk), BF16)

    def keys(j):
        return k_ref[0, pl.ds(pl.multiple_of(j * bk, bk), bk), :]

    def values(j):
        vt = vt_ref[0, :, pl.ds(pl.multiple_of(j * bk, bk), bk)]
        return jnp.concatenate([vt, ones], axis=0)

    def logits(k, n, bias):
        c, h = strips[n]
        cols = slice(h * ATTN_STRIP, (h + 1) * ATTN_STRIP)
        s = lax.dot_general(k, qs[c][cols], (((1,), (1,)), ((), ())), preferred_element_type=F32)
        if isinstance(bias, tuple):
            s = s + bias_ref[0, pl.ds(bias[0], bias[1]), cols]
        elif bias is not None:
            s = s + bias
        return s

    def beat(new=None, old=None):
        if new is not None:
            k = keys(new[0])
        if old is not None:
            vt1 = values(old[0])
        for n in range(len(strips)):
            if new is not None and n not in new[3]:
                _, slot, bias, _ = new
                s = logits(k, n, bias)
                r = r_s[n]
                p_s[slot, n] = jnp.exp2(s - r).astype(BF16)
                r_new = jnp.maximum(r, jnp.max(s, axis=0, keepdims=True))
                alpha_s[slot, n] = jnp.exp2(r - r_new)
                r_s[n] = r_new
            if old is not None and n not in old[2]:
                slot = old[1]
                pv = jnp.dot(vt1, p_s[slot, n], preferred_element_type=F32)
                acc_s[n] = (acc_s[n] + pv) * alpha_s[slot, n]

    n_far = jnp.maximum((qi * bq - MAX_DISTANCE) // bk, 0)
    first = jnp.where(qi == 0, 1, 0)
    near_bias = [(pl.multiple_of((i + first) * bk, bk), bk) for i in range(n_near)]

    k0 = k_ref[0, :R_INIT_KEYS, :]
    for n, (_, h) in enumerate(strips):
        bias0 = jnp.where(qi == 0, bias_ref[0, bk:bk + R_INIT_KEYS, h * ATTN_STRIP:(h + 1) * ATTN_STRIP], 0.0)
        r_s[n] = jnp.max(logits(k0, n, bias0), axis=0, keepdims=True)
    acc_s[...] = jnp.zeros(acc_s.shape, F32)
    alpha_s[1] = jnp.ones(alpha_s.shape[1:], F32)
    p_s[1] = jnp.zeros(p_s.shape[1:], BF16)

    n_pairs = n_far // 2

    def pair(j):
        beat(new=(j, 0, None, ()), old=(jnp.maximum(j - 1, 0), 1, ()))
        beat(new=(j + 1, 1, None, ()), old=(j, 0, ()))

    n_quads = n_far // 4

    @pl.loop(0, n_quads)
    def _(t):
        pair(4 * t)
        pair(4 * t + 2)

    @pl.loop(2 * n_quads, n_pairs)
    def _(t):
        pair(2 * t)

    x = 2 * n_pairs
    tail = [(x, jnp.where(n_far % 2 == 1, 0.0, MASK_VALUE), ())]
    for i, bias_i in enumerate(near_bias):
        skip = tuple(n for n, (_, h) in enumerate(strips) if (i - 1) * bk >= (h + 1) * ATTN_STRIP)
        tail.append((n_far + i, bias_i, skip))
    for t, (j, bias_t, skip) in enumerate(tail):
        prev = (jnp.maximum(x - 1, 0), 1, ()) if t == 0 else (tail[t - 1][0], (t - 1) % 2, tail[t - 1][2])
        beat(new=(j, t % 2, bias_t, skip), old=prev)
    beat(old=(tail[-1][0], (len(tail) - 1) % 2, tail[-1][2]))

    lam = _diff_lambda(lam_ref, lam_init)

    def emit_output():
        for h in range(halves):
            o = []
            for c in range(2):
                acc = acc_s[c * halves + h]
                o.append(acc[:V_HEAD_DIM] / acc[V_HEAD_DIM:V_HEAD_DIM + 1])
            a = o[0] - lam * o[1]
            a = a * lax.rsqrt(jnp.mean(a * a, axis=0, keepdims=True) + EPS)
            a = a * subg_ref[...] * (1.0 - lam_init)
            o_ref[0, h * ATTN_STRIP:(h + 1) * ATTN_STRIP, :] = a.T.astype(o_ref.dtype)

    emit_output()

    acc_all = acc_s[...]
    bad = jnp.where(jnp.abs(acc_all) < jnp.inf, 0.0, 1.0)
    bad = jnp.maximum(bad, jnp.where(acc_all[:, V_HEAD_DIM:V_HEAD_DIM + 1] > 0.0, 0.0, 1.0))
    bad = jnp.max(jnp.max(bad, axis=0), axis=0, keepdims=True)
    bad = jnp.max(bad, axis=1, keepdims=True)[0, 0]

    @pl.when(bad > 0.0)
    def _():
        r_s[...] = jnp.full(r_s.shape, MASK_VALUE, F32)
        acc_s[...] = jnp.zeros(acc_s.shape, F32)

        @pl.loop(0, n_far + n_near)
        def _(j):
            start = pl.multiple_of(jnp.maximum(j - n_far + first, 0) * bk, bk)
            k = keys(j)
            vt1 = values(j)
            for n, (_, h) in enumerate(strips):
                tile = bias_ref[0, pl.ds(start, bk), h * ATTN_STRIP:(h + 1) * ATTN_STRIP]
                s = logits(k, n, jnp.where(j < n_far, 0.0, tile))
                m_old = r_s[n]
                m_new = jnp.maximum(m_old, jnp.max(s, axis=0, keepdims=True))
                p = jnp.exp2(s - m_new).astype(BF16)
                acc_s[n] = jnp.exp2(m_old - m_new) * acc_s[n] + jnp.dot(vt1, p, preferred_element_type=F32)
                r_s[n] = m_new

        emit_output()


def _attn_prompt(q, k, vt, bias, lam_vecs, sub_g_col, *, bq, bk, lam_init):
    b, s, _ = q.shape
    n_strips = 2 * (bq // ATTN_STRIP)
    assert bq % bk == 0 and bk >= MAX_DISTANCE and s >= bq + bk
    return pl.pallas_call(
        functools.partial(_attn_prompt_kernel, bq=bq, bk=bk, lam_init=lam_init),
        grid=(b, N_HEADS, s // bq),
        in_specs=[
            pl.BlockSpec((1, bq, LANES), lambda bi, h, qi: (bi, qi, h)),
            pl.BlockSpec((1, s, LANES), lambda bi, h, qi: (bi, 0, h)),
            pl.BlockSpec((1, V_HEAD_DIM, s), lambda bi, h, qi: (bi, h, 0)),
            pl.BlockSpec((1, bq + 2 * bk, bq), lambda bi, h, qi: (h, 0, 0)),
            pl.BlockSpec((4, HEAD_DIM), lambda bi, h, qi: (0, 0)),
            pl.BlockSpec((V_HEAD_DIM, 1), lambda bi, h, qi: (0, 0)),
        ],
        out_specs=pl.BlockSpec((1, bq, LANES), lambda bi, h, qi: (bi, qi, h)),
        out_shape=jax.ShapeDtypeStruct((b, s, N_HEADS * V_HEAD_DIM), BF16),
        scratch_shapes=[
            pltpu.VMEM((2, n_strips, bk, ATTN_STRIP), BF16),
            pltpu.VMEM((2, n_strips, 1, ATTN_STRIP), F32),
            pltpu.VMEM((n_strips, 1, ATTN_STRIP), F32),
            pltpu.VMEM((n_strips, V_HEAD_DIM + ONES_ROWS, ATTN_STRIP), F32),
        ],
        compiler_params=_params(("arbitrary", "arbitrary", "arbitrary")),
        name="attn_prompt",
    )(q, k, vt, bias, lam_vecs, sub_g_col)


def _attn_sample_kernel(q_ref, ck_ref, cv_ref, nk_ref, nv_ref, bias_ref, lam_ref, subg_ref, o_ref,
                        *, past, lam_init):
    t = q_ref.shape[1]
    lam = _diff_lambda(lam_ref, lam_init)
    for hd in range(N_HEADS):
        cs = slice(hd * LANES, (hd + 1) * LANES)
        qs = _split_q(q_ref[0, :, cs])
        bias = bias_ref[hd]
        bias2 = jnp.concatenate([bias, bias], axis=0)
        carry = (jnp.full((2 * t, 1), MASK_VALUE, F32), jnp.zeros((2 * t, 1), F32),
                 jnp.zeros((2 * t, V_HEAD_DIM), F32))
        kc = ck_ref[0, pl.ds(hd, past, stride=N_HEADS), :].astype(BF16)
        vc = cv_ref[0, pl.ds(hd, past, stride=N_HEADS), :].astype(BF16)
        carry = _softmax_step(_qk(qs, kc) + bias2[:, :past], vc, *carry)
        m, l, acc = _softmax_step(_qk(qs, nk_ref[0, :, cs]) + bias2[:, past:], nv_ref[0, :, cs], *carry)
        o_ref[0, :, cs] = _finish_heads(l, acc, lam, subg_ref[...], lam_init, t).astype(o_ref.dtype)


def _attn_sample(q, cache_k, cache_v, new_k, new_v, bias, lam_vecs, sub_g, *, lam_init):
    b, t, width = q.shape
    past = cache_k.shape[1] // N_HEADS
    flat_spec = pl.BlockSpec((1, t, width), lambda bi: (bi, 0, 0))
    cache_spec = pl.BlockSpec((1, past * N_HEADS, LANES), lambda bi: (bi, 0, 0))
    return pl.pallas_call(
        functools.partial(_attn_sample_kernel, past=past, lam_init=lam_init),
        grid=(b,),
        in_specs=[
            flat_spec, cache_spec, cache_spec, flat_spec, flat_spec,
            pl.BlockSpec((N_HEADS, t, past + t), lambda bi: (0, 0, 0)),
            pl.BlockSpec((4, HEAD_DIM), lambda bi: (0, 0)),
            pl.BlockSpec((1, V_HEAD_DIM), lambda bi: (0, 0)),
        ],
        out_specs=flat_spec,
        out_shape=jax.ShapeDtypeStruct((b, t, width), BF16),
        compiler_params=_params(("arbitrary",)),
        name="attn_sample",
    )(q, cache_k, cache_v, new_k, new_v, bias, lam_vecs, sub_g)


def _mixer_out_kernel(x_ref, m_ref, a_ref, gt_ref, w_ref, o_ref):
    half = m_ref.shape[-1]
    y = jnp.dot(m_ref[...], w_ref[:half, :], preferred_element_type=F32)
    y = y + jnp.dot(a_ref[...], w_ref[half:, :], preferred_element_type=F32)
    nb, r, d = x_ref.shape
    o_ref[...] = x_ref[...] + gt_ref[...] * y.reshape(nb, r, d)


def _mixer_out(x, m, a, gate, w_out_b, *, nb_blk, r_blk):
    nbat, r, d = x.shape
    nr = r // r_blk
    rows = nb_blk * r_blk
    half = m.shape[-1]
    return pl.pallas_call(
        _mixer_out_kernel,
        grid=((nbat // nb_blk) * nr,),
        in_specs=[
            pl.BlockSpec((nb_blk, r_blk, d), lambda i: (i // nr, i % nr, 0)),
            pl.BlockSpec((rows, half), lambda i: (i, 0)),
            pl.BlockSpec((rows, half), lambda i: (i, 0)),
            pl.BlockSpec((nb_blk, 1, d), lambda i: (i // nr, 0, 0)),
            pl.BlockSpec((d, d), lambda i: (0, 0)),
        ],
        out_specs=pl.BlockSpec((nb_blk, r_blk, d), lambda i: (i // nr, i % nr, 0)),
        out_shape=jax.ShapeDtypeStruct(x.shape, F32),
        compiler_params=_params(("arbitrary",)),
        name="mixer_out",
    )(x, m, a, gate[:, None, :], w_out_b)


def _row_subtiles(nb, r, count):
    if nb >= count:
        per = nb // count
        return [(slice(s * per, (s + 1) * per), slice(None), slice(s * per * r, (s + 1) * per * r))
                for s in range(count)]
    per = r // count
    return [(slice(None), slice(s * per, (s + 1) * per), slice(s * per, (s + 1) * per)) for s in range(count)]


def _ffn_kernel(x_ref, sh_ref, sc_ref, gt_ref, g_ref, wg_ref, wu_ref, wo_ref, gf_ref, shf_ref, scf_ref,
                o_ref, h_s, acc_s):
    f = pl.program_id(1)
    last = pl.num_programs(1) - 1
    nb, r, d = x_ref.shape
    subs = _row_subtiles(nb, r, FFN_ROW_GROUPS)

    def normalize(sub):
        bsl, rsl, rows = sub
        x = x_ref[bsl, rsl, :]
        y = x * lax.rsqrt(jnp.mean(x * x, axis=-1, keepdims=True) + EPS)
        hm = (y * g_ref[...]) * (1.0 + sc_ref[bsl]) + sh_ref[bsl]
        h_s[rows, :] = hm.reshape(-1, d).astype(BF16)

    def swiglu(rows):
        h = h_s[rows, :]
        zg = jnp.dot(h, wg_ref[...], preferred_element_type=F32)
        zu = jnp.dot(h, wu_ref[...], preferred_element_type=F32)
        act = (zg * jax.nn.sigmoid(zg) * zu).astype(BF16)
        return jnp.dot(act, wo_ref[...], preferred_element_type=F32)

    def finish(sub, acc):
        bsl, rsl, _ = sub
        x = x_ref[bsl, rsl, :]
        x2 = x + gt_ref[bsl] * acc.reshape(x.shape)
        y = x2 * lax.rsqrt(jnp.mean(x2 * x2, axis=-1, keepdims=True) + EPS)
        o_ref[bsl, rsl, :] = (y * gf_ref[...]) * (1.0 + scf_ref[bsl]) + shf_ref[bsl]

    @pl.when(f == 0)
    def _():
        for sub in subs:
            normalize(sub)
            acc_s[sub[2], :] = swiglu(sub[2])

    @pl.when(jnp.logical_and(f > 0, f < last))
    def _():
        acc_s[...] += swiglu(slice(None))

    @pl.when(f == last)
    def _():
        for sub in subs:
            finish(sub, acc_s[sub[2], :] + swiglu(sub[2]))


def _ffn(x, shift, scale, gate, g_ffn, w_in_b, w_out_b, g_final, shift_f, scale_f, *, nb_blk, r_blk, tf):
    nbat, r, d = x.shape
    d_ff = w_out_b.shape[0]
    nf = d_ff // tf
    assert nf >= 2, "first and last d_ff steps are distinct code paths"
    nr = r // r_blk
    rows = nb_blk * r_blk
    x_spec = pl.BlockSpec((nb_blk, r_blk, d), lambda i, f: (i // nr, i % nr, 0))
    vec_spec = pl.BlockSpec((nb_blk, 1, d), lambda i, f: (i // nr, 0, 0))
    par_spec = pl.BlockSpec((1, 1, d), lambda i, f: (0, 0, 0))
    return pl.pallas_call(
        _ffn_kernel,
        grid=((nbat // nb_blk) * nr, nf),
        in_specs=[
            x_spec, vec_spec, vec_spec, vec_spec, par_spec,
            pl.BlockSpec((d, tf), lambda i, f: (0, f)),
            pl.BlockSpec((d, tf), lambda i, f: (0, f + nf)),
            pl.BlockSpec((tf, d), lambda i, f: (f, 0)),
            par_spec, vec_spec, vec_spec,
        ],
        out_specs=x_spec,
        out_shape=jax.ShapeDtypeStruct(x.shape, F32),
        scratch_shapes=[pltpu.VMEM((rows, d), BF16), pltpu.VMEM((rows, d), F32)],
        compiler_params=_params(("arbitrary", "arbitrary")),
        name="ffn",
    )(x, shift[:, None, :], scale[:, None, :], gate[:, None, :], g_ffn.reshape(1, 1, d),
      w_in_b, w_in_b, w_out_b, g_final.reshape(1, 1, d), shift_f[:, None, :], scale_f[:, None, :])


ATTN_Q_BLOCK = 1024
ATTN_K_BEAT = 512
PROMPT_ROWS = 512
SAMPLE_BATCH_BLOCK = 8
FFN_TILE = 512
FFN_ROW_GROUPS = 2


def kernel(x_prompt, x_sample, cache_k, cache_v, c_prompt, c_sample, rel_bias, w_ada, b_ada, w_ada_final,
           b_ada_final, g_mix, g_ffn, g_final, w_in, mlp_ln_g, mlp_ln_b, w_s, b_s, lambda_q1, lambda_k1,
           lambda_q2, lambda_k2, sub_g, w_out, w_ffn_in, w_ffn_out):
    B, S, D = x_prompt.shape
    DB, T, _ = x_sample.shape
    depth = w_in.shape[0]
    past = cache_k.shape[2]
    width = N_HEADS * V_HEAD_DIM
    mlp_chunk = w_s.shape[-1]

    c_all = jnp.concatenate([c_prompt, c_sample], axis=0)
    mod_f = _adaln(c_all, w_ada_final, b_ada_final)
    bias_p = _rel_bias_tiles(rel_bias, ATTN_Q_BLOCK, ATTN_Q_BLOCK + 2 * ATTN_K_BEAT, ATTN_K_BEAT, 0, True, True)
    bias_s = _rel_bias_tiles(rel_bias, T, past + T, past, 0, False, False)

    assert depth == 1, "the final adaLN norm is fused into the single layer's FFN kernel"
    lam_init = 0.8 - 0.6 * math.exp(-0.3 * 0)
    lam_vecs = jnp.stack([lambda_q1[0], lambda_k1[0], lambda_q2[0], lambda_k2[0]])
    subg = sub_g.reshape(1, V_HEAD_DIM)
    mod = _adaln(c_all, w_ada[0], b_ada[0])
    sh1, sc1, gt1, sh2, sc2, gt2 = jnp.split(mod, 6, axis=-1)
    shf, scf = jnp.split(mod_f, 2, axis=-1)
    w_in_b = w_in[0].astype(BF16)
    w_out_b = w_out[0].astype(BF16)
    w_f_in_b = w_ffn_in[0].astype(BF16)
    w_f_out_b = w_ffn_out[0].astype(BF16)
    mixer_w = (g_mix[0], w_in_b, mlp_ln_g[0], mlp_ln_b[0], w_s[0], b_s[0])

    m, q, kp, kb, vp, vt = _mixer_in(x_prompt, sh1[:B], sc1[:B], *mixer_w, nb_blk=1, r_blk=PROMPT_ROWS,
                                     t_chunk=mlp_chunk, emit_gv=False, v_transposed=True)
    a = _attn_prompt(q.reshape(B, S, width), kb.reshape(B, S, width), vt, bias_p, lam_vecs,
                     subg.reshape(V_HEAD_DIM, 1), bq=ATTN_Q_BLOCK, bk=ATTN_K_BEAT, lam_init=lam_init)
    xp = _mixer_out(x_prompt, m, a.reshape(B * S, width), gt1[:B], w_out_b, nb_blk=1, r_blk=PROMPT_ROWS)
    yp = _ffn(xp, sh2[:B], sc2[:B], gt2[:B], g_ffn[0], w_f_in_b, w_f_out_b, g_final, shf[:B], scf[:B],
              nb_blk=1, r_blk=PROMPT_ROWS, tf=FFN_TILE)

    m, q, ks, kb, vs, vb, gvs = _mixer_in(x_sample, sh1[B:], sc1[B:], *mixer_w, nb_blk=SAMPLE_BATCH_BLOCK,
                                          r_blk=T, t_chunk=T, emit_gv=True, v_transposed=False)
    a = _attn_sample(q.reshape(DB, T, width), cache_k.reshape(DB, past * N_HEADS, LANES),
                     cache_v.reshape(DB, past * N_HEADS, LANES), kb.reshape(DB, T, width),
                     vb.reshape(DB, T, width), bias_s, lam_vecs, subg, lam_init=lam_init)
    xs = _mixer_out(x_sample, m, a.reshape(DB * T, width), gt1[B:], w_out_b, nb_blk=SAMPLE_BATCH_BLOCK, r_blk=T)
    ys = _ffn(xs, sh2[B:], sc2[B:], gt2[B:], g_ffn[0], w_f_in_b, w_f_out_b, g_final, shf[B:], scf[B:],
              nb_blk=SAMPLE_BATCH_BLOCK, r_blk=T, tf=FFN_TILE)

    head_shape = (N_HEADS, V_HEAD_DIM)
    return (yp, ys, kp.reshape(1, B, S, *head_shape), vp.reshape(1, B, S, *head_shape),
            ks.reshape(1, DB, T, *head_shape), vs.reshape(1, DB, T, *head_shape),
            gvs.reshape(1, DB, T, MLP_GROUPS, MLP_GROUP_DIM))
```

```python
import functools
import math

import jax
import jax.numpy as jnp
from jax import lax
from jax.experimental import pallas as pl
from jax.experimental.pallas import tpu as pltpu

LANES = 128
SUBLANES = 8
VMEM_LIMIT_BYTES = 56 * 1024 * 1024
MXU_WIDTH = 256

ATTN_STRIP = MXU_WIDTH
ONES_ROWS = SUBLANES
R_INIT_KEYS = 128

CHUNK = 64
N_HEADS = 8
HEAD_DIM = 64
V_HEAD_DIM = 128
MLP_GROUPS = 8
MLP_GROUP_DIM = 128
N_BUCKETS = 32
MAX_DISTANCE = 128
EPS = 1e-6
MASK_VALUE = -1e30
LOG2E = math.log2(math.e)

BF16 = jnp.bfloat16
F32 = jnp.float32


def _params(sem):
    return pltpu.CompilerParams(dimension_semantics=sem, vmem_limit_bytes=VMEM_LIMIT_BYTES)


def _adaln_kernel(c_ref, w_ref, b_ref, o_ref):
    c = c_ref[...]
    a = c * jax.nn.sigmoid(c)
    o_ref[...] = jnp.dot(a, w_ref[...], preferred_element_type=F32) + b_ref[...]


def _adaln(c, w, b, tn=1024):
    rows, d = c.shape
    n = w.shape[1]
    return pl.pallas_call(
        _adaln_kernel,
        grid=(n // tn,),
        in_specs=[
            pl.BlockSpec((rows, d), lambda j: (0, 0)),
            pl.BlockSpec((d, tn), lambda j: (0, j)),
            pl.BlockSpec((1, tn), lambda j: (0, j)),
        ],
        out_specs=pl.BlockSpec((rows, tn), lambda j: (0, j)),
        out_shape=jax.ShapeDtypeStruct((rows, n), F32),
        compiler_params=_params(("arbitrary",)),
        name="adaln",
    )(c, w, b.reshape(1, n))


def _rel_bias_kernel(tab_ref, o_ref, *, nq, nk, q_start, k_start, shift_far, keys_on_rows):
    h = pl.program_id(0)
    nb = N_BUCKETS // 2
    max_exact = nb // 2
    q_axis, k_axis = (1, 0) if keys_on_rows else (0, 1)
    rows, cols = (nk, nq) if keys_on_rows else (nq, nk)
    shift = tab_ref[nb - 1, h] if shift_far else 0.0

    def block(shape, q0, k0):
        q_pos = q0 + lax.broadcasted_iota(jnp.int32, shape, q_axis)
        k_pos = k0 + lax.broadcasted_iota(jnp.int32, shape, k_axis)
        rel = k_pos - q_pos
        ret = jnp.where(rel > 0, nb, 0)
        n = jnp.abs(rel)
        nf = jnp.maximum(n, 1).astype(F32)
        large = max_exact + (jnp.log(nf / max_exact) / math.log(MAX_DISTANCE / max_exact)
                             * (nb - max_exact)).astype(jnp.int32)
        large = jnp.minimum(large, nb - 1)
        bucket = ret + jnp.where(n < max_exact, n, large)
        bias = jnp.zeros(shape, F32)
        for bkt in range(N_BUCKETS):
            bias = jnp.where(bucket == bkt, tab_ref[bkt, h], bias)
        allowed = (k_pos // CHUNK) <= (q_pos // CHUNK)
        return jnp.where(allowed, (bias - shift) * LOG2E, MASK_VALUE)

    if rows % LANES or cols % LANES:
        o_ref[0] = block((rows, cols), q_start, k_start)
        return
    far_value = (tab_ref[nb - 1, h] - shift) * LOG2E
    for rb in range(rows // LANES):
        for cb in range(cols // LANES):
            k0 = k_start + LANES * (rb if keys_on_rows else cb)
            q0 = q_start + LANES * (cb if keys_on_rows else rb)
            sl = (0, slice(rb * LANES, (rb + 1) * LANES), slice(cb * LANES, (cb + 1) * LANES))
            if (k0 - q0) + (LANES - 1) <= -MAX_DISTANCE:
                o_ref[sl] = jnp.full((LANES, LANES), far_value, F32)
            elif (k0 - q0) - (LANES - 1) >= CHUNK:
                o_ref[sl] = jnp.full((LANES, LANES), MASK_VALUE, F32)
            else:
                o_ref[sl] = block((LANES, LANES), q0, k0)


def _rel_bias_tiles(rel_bias, nq, nk, q_start, k_start, shift_far, keys_on_rows):
    out_tile = (nk, nq) if keys_on_rows else (nq, nk)
    return pl.pallas_call(
        functools.partial(_rel_bias_kernel, nq=nq, nk=nk, q_start=q_start, k_start=k_start,
                          shift_far=shift_far, keys_on_rows=keys_on_rows),
        grid=(N_HEADS,),
        in_specs=[pl.BlockSpec(memory_space=pltpu.SMEM)],
        out_specs=pl.BlockSpec((1,) + out_tile, lambda h: (h, 0, 0)),
        out_shape=jax.ShapeDtypeStruct((N_HEADS,) + out_tile, F32),
        compiler_params=_params(("arbitrary",)),
        name="rel_bias",
    )(rel_bias)


def _mixer_in_kernel(x_ref, sh_ref, sc_ref, g_ref, w_ref, lng_ref, lnb_ref, ws_ref, bs_ref,
                     *refs, t_chunk, emit_gv, v_transposed):
    if emit_gv:
        m_ref, q_ref, k_ref, kb_ref, v_ref, vb_ref, gv_ref, u_s = refs
    else:
        m_ref, q_ref, k_ref, kb_ref, v_ref, vb_ref, u_s = refs
        gv_ref = None
    rows = u_s.shape[0]
    width = u_s.shape[1]

    def store_per_head(ref, val):
        for hd in range(N_HEADS):
            ref[pl.ds(hd, rows, stride=N_HEADS), :] = val[:, hd * LANES:(hd + 1) * LANES]

    x = x_ref[...]
    y = x * lax.rsqrt(jnp.mean(x * x, axis=-1, keepdims=True) + EPS)
    y = y * g_ref[...]
    hm = y * (1.0 + sc_ref[...]) + sh_ref[...]
    h = hm.reshape(rows, hm.shape[-1]).astype(BF16)

    def project(seg):
        return jnp.dot(h, w_ref[:, seg * width:(seg + 1) * width], preferred_element_type=F32)

    u_s[...] = jax.nn.gelu(project(0))

    g = jax.nn.gelu(project(1))
    mu = jnp.mean(g, axis=-1, keepdims=True)
    var = jnp.mean(jnp.square(g - mu), axis=-1, keepdims=True)
    gv = (g - mu) * lax.rsqrt(var + EPS) * lng_ref[...] + lnb_ref[...]
    if emit_gv:
        store_per_head(gv_ref, gv)
    gvb = gv.astype(BF16)
    ii = lax.broadcasted_iota(jnp.int32, (t_chunk, t_chunk), 0)
    jj = lax.broadcasted_iota(jnp.int32, (t_chunk, t_chunk), 1)
    mask = (jj // CHUNK) <= (ii // CHUNK)
    for grp in range(MLP_GROUPS):
        wg = jnp.where(mask, ws_ref[grp], 0.0).astype(BF16)
        bg = bs_ref[grp]
        cs = slice(grp * MLP_GROUP_DIM, (grp + 1) * MLP_GROUP_DIM)
        for c in range(rows // t_chunk):
            rs = slice(c * t_chunk, (c + 1) * t_chunk)
            mixed = jnp.dot(wg, gvb[rs, cs], preferred_element_type=F32) + bg
            m_ref[rs, cs] = (u_s[rs, cs] * mixed).astype(BF16)

    q_ref[...] = (project(2) * (HEAD_DIM ** -0.5 * LOG2E)).astype(BF16)

    z = project(3)
    store_per_head(k_ref, z)
    kb_ref[...] = z.astype(BF16)

    z = project(4)
    store_per_head(v_ref, z)
    if v_transposed:
        vb_ref[0] = z.T.astype(BF16)
    else:
        vb_ref[...] = z.astype(BF16)


def _mixer_in(x, shift, scale, g_mix, w_in_b, ln_g, ln_b, w_s, b_s, *, nb_blk, r_blk, t_chunk, emit_gv,
              v_transposed):
    nbat, r, d = x.shape
    width = 1024
    nr = r // r_blk
    n_tiles = (nbat // nb_blk) * nr
    rows = nb_blk * r_blk
    tokens = nbat * r
    row_idx = lambda i: (i, 0)
    flat = jax.ShapeDtypeStruct((tokens, width), BF16)
    per_head = jax.ShapeDtypeStruct((tokens * N_HEADS, LANES), F32)
    flat_spec = pl.BlockSpec((rows, width), row_idx)
    per_head_spec = pl.BlockSpec((rows * N_HEADS, LANES), row_idx)
    out_shape = [flat, flat, per_head, flat, per_head, flat]
    out_specs = [flat_spec, flat_spec, per_head_spec, flat_spec, per_head_spec, flat_spec]
    if v_transposed:
        assert nb_blk == 1
        out_shape[5] = jax.ShapeDtypeStruct((nbat, width, r), BF16)
        out_specs[5] = pl.BlockSpec((1, width, r_blk), lambda i: (i // nr, 0, i % nr))
    if emit_gv:
        out_shape.append(per_head)
        out_specs.append(per_head_spec)
    ws_t = w_s[:, :t_chunk, :t_chunk]
    bs_t = b_s[:, :t_chunk, None]
    once = pl.Buffered(1)
    return pl.pallas_call(
        functools.partial(_mixer_in_kernel, t_chunk=t_chunk, emit_gv=emit_gv, v_transposed=v_transposed),
        grid=(n_tiles,),
        in_specs=[
            pl.BlockSpec((nb_blk, r_blk, d), lambda i: (i // nr, i % nr, 0)),
            pl.BlockSpec((nb_blk, 1, d), lambda i: (i // nr, 0, 0)),
            pl.BlockSpec((nb_blk, 1, d), lambda i: (i // nr, 0, 0)),
            pl.BlockSpec((1, 1, d), lambda i: (0, 0, 0)),
            pl.BlockSpec(w_in_b.shape, lambda i: (0, 0), pipeline_mode=once),
            pl.BlockSpec((1, width), lambda i: (0, 0)),
            pl.BlockSpec((1, width), lambda i: (0, 0)),
            pl.BlockSpec((MLP_GROUPS, t_chunk, t_chunk), lambda i: (0, 0, 0)),
            pl.BlockSpec((MLP_GROUPS, t_chunk, 1), lambda i: (0, 0, 0)),
        ],
        out_specs=out_specs,
        out_shape=out_shape,
        scratch_shapes=[pltpu.VMEM((rows, width), F32)],
        compiler_params=_params(("arbitrary",)),
        name="mixer_in",
    )(x, shift[:, None, :], scale[:, None, :], g_mix.reshape(1, 1, d), w_in_b,
      ln_g.reshape(1, width), ln_b.reshape(1, width), ws_t, bs_t)


def _split_q(q):
    lane = lax.broadcasted_iota(jnp.int32, q.shape, 1)
    zero = jnp.zeros_like(q)
    return jnp.concatenate([jnp.where(lane < HEAD_DIM, q, zero), jnp.where(lane >= HEAD_DIM, q, zero)], axis=0)


def _softmax_step(s, v, m, l, acc):
    m_new = jnp.maximum(m, jnp.max(s, axis=-1, keepdims=True))
    alpha = jnp.exp2(m - m_new)
    p = jnp.exp2(s - m_new)
    l_new = alpha * l + jnp.sum(p, axis=-1, keepdims=True)
    acc_new = alpha * acc + jnp.dot(p.astype(BF16), v, preferred_element_type=F32)
    return m_new, l_new, acc_new


def _diff_lambda(lam_ref, lam_init):
    lv = lam_ref[...]
    s1 = jnp.sum(lv[0:1] * lv[1:2], axis=-1, keepdims=True)
    s2 = jnp.sum(lv[2:3] * lv[3:4], axis=-1, keepdims=True)
    return jnp.exp(s1) - jnp.exp(s2) + lam_init


def _finish_heads(l, acc, lam, subg, lam_init, n):
    o = acc / l
    a = o[:n] - lam * o[n:]
    a = a * lax.rsqrt(jnp.mean(a * a, axis=-1, keepdims=True) + EPS)
    return a * subg * (1.0 - lam_init)


def _qk(qs, k):
    return lax.dot_general(qs, k, (((1,), (1,)), ((), ())), preferred_element_type=F32)


def _attn_prompt_kernel(q_ref, k_ref, vt_ref, bias_ref, lam_ref, subg_ref, o_ref,
                        p_s, alpha_s, r_s, acc_s, *, bq, bk, lam_init):
    qi = pl.program_id(2)
    halves = bq // ATTN_STRIP
    strips = [(c, h) for c in range(2) for h in range(halves)]
    n_near = bq // bk + 1
    q = q_ref[0]
    lane = lax.broadcasted_iota(jnp.int32, q.shape, 1)
    zero = jnp.zeros_like(q)
    qs = (jnp.where(lane < HEAD_DIM, q, zero), jnp.where(lane >= HEAD_DIM, q, zero))

    ones = jnp.ones((ONES_ROWS, bk), BF16)

    def keys(j):
        return k_ref[0, pl.ds(pl.multiple_of(j * bk, bk), bk), :]

    def values(j):
        vt = vt_ref[0, :, pl.ds(pl.multiple_of(j * bk, bk), bk)]
        return jnp.concatenate([vt, ones], axis=0)

    def logits(k, n, bias):
        c, h = strips[n]
        cols = slice(h * ATTN_STRIP, (h + 1) * ATTN_STRIP)
        s = lax.dot_general(k, qs[c][cols], (((1,), (1,)), ((), ())), preferred_element_type=F32)
        if isinstance(bias, tuple):
            s = s + bias_ref[0, pl.ds(bias[0], bias[1]), cols]
        elif bias is not None:
            s = s + bias
        return s

    def beat(new=None, old=None):
        if new is not None:
            k = keys(new[0])
        if old is not None:
            vt1 = values(old[0])
        for n in range(len(strips)):
            if new is not None and n not in new[3]:
                _, slot, bias, _ = new
                s = logits(k, n, bias)
                r = r_s[n]
                p_s[slot, n] = jnp.exp2(s - r).astype(BF16)
                r_new = jnp.maximum(r, jnp.max(s, axis=0, keepdims=True))
                alpha_s[slot, n] = jnp.exp2(r - r_new)
                r_s[n] = r_new
            if old is not None and n not in old[2]:
                slot = old[1]
                pv = jnp.dot(vt1, p_s[slot, n], preferred_element_type=F32)
                acc_s[n] = (acc_s[n] + pv) * alpha_s[slot, n]

    n_far = jnp.maximum((qi * bq - MAX_DISTANCE) // bk, 0)
    first = jnp.where(qi == 0, 1, 0)
    near_bias = [(pl.multiple_of((i + first) * bk, bk), bk) for i in range(n_near)]

    k0 = k_ref[0, :R_INIT_KEYS, :]
    for n, (_, h) in enumerate(strips):
        bias0 = jnp.where(qi == 0, bias_ref[0, bk:bk + R_INIT_KEYS, h * ATTN_STRIP:(h + 1) * ATTN_STRIP], 0.0)
        r_s[n] = jnp.max(logits(k0, n, bias0), axis=0, keepdims=True)
    acc_s[...] = jnp.zeros(acc_s.shape, F32)
    alpha_s[1] = jnp.ones(alpha_s.shape[1:], F32)
    p_s[1] = jnp.zeros(p_s.shape[1:], BF16)

    n_pairs = n_far // 2

    def pair(j):
        beat(new=(j, 0, None, ()), old=(jnp.maximum(j - 1, 0), 1, ()))
        beat(new=(j + 1, 1, None, ()), old=(j, 0, ()))

    n_quads = n_far // 4

    @pl.loop(0, n_quads)
    def _(t):
        pair(4 * t)
        pair(4 * t + 2)

    @pl.loop(2 * n_quads, n_pairs)
    def _(t):
        pair(2 * t)

    x = 2 * n_pairs
    tail = [(x, jnp.where(n_far % 2 == 1, 0.0, MASK_VALUE), ())]
    for i, bias_i in enumerate(near_bias):
        skip = tuple(n for n, (_, h) in enumerate(strips) if (i - 1) * bk >= (h + 1) * ATTN_STRIP)
        tail.append((n_far + i, bias_i, skip))
    for t, (j, bias_t, skip) in enumerate(tail):
        prev = (jnp.maximum(x - 1, 0), 1, ()) if t == 0 else (tail[t - 1][0], (t - 1) % 2, tail[t - 1][2])
        beat(new=(j, t % 2, bias_t, skip), old=prev)
    beat(old=(tail[-1][0], (len(tail) - 1) % 2, tail[-1][2]))

    lam = _diff_lambda(lam_ref, lam_init)

    def emit_output():
        for h in range(halves):
            o = []
            for c in range(2):
                acc = acc_s[c * halves + h]
                o.append(acc[:V_HEAD_DIM] / acc[V_HEAD_DIM:V_HEAD_DIM + 1])
            a = o[0] - lam * o[1]
            a = a * lax.rsqrt(jnp.mean(a * a, axis=0, keepdims=True) + EPS)
            a = a * subg_ref[...] * (1.0 - lam_init)
            o_ref[0, h * ATTN_STRIP:(h + 1) * ATTN_STRIP, :] = a.T.astype(o_ref.dtype)

    emit_output()

    acc_all = acc_s[...]
    bad = jnp.where(jnp.abs(acc_all) < jnp.inf, 0.0, 1.0)
    bad = jnp.maximum(bad, jnp.where(acc_all[:, V_HEAD_DIM:V_HEAD_DIM + 1] > 0.0, 0.0, 1.0))
    bad = jnp.max(jnp.max(bad, axis=0), axis=0, keepdims=True)
    bad = jnp.max(bad, axis=1, keepdims=True)[0, 0]

    @pl.when(bad > 0.0)
    def _():
        r_s[...] = jnp.full(r_s.shape, MASK_VALUE, F32)
        acc_s[...] = jnp.zeros(acc_s.shape, F32)

        @pl.loop(0, n_far + n_near)
        def _(j):
            start = pl.multiple_of(jnp.maximum(j - n_far + first, 0) * bk, bk)
            k = keys(j)
            vt1 = values(j)
            for n, (_, h) in enumerate(strips):
                tile = bias_ref[0, pl.ds(start, bk), h * ATTN_STRIP:(h + 1) * ATTN_STRIP]
                s = logits(k, n, jnp.where(j < n_far, 0.0, tile))
                m_old = r_s[n]
                m_new = jnp.maximum(m_old, jnp.max(s, axis=0, keepdims=True))
                p = jnp.exp2(s - m_new).astype(BF16)
                acc_s[n] = jnp.exp2(m_old - m_new) * acc_s[n] + jnp.dot(vt1, p, preferred_element_type=F32)
                r_s[n] = m_new

        emit_output()


def _attn_prompt(q, k, vt, bias, lam_vecs, sub_g_col, *, bq, bk, lam_init):
    b, s, _ = q.shape
    n_strips = 2 * (bq // ATTN_STRIP)
    assert bq % bk == 0 and bk >= MAX_DISTANCE and s >= bq + bk
    return pl.pallas_call(
        functools.partial(_attn_prompt_kernel, bq=bq, bk=bk, lam_init=lam_init),
        grid=(b, N_HEADS, s // bq),
        in_specs=[
            pl.BlockSpec((1, bq, LANES), lambda bi, h, qi: (bi, qi, h)),
            pl.BlockSpec((1, s, LANES), lambda bi, h, qi: (bi, 0, h)),
            pl.BlockSpec((1, V_HEAD_DIM, s), lambda bi, h, qi: (bi, h, 0)),
            pl.BlockSpec((1, bq + 2 * bk, bq), lambda bi, h, qi: (h, 0, 0)),
            pl.BlockSpec((4, HEAD_DIM), lambda bi, h, qi: (0, 0)),
            pl.BlockSpec((V_HEAD_DIM, 1), lambda bi, h, qi: (0, 0)),
        ],
        out_specs=pl.BlockSpec((1, bq, LANES), lambda bi, h, qi: (bi, qi, h)),
        out_shape=jax.ShapeDtypeStruct((b, s, N_HEADS * V_HEAD_DIM), BF16),
        scratch_shapes=[
            pltpu.VMEM((2, n_strips, bk, ATTN_STRIP), BF16),
            pltpu.VMEM((2, n_strips, 1, ATTN_STRIP), F32),
            pltpu.VMEM((n_strips, 1, ATTN_STRIP), F32),
            pltpu.VMEM((n_strips, V_HEAD_DIM + ONES_ROWS, ATTN_STRIP), F32),
        ],
        compiler_params=_params(("arbitrary", "arbitrary", "arbitrary")),
        name="attn_prompt",
    )(q, k, vt, bias, lam_vecs, sub_g_col)


def _attn_sample_kernel(q_ref, ck_ref, cv_ref, nk_ref, nv_ref, bias_ref, lam_ref, subg_ref, o_ref,
                        *, past, lam_init):
    t = q_ref.shape[1]
    lam = _diff_lambda(lam_ref, lam_init)
    for hd in range(N_HEADS):
        cs = slice(hd * LANES, (hd + 1) * LANES)
        qs = _split_q(q_ref[0, :, cs])
        bias = bias_ref[hd]
        bias2 = jnp.concatenate([bias, bias], axis=0)
        carry = (jnp.full((2 * t, 1), MASK_VALUE, F32), jnp.zeros((2 * t, 1), F32),
                 jnp.zeros((2 * t, V_HEAD_DIM), F32))
        kc = ck_ref[0, pl.ds(hd, past, stride=N_HEADS), :].astype(BF16)
        vc = cv_ref[0, pl.ds(hd, past, stride=N_HEADS), :].astype(BF16)
        carry = _softmax_step(_qk(qs, kc) + bias2[:, :past], vc, *carry)
        m, l, acc = _softmax_step(_qk(qs, nk_ref[0, :, cs]) + bias2[:, past:], nv_ref[0, :, cs], *carry)
        o_ref[0, :, cs] = _finish_heads(l, acc, lam, subg_ref[...], lam_init, t).astype(o_ref.dtype)


def _attn_sample(q, cache_k, cache_v, new_k, new_v, bias, lam_vecs, sub_g, *, lam_init):
    b, t, width = q.shape
    past = cache_k.shape[1] // N_HEADS
    flat_spec = pl.BlockSpec((1, t, width), lambda bi: (bi, 0, 0))
    cache_spec = pl.BlockSpec((1, past * N_HEADS, LANES), lambda bi: (bi, 0, 0))
    return pl.pallas_call(
        functools.partial(_attn_sample_kernel, past=past, lam_init=lam_init),
        grid=(b,),
        in_specs=[
            flat_spec, cache_spec, cache_spec, flat_spec, flat_spec,
            pl.BlockSpec((N_HEADS, t, past + t), lambda bi: (0, 0, 0)),
            pl.BlockSpec((4, HEAD_DIM), lambda bi: (0, 0)),
            pl.BlockSpec((1, V_HEAD_DIM), lambda bi: (0, 0)),
        ],
        out_specs=flat_spec,
        out_shape=jax.ShapeDtypeStruct((b, t, width), BF16),
        compiler_params=_params(("arbitrary",)),
        name="attn_sample",
    )(q, cache_k, cache_v, new_k, new_v, bias, lam_vecs, sub_g)


def _mixer_out_kernel(x_ref, m_ref, a_ref, gt_ref, w_ref, o_ref):
    half = m_ref.shape[-1]
    y = jnp.dot(m_ref[...], w_ref[:half, :], preferred_element_type=F32)
    y = y + jnp.dot(a_ref[...], w_ref[half:, :], preferred_element_type=F32)
    nb, r, d = x_ref.shape
    o_ref[...] = x_ref[...] + gt_ref[...] * y.reshape(nb, r, d)


def _mixer_out(x, m, a, gate, w_out_b, *, nb_blk, r_blk):
    nbat, r, d = x.shape
    nr = r // r_blk
    rows = nb_blk * r_blk
    half = m.shape[-1]
    return pl.pallas_call(
        _mixer_out_kernel,
        grid=((nbat // nb_blk) * nr,),
        in_specs=[
            pl.BlockSpec((nb_blk, r_blk, d), lambda i: (i // nr, i % nr, 0)),
            pl.BlockSpec((rows, half), lambda i: (i, 0)),
            pl.BlockSpec((rows, half), lambda i: (i, 0)),
            pl.BlockSpec((nb_blk, 1, d), lambda i: (i // nr, 0, 0)),
            pl.BlockSpec((d, d), lambda i: (0, 0)),
        ],
        out_specs=pl.BlockSpec((nb_blk, r_blk, d), lambda i: (i // nr, i % nr, 0)),
        out_shape=jax.ShapeDtypeStruct(x.shape, F32),
        compiler_params=_params(("arbitrary",)),
        name="mixer_out",
    )(x, m, a, gate[:, None, :], w_out_b)


def _row_subtiles(nb, r, count):
    if nb >= count:
        per = nb // count
        return [(slice(s * per, (s + 1) * per), slice(None), slice(s * per * r, (s + 1) * per * r))
                for s in range(count)]
    per = r // count
    return [(slice(None), slice(s * per, (s + 1) * per), slice(s * per, (s + 1) * per)) for s in range(count)]


def _ffn_kernel(x_ref, sh_ref, sc_ref, gt_ref, g_ref, wg_ref, wu_ref, wo_ref, gf_ref, shf_ref, scf_ref,
                o_ref, h_s):
    f = pl.program_id(1)
    last = pl.num_programs(1) - 1
    nb, r, d = x_ref.shape
    subs = _row_subtiles(nb, r, FFN_ROW_GROUPS)

    def normalize(sub):
        bsl, rsl, rows = sub
        x = x_ref[bsl, rsl, :]
        y = x * lax.rsqrt(jnp.mean(x * x, axis=-1, keepdims=True) + EPS)
        hm = (y * g_ref[...]) * (1.0 + sc_ref[bsl]) + sh_ref[bsl]
        h_s[rows, :] = hm.reshape(-1, d).astype(BF16)

    def swiglu(rows, shape):
        h = h_s[rows, :]
        zg = jnp.dot(h, wg_ref[...], preferred_element_type=F32)
        zu = jnp.dot(h, wu_ref[...], preferred_element_type=F32)
        act = (zg * jax.nn.sigmoid(zg) * zu).astype(BF16)
        return jnp.dot(act, wo_ref[...], preferred_element_type=F32).reshape(shape)

    def finish(sub):
        bsl, rsl, rows = sub
        x = x_ref[bsl, rsl, :]
        x2 = x + gt_ref[bsl] * (o_ref[bsl, rsl, :] + swiglu(rows, x.shape))
        y = x2 * lax.rsqrt(jnp.mean(x2 * x2, axis=-1, keepdims=True) + EPS)
        o_ref[bsl, rsl, :] = (y * gf_ref[...]) * (1.0 + scf_ref[bsl]) + shf_ref[bsl]

    @pl.when(f == 0)
    def _():
        for bsl, rsl, rows in subs:
            normalize((bsl, rsl, rows))
            o_ref[bsl, rsl, :] = swiglu(rows, o_ref[bsl, rsl, :].shape)

    @pl.when(jnp.logical_and(f > 0, f < last))
    def _():
        o_ref[...] += swiglu(slice(None), o_ref.shape)

    @pl.when(f == last)
    def _():
        for sub in subs:
            finish(sub)


def _ffn(x, shift, scale, gate, g_ffn, w_in_b, w_out_b, g_final, shift_f, scale_f, *, nb_blk, r_blk, tf):
    nbat, r, d = x.shape
    d_ff = w_out_b.shape[0]
    nf = d_ff // tf
    assert nf >= 2, "first and last d_ff steps are distinct code paths"
    nr = r // r_blk
    rows = nb_blk * r_blk
    x_spec = pl.BlockSpec((nb_blk, r_blk, d), lambda i, f: (i // nr, i % nr, 0))
    vec_spec = pl.BlockSpec((nb_blk, 1, d), lambda i, f: (i // nr, 0, 0))
    par_spec = pl.BlockSpec((1, 1, d), lambda i, f: (0, 0, 0))
    return pl.pallas_call(
        _ffn_kernel,
        grid=((nbat // nb_blk) * nr, nf),
        in_specs=[
            x_spec, vec_spec, vec_spec, vec_spec, par_spec,
            pl.BlockSpec((d, tf), lambda i, f: (0, f)),
            pl.BlockSpec((d, tf), lambda i, f: (0, f + nf)),
            pl.BlockSpec((tf, d), lambda i, f: (f, 0)),
            par_spec, vec_spec, vec_spec,
        ],
        out_specs=x_spec,
        out_shape=jax.ShapeDtypeStruct(x.shape, F32),
        scratch_shapes=[pltpu.VMEM((rows, d), BF16)],
        compiler_params=_params(("arbitrary", "arbitrary")),
        name="ffn",
    )(x, shift[:, None, :], scale[:, None, :], gate[:, None, :], g_ffn.reshape(1, 1, d),
      w_in_b, w_in_b, w_out_b, g_final.reshape(1, 1, d), shift_f[:, None, :], scale_f[:, None, :])


ATTN_Q_BLOCK = 1024
ATTN_K_BEAT = 512
PROMPT_ROWS = 512
SAMPLE_BATCH_BLOCK = 8
FFN_TILE = 512
FFN_ROWS = 1024
FFN_ROW_GROUPS = 2


def kernel(x_prompt, x_sample, cache_k, cache_v, c_prompt, c_sample, rel_bias, w_ada, b_ada, w_ada_final,
           b_ada_final, g_mix, g_ffn, g_final, w_in, mlp_ln_g, mlp_ln_b, w_s, b_s, lambda_q1, lambda_k1,
           lambda_q2, lambda_k2, sub_g, w_out, w_ffn_in, w_ffn_out):
    B, S, D = x_prompt.shape
    DB, T, _ = x_sample.shape
    depth = w_in.shape[0]
    past = cache_k.shape[2]
    width = N_HEADS * V_HEAD_DIM
    mlp_chunk = w_s.shape[-1]

    c_all = jnp.concatenate([c_prompt, c_sample], axis=0)
    mod_f = _adaln(c_all, w_ada_final, b_ada_final)
    bias_p = _rel_bias_tiles(rel_bias, ATTN_Q_BLOCK, ATTN_Q_BLOCK + 2 * ATTN_K_BEAT, ATTN_K_BEAT, 0, True, True)
    bias_s = _rel_bias_tiles(rel_bias, T, past + T, past, 0, False, False)

    assert depth == 1, "the final adaLN norm is fused into the single layer's FFN kernel"
    lam_init = 0.8 - 0.6 * math.exp(-0.3 * 0)
    lam_vecs = jnp.stack([lambda_q1[0], lambda_k1[0], lambda_q2[0], lambda_k2[0]])
    subg = sub_g.reshape(1, V_HEAD_DIM)
    mod = _adaln(c_all, w_ada[0], b_ada[0])
    sh1, sc1, gt1, sh2, sc2, gt2 = jnp.split(mod, 6, axis=-1)
    shf, scf = jnp.split(mod_f, 2, axis=-1)
    w_in_b = w_in[0].astype(BF16)
    w_out_b = w_out[0].astype(BF16)
    w_f_in_b = w_ffn_in[0].astype(BF16)
    w_f_out_b = w_ffn_out[0].astype(BF16)
    mixer_w = (g_mix[0], w_in_b, mlp_ln_g[0], mlp_ln_b[0], w_s[0], b_s[0])

    m, q, kp, kb, vp, vt = _mixer_in(x_prompt, sh1[:B], sc1[:B], *mixer_w, nb_blk=1, r_blk=PROMPT_ROWS,
                                     t_chunk=mlp_chunk, emit_gv=False, v_transposed=True)
    a = _attn_prompt(q.reshape(B, S, width), kb.reshape(B, S, width), vt, bias_p, lam_vecs,
                     subg.reshape(V_HEAD_DIM, 1), bq=ATTN_Q_BLOCK, bk=ATTN_K_BEAT, lam_init=lam_init)
    xp = _mixer_out(x_prompt, m, a.reshape(B * S, width), gt1[:B], w_out_b, nb_blk=1, r_blk=PROMPT_ROWS)
    yp = _ffn(xp, sh2[:B], sc2[:B], gt2[:B], g_ffn[0], w_f_in_b, w_f_out_b, g_final, shf[:B], scf[:B],
              nb_blk=1, r_blk=FFN_ROWS, tf=FFN_TILE)

    m, q, ks, kb, vs, vb, gvs = _mixer_in(x_sample, sh1[B:], sc1[B:], *mixer_w, nb_blk=SAMPLE_BATCH_BLOCK,
                                          r_blk=T, t_chunk=T, emit_gv=True, v_transposed=False)
    a = _attn_sample(q.reshape(DB, T, width), cache_k.reshape(DB, past * N_HEADS, LANES),
                     cache_v.reshape(DB, past * N_HEADS, LANES), kb.reshape(DB, T, width),
                     vb.reshape(DB, T, width), bias_s, lam_vecs, subg, lam_init=lam_init)
    xs = _mixer_out(x_sample, m, a.reshape(DB * T, width), gt1[B:], w_out_b, nb_blk=SAMPLE_BATCH_BLOCK, r_blk=T)
    ys = _ffn(xs, sh2[B:], sc2[B:], gt2[B:], g_ffn[0], w_f_in_b, w_f_out_b, g_final, shf[B:], scf[B:],
              nb_blk=SAMPLE_BATCH_BLOCK, r_blk=T, tf=FFN_TILE)

    head_shape = (N_HEADS, V_HEAD_DIM)
    return (yp, ys, kp.reshape(1, B, S, *head_shape), vp.reshape(1, B, S, *head_shape),
            ks.reshape(1, DB, T, *head_shape), vs.reshape(1, DB, T, *head_shape),
            gvs.reshape(1, DB, T, MLP_GROUPS, MLP_GROUP_DIM))
```

```python
import functools
import math

import jax
import jax.numpy as jnp
from jax import lax
from jax.experimental import pallas as pl
from jax.experimental.pallas import tpu as pltpu

LANES = 128
SUBLANES = 8
VMEM_LIMIT_BYTES = 56 * 1024 * 1024
MXU_WIDTH = 256

ATTN_STRIP = MXU_WIDTH
ONES_ROWS = SUBLANES
R_INIT_KEYS = 128

CHUNK = 64
N_HEADS = 8
HEAD_DIM = 64
V_HEAD_DIM = 128
MLP_GROUPS = 8
MLP_GROUP_DIM = 128
N_BUCKETS = 32
MAX_DISTANCE = 128
EPS = 1e-6
MASK_VALUE = -1e30
LOG2E = math.log2(math.e)

BF16 = jnp.bfloat16
F32 = jnp.float32


def _params(sem):
    return pltpu.CompilerParams(dimension_semantics=sem, vmem_limit_bytes=VMEM_LIMIT_BYTES)


def _adaln_kernel(c_ref, w_ref, b_ref, o_ref):
    c = c_ref[...]
    a = c * jax.nn.sigmoid(c)
    o_ref[...] = jnp.dot(a, w_ref[...], preferred_element_type=F32) + b_ref[...]


def _adaln(c, w, b, tn=1024):
    rows, d = c.shape
    n = w.shape[1]
    return pl.pallas_call(
        _adaln_kernel,
        grid=(n // tn,),
        in_specs=[
            pl.BlockSpec((rows, d), lambda j: (0, 0)),
            pl.BlockSpec((d, tn), lambda j: (0, j)),
            pl.BlockSpec((1, tn), lambda j: (0, j)),
        ],
        out_specs=pl.BlockSpec((rows, tn), lambda j: (0, j)),
        out_shape=jax.ShapeDtypeStruct((rows, n), F32),
        compiler_params=_params(("arbitrary",)),
        name="adaln",
    )(c, w, b.reshape(1, n))


def _rel_bias_kernel(tab_ref, o_ref, *, nq, nk, q_start, k_start, shift_far, keys_on_rows):
    h = pl.program_id(0)
    nb = N_BUCKETS // 2
    max_exact = nb // 2
    q_axis, k_axis = (1, 0) if keys_on_rows else (0, 1)
    rows, cols = (nk, nq) if keys_on_rows else (nq, nk)
    shift = tab_ref[nb - 1, h] if shift_far else 0.0

    def block(shape, q0, k0):
        q_pos = q0 + lax.broadcasted_iota(jnp.int32, shape, q_axis)
        k_pos = k0 + lax.broadcasted_iota(jnp.int32, shape, k_axis)
        rel = k_pos - q_pos
        ret = jnp.where(rel > 0, nb, 0)
        n = jnp.abs(rel)
        nf = jnp.maximum(n, 1).astype(F32)
        large = max_exact + (jnp.log(nf / max_exact) / math.log(MAX_DISTANCE / max_exact)
                             * (nb - max_exact)).astype(jnp.int32)
        large = jnp.minimum(large, nb - 1)
        bucket = ret + jnp.where(n < max_exact, n, large)
        bias = jnp.zeros(shape, F32)
        for bkt in range(N_BUCKETS):
            bias = jnp.where(bucket == bkt, tab_ref[bkt, h], bias)
        allowed = (k_pos // CHUNK) <= (q_pos // CHUNK)
        return jnp.where(allowed, (bias - shift) * LOG2E, MASK_VALUE)

    if rows % LANES or cols % LANES:
        o_ref[0] = block((rows, cols), q_start, k_start)
        return
    far_value = (tab_ref[nb - 1, h] - shift) * LOG2E
    for rb in range(rows // LANES):
        for cb in range(cols // LANES):
            k0 = k_start + LANES * (rb if keys_on_rows else cb)
            q0 = q_start + LANES * (cb if keys_on_rows else rb)
            sl = (0, slice(rb * LANES, (rb + 1) * LANES), slice(cb * LANES, (cb + 1) * LANES))
            if (k0 - q0) + (LANES - 1) <= -MAX_DISTANCE:
                o_ref[sl] = jnp.full((LANES, LANES), far_value, F32)
            elif (k0 - q0) - (LANES - 1) >= CHUNK:
                o_ref[sl] = jnp.full((LANES, LANES), MASK_VALUE, F32)
            else:
                o_ref[sl] = block((LANES, LANES), q0, k0)


def _rel_bias_tiles(rel_bias, nq, nk, q_start, k_start, shift_far, keys_on_rows):
    out_tile = (nk, nq) if keys_on_rows else (nq, nk)
    return pl.pallas_call(
        functools.partial(_rel_bias_kernel, nq=nq, nk=nk, q_start=q_start, k_start=k_start,
                          shift_far=shift_far, keys_on_rows=keys_on_rows),
        grid=(N_HEADS,),
        in_specs=[pl.BlockSpec(memory_space=pltpu.SMEM)],
        out_specs=pl.BlockSpec((1,) + out_tile, lambda h: (h, 0, 0)),
        out_shape=jax.ShapeDtypeStruct((N_HEADS,) + out_tile, F32),
        compiler_params=_params(("arbitrary",)),
        name="rel_bias",
    )(rel_bias)


def _mixer_in_kernel(x_ref, sh_ref, sc_ref, g_ref, w_ref, lng_ref, lnb_ref, ws_ref, bs_ref,
                     *refs, t_chunk, emit_gv, v_transposed):
    if emit_gv:
        m_ref, q_ref, k_ref, kb_ref, v_ref, vb_ref, gv_ref, u_s = refs
    else:
        m_ref, q_ref, k_ref, kb_ref, v_ref, vb_ref, u_s = refs
        gv_ref = None
    rows = u_s.shape[0]
    width = u_s.shape[1]

    def store_per_head(ref, val):
        for hd in range(N_HEADS):
            ref[pl.ds(hd, rows, stride=N_HEADS), :] = val[:, hd * LANES:(hd + 1) * LANES]

    x = x_ref[...]
    y = x * lax.rsqrt(jnp.mean(x * x, axis=-1, keepdims=True) + EPS)
    y = y * g_ref[...]
    hm = y * (1.0 + sc_ref[...]) + sh_ref[...]
    h = hm.reshape(rows, hm.shape[-1]).astype(BF16)

    def project(seg):
        return jnp.dot(h, w_ref[:, seg * width:(seg + 1) * width], preferred_element_type=F32)

    u_s[...] = jax.nn.gelu(project(0))

    g = jax.nn.gelu(project(1))
    mu = jnp.mean(g, axis=-1, keepdims=True)
    var = jnp.mean(jnp.square(g - mu), axis=-1, keepdims=True)
    gv = (g - mu) * lax.rsqrt(var + EPS) * lng_ref[...] + lnb_ref[...]
    if emit_gv:
        store_per_head(gv_ref, gv)
    gvb = gv.astype(BF16)
    ii = lax.broadcasted_iota(jnp.int32, (t_chunk, t_chunk), 0)
    jj = lax.broadcasted_iota(jnp.int32, (t_chunk, t_chunk), 1)
    mask = (jj // CHUNK) <= (ii // CHUNK)
    for grp in range(MLP_GROUPS):
        wg = jnp.where(mask, ws_ref[grp], 0.0).astype(BF16)
        bg = bs_ref[grp]
        cs = slice(grp * MLP_GROUP_DIM, (grp + 1) * MLP_GROUP_DIM)
        for c in range(rows // t_chunk):
            rs = slice(c * t_chunk, (c + 1) * t_chunk)
            mixed = jnp.dot(wg, gvb[rs, cs], preferred_element_type=F32) + bg
            m_ref[rs, cs] = (u_s[rs, cs] * mixed).astype(BF16)

    z = project(2) * (HEAD_DIM ** -0.5 * LOG2E)
    if v_transposed:
        q_ref[0] = z.T.astype(BF16)
    else:
        q_ref[...] = z.astype(BF16)

    z = project(3)
    store_per_head(k_ref, z)
    kb_ref[...] = z.astype(BF16)

    z = project(4)
    store_per_head(v_ref, z)
    if v_transposed:
        vb_ref[0] = z.T.astype(BF16)
    else:
        vb_ref[...] = z.astype(BF16)


def _mixer_in(x, shift, scale, g_mix, w_in_b, ln_g, ln_b, w_s, b_s, *, nb_blk, r_blk, t_chunk, emit_gv,
              v_transposed):
    nbat, r, d = x.shape
    width = 1024
    nr = r // r_blk
    n_tiles = (nbat // nb_blk) * nr
    rows = nb_blk * r_blk
    tokens = nbat * r
    row_idx = lambda i: (i, 0)
    flat = jax.ShapeDtypeStruct((tokens, width), BF16)
    per_head = jax.ShapeDtypeStruct((tokens * N_HEADS, LANES), F32)
    flat_spec = pl.BlockSpec((rows, width), row_idx)
    per_head_spec = pl.BlockSpec((rows * N_HEADS, LANES), row_idx)
    out_shape = [flat, flat, per_head, flat, per_head, flat]
    out_specs = [flat_spec, flat_spec, per_head_spec, flat_spec, per_head_spec, flat_spec]
    if v_transposed:
        assert nb_blk == 1
        for idx in (1, 5):
            out_shape[idx] = jax.ShapeDtypeStruct((nbat, width, r), BF16)
            out_specs[idx] = pl.BlockSpec((1, width, r_blk), lambda i: (i // nr, 0, i % nr))
    if emit_gv:
        out_shape.append(per_head)
        out_specs.append(per_head_spec)
    ws_t = w_s[:, :t_chunk, :t_chunk]
    bs_t = b_s[:, :t_chunk, None]
    once = pl.Buffered(1)
    return pl.pallas_call(
        functools.partial(_mixer_in_kernel, t_chunk=t_chunk, emit_gv=emit_gv, v_transposed=v_transposed),
        grid=(n_tiles,),
        in_specs=[
            pl.BlockSpec((nb_blk, r_blk, d), lambda i: (i // nr, i % nr, 0)),
            pl.BlockSpec((nb_blk, 1, d), lambda i: (i // nr, 0, 0)),
            pl.BlockSpec((nb_blk, 1, d), lambda i: (i // nr, 0, 0)),
            pl.BlockSpec((1, 1, d), lambda i: (0, 0, 0)),
            pl.BlockSpec(w_in_b.shape, lambda i: (0, 0), pipeline_mode=once),
            pl.BlockSpec((1, width), lambda i: (0, 0)),
            pl.BlockSpec((1, width), lambda i: (0, 0)),
            pl.BlockSpec((MLP_GROUPS, t_chunk, t_chunk), lambda i: (0, 0, 0)),
            pl.BlockSpec((MLP_GROUPS, t_chunk, 1), lambda i: (0, 0, 0)),
        ],
        out_specs=out_specs,
        out_shape=out_shape,
        scratch_shapes=[pltpu.VMEM((rows, width), F32)],
        compiler_params=_params(("arbitrary",)),
        name="mixer_in",
    )(x, shift[:, None, :], scale[:, None, :], g_mix.reshape(1, 1, d), w_in_b,
      ln_g.reshape(1, width), ln_b.reshape(1, width), ws_t, bs_t)


def _split_q(q):
    lane = lax.broadcasted_iota(jnp.int32, q.shape, 1)
    zero = jnp.zeros_like(q)
    return jnp.concatenate([jnp.where(lane < HEAD_DIM, q, zero), jnp.where(lane >= HEAD_DIM, q, zero)], axis=0)


def _softmax_step(s, v, m, l, acc):
    m_new = jnp.maximum(m, jnp.max(s, axis=-1, keepdims=True))
    alpha = jnp.exp2(m - m_new)
    p = jnp.exp2(s - m_new)
    l_new = alpha * l + jnp.sum(p, axis=-1, keepdims=True)
    acc_new = alpha * acc + jnp.dot(p.astype(BF16), v, preferred_element_type=F32)
    return m_new, l_new, acc_new


def _diff_lambda(lam_ref, lam_init):
    lv = lam_ref[...]
    s1 = jnp.sum(lv[0:1] * lv[1:2], axis=-1, keepdims=True)
    s2 = jnp.sum(lv[2:3] * lv[3:4], axis=-1, keepdims=True)
    return jnp.exp(s1) - jnp.exp(s2) + lam_init


def _finish_heads(l, acc, lam, subg, lam_init, n):
    o = acc / l
    a = o[:n] - lam * o[n:]
    a = a * lax.rsqrt(jnp.mean(a * a, axis=-1, keepdims=True) + EPS)
    return a * subg * (1.0 - lam_init)


def _qk(qs, k):
    return lax.dot_general(qs, k, (((1,), (1,)), ((), ())), preferred_element_type=F32)


def _attn_prompt_kernel(qt_ref, k_ref, vt_ref, bias_ref, lam_ref, subg_ref, o_ref,
                        p_s, alpha_s, r_s, acc_s, *, bq, bk, lam_init):
    qi = pl.program_id(2)
    halves = bq // ATTN_STRIP
    strips = [(c, h) for c in range(2) for h in range(halves)]
    n_near = bq // bk + 1
    qt = qt_ref[0]
    row = lax.broadcasted_iota(jnp.int32, qt.shape, 0)
    zero = jnp.zeros_like(qt)
    qs = (jnp.where(row < HEAD_DIM, qt, zero), jnp.where(row >= HEAD_DIM, qt, zero))

    ones = jnp.ones((ONES_ROWS, bk), BF16)

    def keys(j):
        return k_ref[0, pl.ds(pl.multiple_of(j * bk, bk), bk), :]

    def values(j):
        vt = vt_ref[0, :, pl.ds(pl.multiple_of(j * bk, bk), bk)]
        return jnp.concatenate([vt, ones], axis=0)

    def logits(k, n, bias):
        c, h = strips[n]
        cols = slice(h * ATTN_STRIP, (h + 1) * ATTN_STRIP)
        s = jnp.dot(k, qs[c][:, cols], preferred_element_type=F32)
        if isinstance(bias, tuple):
            s = s + bias_ref[0, pl.ds(bias[0], bias[1]), cols]
        elif bias is not None:
            s = s + bias
        return s

    def beat(new=None, old=None):
        if new is not None:
            k = keys(new[0])
        if old is not None:
            vt1 = values(old[0])
        for n in range(len(strips)):
            if new is not None and n not in new[3]:
                _, slot, bias, _ = new
                s = logits(k, n, bias)
                r = r_s[n]
                p_s[slot, n] = jnp.exp2(s - r).astype(BF16)
                r_new = jnp.maximum(r, jnp.max(s, axis=0, keepdims=True))
                alpha_s[slot, n] = jnp.exp2(r - r_new)
                r_s[n] = r_new
            if old is not None and n not in old[2]:
                slot = old[1]
                pv = jnp.dot(vt1, p_s[slot, n], preferred_element_type=F32)
                acc_s[n] = (acc_s[n] + pv) * alpha_s[slot, n]

    n_far = jnp.maximum((qi * bq - MAX_DISTANCE) // bk, 0)
    first = jnp.where(qi == 0, 1, 0)
    near_bias = [(pl.multiple_of((i + first) * bk, bk), bk) for i in range(n_near)]

    k0 = k_ref[0, :R_INIT_KEYS, :]
    for n, (_, h) in enumerate(strips):
        bias0 = jnp.where(qi == 0, bias_ref[0, bk:bk + R_INIT_KEYS, h * ATTN_STRIP:(h + 1) * ATTN_STRIP], 0.0)
        r_s[n] = jnp.max(logits(k0, n, bias0), axis=0, keepdims=True)
    acc_s[...] = jnp.zeros(acc_s.shape, F32)
    alpha_s[1] = jnp.ones(alpha_s.shape[1:], F32)
    p_s[1] = jnp.zeros(p_s.shape[1:], BF16)

    n_pairs = n_far // 2

    def pair(j):
        beat(new=(j, 0, None, ()), old=(jnp.maximum(j - 1, 0), 1, ()))
        beat(new=(j + 1, 1, None, ()), old=(j, 0, ()))

    n_quads = n_far // 4

    @pl.loop(0, n_quads)
    def _(t):
        pair(4 * t)
        pair(4 * t + 2)

    @pl.loop(2 * n_quads, n_pairs)
    def _(t):
        pair(2 * t)

    x = 2 * n_pairs
    tail = [(x, jnp.where(n_far % 2 == 1, 0.0, MASK_VALUE), ())]
    for i, bias_i in enumerate(near_bias):
        skip = tuple(n for n, (_, h) in enumerate(strips) if (i - 1) * bk >= (h + 1) * ATTN_STRIP)
        tail.append((n_far + i, bias_i, skip))
    for t, (j, bias_t, skip) in enumerate(tail):
        prev = (jnp.maximum(x - 1, 0), 1, ()) if t == 0 else (tail[t - 1][0], (t - 1) % 2, tail[t - 1][2])
        beat(new=(j, t % 2, bias_t, skip), old=prev)
    beat(old=(tail[-1][0], (len(tail) - 1) % 2, tail[-1][2]))

    lam = _diff_lambda(lam_ref, lam_init)

    def emit_output():
        for h in range(halves):
            o = []
            for c in range(2):
                acc = acc_s[c * halves + h]
                o.append(acc[:V_HEAD_DIM] / acc[V_HEAD_DIM:V_HEAD_DIM + 1])
            a = o[0] - lam * o[1]
            a = a * lax.rsqrt(jnp.mean(a * a, axis=0, keepdims=True) + EPS)
            a = a * subg_ref[...] * (1.0 - lam_init)
            o_ref[0, h * ATTN_STRIP:(h + 1) * ATTN_STRIP, :] = a.T.astype(o_ref.dtype)

    emit_output()

    acc_all = acc_s[...]
    bad = jnp.where(jnp.abs(acc_all) < jnp.inf, 0.0, 1.0)
    bad = jnp.maximum(bad, jnp.where(acc_all[:, V_HEAD_DIM:V_HEAD_DIM + 1] > 0.0, 0.0, 1.0))
    bad = jnp.max(jnp.max(bad, axis=0), axis=0, keepdims=True)
    bad = jnp.max(bad, axis=1, keepdims=True)[0, 0]

    @pl.when(bad > 0.0)
    def _():
        r_s[...] = jnp.full(r_s.shape, MASK_VALUE, F32)
        acc_s[...] = jnp.zeros(acc_s.shape, F32)

        @pl.loop(0, n_far + n_near)
        def _(j):
            start = pl.multiple_of(jnp.maximum(j - n_far + first, 0) * bk, bk)
            k = keys(j)
            vt1 = values(j)
            for n, (_, h) in enumerate(strips):
                tile = bias_ref[0, pl.ds(start, bk), h * ATTN_STRIP:(h + 1) * ATTN_STRIP]
                s = logits(k, n, jnp.where(j < n_far, 0.0, tile))
                m_old = r_s[n]
                m_new = jnp.maximum(m_old, jnp.max(s, axis=0, keepdims=True))
                p = jnp.exp2(s - m_new).astype(BF16)
                acc_s[n] = jnp.exp2(m_old - m_new) * acc_s[n] + jnp.dot(vt1, p, preferred_element_type=F32)
                r_s[n] = m_new

        emit_output()


def _attn_prompt(qt, k, vt, bias, lam_vecs, sub_g_col, *, bq, bk, lam_init):
    b, s, _ = k.shape
    n_strips = 2 * (bq // ATTN_STRIP)
    assert bq % bk == 0 and bk >= MAX_DISTANCE and s >= bq + bk
    return pl.pallas_call(
        functools.partial(_attn_prompt_kernel, bq=bq, bk=bk, lam_init=lam_init),
        grid=(b, N_HEADS, s // bq),
        in_specs=[
            pl.BlockSpec((1, LANES, bq), lambda bi, h, qi: (bi, h, qi)),
            pl.BlockSpec((1, s, LANES), lambda bi, h, qi: (bi, 0, h)),
            pl.BlockSpec((1, V_HEAD_DIM, s), lambda bi, h, qi: (bi, h, 0)),
            pl.BlockSpec((1, bq + 2 * bk, bq), lambda bi, h, qi: (h, 0, 0)),
            pl.BlockSpec((4, HEAD_DIM), lambda bi, h, qi: (0, 0)),
            pl.BlockSpec((V_HEAD_DIM, 1), lambda bi, h, qi: (0, 0)),
        ],
        out_specs=pl.BlockSpec((1, bq, LANES), lambda bi, h, qi: (bi, qi, h)),
        out_shape=jax.ShapeDtypeStruct((b, s, N_HEADS * V_HEAD_DIM), BF16),
        scratch_shapes=[
            pltpu.VMEM((2, n_strips, bk, ATTN_STRIP), BF16),
            pltpu.VMEM((2, n_strips, 1, ATTN_STRIP), F32),
            pltpu.VMEM((n_strips, 1, ATTN_STRIP), F32),
            pltpu.VMEM((n_strips, V_HEAD_DIM + ONES_ROWS, ATTN_STRIP), F32),
        ],
        compiler_params=_params(("arbitrary", "arbitrary", "arbitrary")),
        name="attn_prompt",
    )(qt, k, vt, bias, lam_vecs, sub_g_col)


def _attn_sample_kernel(q_ref, ck_ref, cv_ref, nk_ref, nv_ref, bias_ref, lam_ref, subg_ref, o_ref,
                        *, past, lam_init):
    t = q_ref.shape[1]
    lam = _diff_lambda(lam_ref, lam_init)
    for hd in range(N_HEADS):
        cs = slice(hd * LANES, (hd + 1) * LANES)
        qs = _split_q(q_ref[0, :, cs])
        bias = bias_ref[hd]
        bias2 = jnp.concatenate([bias, bias], axis=0)
        carry = (jnp.full((2 * t, 1), MASK_VALUE, F32), jnp.zeros((2 * t, 1), F32),
                 jnp.zeros((2 * t, V_HEAD_DIM), F32))
        kc = ck_ref[0, pl.ds(hd, past, stride=N_HEADS), :].astype(BF16)
        vc = cv_ref[0, pl.ds(hd, past, stride=N_HEADS), :].astype(BF16)
        carry = _softmax_step(_qk(qs, kc) + bias2[:, :past], vc, *carry)
        m, l, acc = _softmax_step(_qk(qs, nk_ref[0, :, cs]) + bias2[:, past:], nv_ref[0, :, cs], *carry)
        o_ref[0, :, cs] = _finish_heads(l, acc, lam, subg_ref[...], lam_init, t).astype(o_ref.dtype)


def _attn_sample(q, cache_k, cache_v, new_k, new_v, bias, lam_vecs, sub_g, *, lam_init):
    b, t, width = q.shape
    past = cache_k.shape[1] // N_HEADS
    flat_spec = pl.BlockSpec((1, t, width), lambda bi: (bi, 0, 0))
    cache_spec = pl.BlockSpec((1, past * N_HEADS, LANES), lambda bi: (bi, 0, 0))
    return pl.pallas_call(
        functools.partial(_attn_sample_kernel, past=past, lam_init=lam_init),
        grid=(b,),
        in_specs=[
            flat_spec, cache_spec, cache_spec, flat_spec, flat_spec,
            pl.BlockSpec((N_HEADS, t, past + t), lambda bi: (0, 0, 0)),
            pl.BlockSpec((4, HEAD_DIM), lambda bi: (0, 0)),
            pl.BlockSpec((1, V_HEAD_DIM), lambda bi: (0, 0)),
        ],
        out_specs=flat_spec,
        out_shape=jax.ShapeDtypeStruct((b, t, width), BF16),
        compiler_params=_params(("arbitrary",)),
        name="attn_sample",
    )(q, cache_k, cache_v, new_k, new_v, bias, lam_vecs, sub_g)


def _mixer_out_kernel(x_ref, m_ref, a_ref, gt_ref, w_ref, o_ref):
    half = m_ref.shape[-1]
    y = jnp.dot(m_ref[...], w_ref[:half, :], preferred_element_type=F32)
    y = y + jnp.dot(a_ref[...], w_ref[half:, :], preferred_element_type=F32)
    nb, r, d = x_ref.shape
    o_ref[...] = x_ref[...] + gt_ref[...] * y.reshape(nb, r, d)


def _mixer_out(x, m, a, gate, w_out_b, *, nb_blk, r_blk):
    nbat, r, d = x.shape
    nr = r // r_blk
    rows = nb_blk * r_blk
    half = m.shape[-1]
    return pl.pallas_call(
        _mixer_out_kernel,
        grid=((nbat // nb_blk) * nr,),
        in_specs=[
            pl.BlockSpec((nb_blk, r_blk, d), lambda i: (i // nr, i % nr, 0)),
            pl.BlockSpec((rows, half), lambda i: (i, 0)),
            pl.BlockSpec((rows, half), lambda i: (i, 0)),
            pl.BlockSpec((nb_blk, 1, d), lambda i: (i // nr, 0, 0)),
            pl.BlockSpec((d, d), lambda i: (0, 0)),
        ],
        out_specs=pl.BlockSpec((nb_blk, r_blk, d), lambda i: (i // nr, i % nr, 0)),
        out_shape=jax.ShapeDtypeStruct(x.shape, F32),
        compiler_params=_params(("arbitrary",)),
        name="mixer_out",
    )(x, m, a, gate[:, None, :], w_out_b)


def _row_subtiles(nb, r, count):
    if nb >= count:
        per = nb // count
        return [(slice(s * per, (s + 1) * per), slice(None), slice(s * per * r, (s + 1) * per * r))
                for s in range(count)]
    per = r // count
    return [(slice(None), slice(s * per, (s + 1) * per), slice(s * per, (s + 1) * per)) for s in range(count)]


def _ffn_kernel(x_ref, sh_ref, sc_ref, gt_ref, g_ref, wg_ref, wu_ref, wo_ref, gf_ref, shf_ref, scf_ref,
                o_ref, h_s):
    f = pl.program_id(1)
    last = pl.num_programs(1) - 1
    nb, r, d = x_ref.shape
    subs = _row_subtiles(nb, r, FFN_ROW_GROUPS)

    def normalize(sub):
        bsl, rsl, rows = sub
        x = x_ref[bsl, rsl, :]
        y = x * lax.rsqrt(jnp.mean(x * x, axis=-1, keepdims=True) + EPS)
        hm = (y * g_ref[...]) * (1.0 + sc_ref[bsl]) + sh_ref[bsl]
        h_s[rows, :] = hm.reshape(-1, d).astype(BF16)

    def swiglu(rows, shape):
        h = h_s[rows, :]
        zg = jnp.dot(h, wg_ref[...], preferred_element_type=F32)
        zu = jnp.dot(h, wu_ref[...], preferred_element_type=F32)
        act = (zg * jax.nn.sigmoid(zg) * zu).astype(BF16)
        return jnp.dot(act, wo_ref[...], preferred_element_type=F32).reshape(shape)

    def finish(sub):
        bsl, rsl, rows = sub
        x = x_ref[bsl, rsl, :]
        x2 = x + gt_ref[bsl] * (o_ref[bsl, rsl, :] + swiglu(rows, x.shape))
        y = x2 * lax.rsqrt(jnp.mean(x2 * x2, axis=-1, keepdims=True) + EPS)
        o_ref[bsl, rsl, :] = (y * gf_ref[...]) * (1.0 + scf_ref[bsl]) + shf_ref[bsl]

    @pl.when(f == 0)
    def _():
        for bsl, rsl, rows in subs:
            normalize((bsl, rsl, rows))
            o_ref[bsl, rsl, :] = swiglu(rows, o_ref[bsl, rsl, :].shape)

    @pl.when(jnp.logical_and(f > 0, f < last))
    def _():
        o_ref[...] += swiglu(slice(None), o_ref.shape)

    @pl.when(f == last)
    def _():
        for sub in subs:
            finish(sub)


def _ffn(x, shift, scale, gate, g_ffn, w_in_b, w_out_b, g_final, shift_f, scale_f, *, nb_blk, r_blk, tf):
    nbat, r, d = x.shape
    d_ff = w_out_b.shape[0]
    nf = d_ff // tf
    assert nf >= 2, "first and last d_ff steps are distinct code paths"
    nr = r // r_blk
    rows = nb_blk * r_blk
    x_spec = pl.BlockSpec((nb_blk, r_blk, d), lambda i, f: (i // nr, i % nr, 0))
    vec_spec = pl.BlockSpec((nb_blk, 1, d), lambda i, f: (i // nr, 0, 0))
    par_spec = pl.BlockSpec((1, 1, d), lambda i, f: (0, 0, 0))
    return pl.pallas_call(
        _ffn_kernel,
        grid=((nbat // nb_blk) * nr, nf),
        in_specs=[
            x_spec, vec_spec, vec_spec, vec_spec, par_spec,
            pl.BlockSpec((d, tf), lambda i, f: (0, f)),
            pl.BlockSpec((d, tf), lambda i, f: (0, f + nf)),
            pl.BlockSpec((tf, d), lambda i, f: (f, 0)),
            par_spec, vec_spec, vec_spec,
        ],
        out_specs=x_spec,
        out_shape=jax.ShapeDtypeStruct(x.shape, F32),
        scratch_shapes=[pltpu.VMEM((rows, d), BF16)],
        compiler_params=_params(("arbitrary", "arbitrary")),
        name="ffn",
    )(x, shift[:, None, :], scale[:, None, :], gate[:, None, :], g_ffn.reshape(1, 1, d),
      w_in_b, w_in_b, w_out_b, g_final.reshape(1, 1, d), shift_f[:, None, :], scale_f[:, None, :])


ATTN_Q_BLOCK = 1024
ATTN_K_BEAT = 512
PROMPT_ROWS = 512
SAMPLE_BATCH_BLOCK = 8
FFN_TILE = 512
FFN_ROWS = 1024
FFN_ROW_GROUPS = 2


def kernel(x_prompt, x_sample, cache_k, cache_v, c_prompt, c_sample, rel_bias, w_ada, b_ada, w_ada_final,
           b_ada_final, g_mix, g_ffn, g_final, w_in, mlp_ln_g, mlp_ln_b, w_s, b_s, lambda_q1, lambda_k1,
           lambda_q2, lambda_k2, sub_g, w_out, w_ffn_in, w_ffn_out):
    B, S, D = x_prompt.shape
    DB, T, _ = x_sample.shape
    depth = w_in.shape[0]
    past = cache_k.shape[2]
    width = N_HEADS * V_HEAD_DIM
    mlp_chunk = w_s.shape[-1]

    c_all = jnp.concatenate([c_prompt, c_sample], axis=0)
    mod_f = _adaln(c_all, w_ada_final, b_ada_final)
    bias_p = _rel_bias_tiles(rel_bias, ATTN_Q_BLOCK, ATTN_Q_BLOCK + 2 * ATTN_K_BEAT, ATTN_K_BEAT, 0, True, True)
    bias_s = _rel_bias_tiles(rel_bias, T, past + T, past, 0, False, False)

    assert depth == 1, "the final adaLN norm is fused into the single layer's FFN kernel"
    lam_init = 0.8 - 0.6 * math.exp(-0.3 * 0)
    lam_vecs = jnp.stack([lambda_q1[0], lambda_k1[0], lambda_q2[0], lambda_k2[0]])
    subg = sub_g.reshape(1, V_HEAD_DIM)
    mod = _adaln(c_all, w_ada[0], b_ada[0])
    sh1, sc1, gt1, sh2, sc2, gt2 = jnp.split(mod, 6, axis=-1)
    shf, scf = jnp.split(mod_f, 2, axis=-1)
    w_in_b = w_in[0].astype(BF16)
    w_out_b = w_out[0].astype(BF16)
    w_f_in_b = w_ffn_in[0].astype(BF16)
    w_f_out_b = w_ffn_out[0].astype(BF16)
    mixer_w = (g_mix[0], w_in_b, mlp_ln_g[0], mlp_ln_b[0], w_s[0], b_s[0])

    m, qt, kp, kb, vp, vt = _mixer_in(x_prompt, sh1[:B], sc1[:B], *mixer_w, nb_blk=1, r_blk=PROMPT_ROWS,
                                     t_chunk=mlp_chunk, emit_gv=False, v_transposed=True)
    a = _attn_prompt(qt, kb.reshape(B, S, width), vt, bias_p, lam_vecs,
                     subg.reshape(V_HEAD_DIM, 1), bq=ATTN_Q_BLOCK, bk=ATTN_K_BEAT, lam_init=lam_init)
    xp = _mixer_out(x_prompt, m, a.reshape(B * S, width), gt1[:B], w_out_b, nb_blk=1, r_blk=PROMPT_ROWS)
    yp = _ffn(xp, sh2[:B], sc2[:B], gt2[:B], g_ffn[0], w_f_in_b, w_f_out_b, g_final, shf[:B], scf[:B],
              nb_blk=1, r_blk=FFN_ROWS, tf=FFN_TILE)

    m, q, ks, kb, vs, vb, gvs = _mixer_in(x_sample, sh1[B:], sc1[B:], *mixer_w, nb_blk=SAMPLE_BATCH_BLOCK,
                                          r_blk=T, t_chunk=T, emit_gv=True, v_transposed=False)
    a = _attn_sample(q.reshape(DB, T, width), cache_k.reshape(DB, past * N_HEADS, LANES),
                     cache_v.reshape(DB, past * N_HEADS, LANES), kb.reshape(DB, T, width),
                     vb.reshape(DB, T, width), bias_s, lam_vecs, subg, lam_init=lam_init)
    xs = _mixer_out(x_sample, m, a.reshape(DB * T, width), gt1[B:], w_out_b, nb_blk=SAMPLE_BATCH_BLOCK, r_blk=T)
    ys = _ffn(xs, sh2[B:], sc2[B:], gt2[B:], g_ffn[0], w_f_in_b, w_f_out_b, g_final, shf[B:], scf[B:],
              nb_blk=DB, r_blk=T, tf=FFN_TILE)

    head_shape = (N_HEADS, V_HEAD_DIM)
    return (yp, ys, kp.reshape(1, B, S, *head_shape), vp.reshape(1, B, S, *head_shape),
            ks.reshape(1, DB, T, *head_shape), vs.reshape(1, DB, T, *head_shape),
            gvs.reshape(1, DB, T, MLP_GROUPS, MLP_GROUP_DIM))
```

```python
import functools
import math

import jax
import jax.numpy as jnp
from jax import lax
from jax.experimental import pallas as pl
from jax.experimental.pallas import tpu as pltpu

LANES = 128
SUBLANES = 8
VMEM_LIMIT_BYTES = 56 * 1024 * 1024
MXU_WIDTH = 256

ATTN_STRIP = MXU_WIDTH
ONES_ROWS = SUBLANES
R_INIT_KEYS = 128

CHUNK = 64
N_HEADS = 8
HEAD_DIM = 64
V_HEAD_DIM = 128
MLP_GROUPS = 8
MLP_GROUP_DIM = 128
N_BUCKETS = 32
MAX_DISTANCE = 128
EPS = 1e-6
MASK_VALUE = -1e30
LOG2E = math.log2(math.e)

BF16 = jnp.bfloat16
F32 = jnp.float32


def _params(sem):
    return pltpu.CompilerParams(dimension_semantics=sem, vmem_limit_bytes=VMEM_LIMIT_BYTES)


def _adaln_kernel(c_ref, w_ref, b_ref, o_ref):
    c = c_ref[...]
    a = c * jax.nn.sigmoid(c)
    o_ref[...] = jnp.dot(a, w_ref[...], preferred_element_type=F32) + b_ref[...]


def _adaln(c, w, b, tn=1024):
    rows, d = c.shape
    n = w.shape[1]
    return pl.pallas_call(
        _adaln_kernel,
        grid=(n // tn,),
        in_specs=[
            pl.BlockSpec((rows, d), lambda j: (0, 0)),
            pl.BlockSpec((d, tn), lambda j: (0, j)),
            pl.BlockSpec((1, tn), lambda j: (0, j)),
        ],
        out_specs=pl.BlockSpec((rows, tn), lambda j: (0, j)),
        out_shape=jax.ShapeDtypeStruct((rows, n), F32),
        compiler_params=_params(("arbitrary",)),
        name="adaln",
    )(c, w, b.reshape(1, n))


def _rel_bias_kernel(tab_ref, o_ref, *, nq, nk, q_start, k_start, shift_far, keys_on_rows):
    h = pl.program_id(0)
    nb = N_BUCKETS // 2
    max_exact = nb // 2
    q_axis, k_axis = (1, 0) if keys_on_rows else (0, 1)
    rows, cols = (nk, nq) if keys_on_rows else (nq, nk)
    shift = tab_ref[nb - 1, h] if shift_far else 0.0

    def block(shape, q0, k0):
        q_pos = q0 + lax.broadcasted_iota(jnp.int32, shape, q_axis)
        k_pos = k0 + lax.broadcasted_iota(jnp.int32, shape, k_axis)
        rel = k_pos - q_pos
        ret = jnp.where(rel > 0, nb, 0)
        n = jnp.abs(rel)
        nf = jnp.maximum(n, 1).astype(F32)
        large = max_exact + (jnp.log(nf / max_exact) / math.log(MAX_DISTANCE / max_exact)
                             * (nb - max_exact)).astype(jnp.int32)
        large = jnp.minimum(large, nb - 1)
        bucket = ret + jnp.where(n < max_exact, n, large)
        bias = jnp.zeros(shape, F32)
        for bkt in range(N_BUCKETS):
            bias = jnp.where(bucket == bkt, tab_ref[bkt, h], bias)
        allowed = (k_pos // CHUNK) <= (q_pos // CHUNK)
        return jnp.where(allowed, (bias - shift) * LOG2E, MASK_VALUE)

    if rows % LANES or cols % LANES:
        o_ref[0] = block((rows, cols), q_start, k_start)
        return
    far_value = (tab_ref[nb - 1, h] - shift) * LOG2E
    for rb in range(rows // LANES):
        for cb in range(cols // LANES):
            k0 = k_start + LANES * (rb if keys_on_rows else cb)
            q0 = q_start + LANES * (cb if keys_on_rows else rb)
            sl = (0, slice(rb * LANES, (rb + 1) * LANES), slice(cb * LANES, (cb + 1) * LANES))
            if (k0 - q0) + (LANES - 1) <= -MAX_DISTANCE:
                o_ref[sl] = jnp.full((LANES, LANES), far_value, F32)
            elif (k0 - q0) - (LANES - 1) >= CHUNK:
                o_ref[sl] = jnp.full((LANES, LANES), MASK_VALUE, F32)
            else:
                o_ref[sl] = block((LANES, LANES), q0, k0)


def _rel_bias_tiles(rel_bias, nq, nk, q_start, k_start, shift_far, keys_on_rows):
    out_tile = (nk, nq) if keys_on_rows else (nq, nk)
    return pl.pallas_call(
        functools.partial(_rel_bias_kernel, nq=nq, nk=nk, q_start=q_start, k_start=k_start,
                          shift_far=shift_far, keys_on_rows=keys_on_rows),
        grid=(N_HEADS,),
        in_specs=[pl.BlockSpec(memory_space=pltpu.SMEM)],
        out_specs=pl.BlockSpec((1,) + out_tile, lambda h: (h, 0, 0)),
        out_shape=jax.ShapeDtypeStruct((N_HEADS,) + out_tile, F32),
        compiler_params=_params(("arbitrary",)),
        name="rel_bias",
    )(rel_bias)


def _mixer_in_kernel(x_ref, sh_ref, sc_ref, g_ref, w_ref, lng_ref, lnb_ref, ws_ref, bs_ref,
                     *refs, t_chunk, emit_gv, v_transposed):
    if emit_gv:
        m_ref, q_ref, k_ref, kb_ref, v_ref, vb_ref, gv_ref, u_s = refs
    else:
        m_ref, q_ref, k_ref, kb_ref, v_ref, vb_ref, u_s = refs
        gv_ref = None
    rows = u_s.shape[0]
    width = u_s.shape[1]

    def store_per_head(ref, val):
        for hd in range(N_HEADS):
            ref[pl.ds(hd, rows, stride=N_HEADS), :] = val[:, hd * LANES:(hd + 1) * LANES]

    x = x_ref[...]
    y = x * lax.rsqrt(jnp.mean(x * x, axis=-1, keepdims=True) + EPS)
    y = y * g_ref[...]
    hm = y * (1.0 + sc_ref[...]) + sh_ref[...]
    h = hm.reshape(rows, hm.shape[-1]).astype(BF16)

    def project(seg):
        return jnp.dot(h, w_ref[:, seg * width:(seg + 1) * width], preferred_element_type=F32)

    u_s[...] = jax.nn.gelu(project(0))

    g = jax.nn.gelu(project(1))
    mu = jnp.mean(g, axis=-1, keepdims=True)
    var = jnp.mean(jnp.square(g - mu), axis=-1, keepdims=True)
    gv = (g - mu) * lax.rsqrt(var + EPS) * lng_ref[...] + lnb_ref[...]
    if emit_gv:
        store_per_head(gv_ref, gv)
    gvb = gv.astype(BF16)
    ii = lax.broadcasted_iota(jnp.int32, (t_chunk, t_chunk), 0)
    jj = lax.broadcasted_iota(jnp.int32, (t_chunk, t_chunk), 1)
    mask = (jj // CHUNK) <= (ii // CHUNK)
    for grp in range(MLP_GROUPS):
        wg = jnp.where(mask, ws_ref[grp], 0.0).astype(BF16)
        bg = bs_ref[grp]
        cs = slice(grp * MLP_GROUP_DIM, (grp + 1) * MLP_GROUP_DIM)
        for c in range(rows // t_chunk):
            rs = slice(c * t_chunk, (c + 1) * t_chunk)
            mixed = jnp.dot(wg, gvb[rs, cs], preferred_element_type=F32) + bg
            m_ref[rs, cs] = (u_s[rs, cs] * mixed).astype(BF16)

    z = project(2) * (HEAD_DIM ** -0.5 * LOG2E)
    if v_transposed:
        q_ref[0] = z.T.astype(BF16)
    else:
        q_ref[...] = z.astype(BF16)

    z = project(3)
    store_per_head(k_ref, z)
    kb_ref[...] = z.astype(BF16)

    z = project(4)
    store_per_head(v_ref, z)
    if v_transposed:
        vb_ref[0] = z.T.astype(BF16)
    else:
        vb_ref[...] = z.astype(BF16)


def _mixer_in(x, shift, scale, g_mix, w_in_b, ln_g, ln_b, w_s, b_s, *, nb_blk, r_blk, t_chunk, emit_gv,
              v_transposed):
    nbat, r, d = x.shape
    width = d // 2
    assert w_in_b.shape[1] == 5 * width
    nr = r // r_blk
    n_tiles = (nbat // nb_blk) * nr
    rows = nb_blk * r_blk
    tokens = nbat * r
    row_idx = lambda i: (i, 0)
    flat = jax.ShapeDtypeStruct((tokens, width), BF16)
    per_head = jax.ShapeDtypeStruct((tokens * N_HEADS, LANES), F32)
    flat_spec = pl.BlockSpec((rows, width), row_idx)
    per_head_spec = pl.BlockSpec((rows * N_HEADS, LANES), row_idx)
    out_shape = [flat, flat, per_head, flat, per_head, flat]
    out_specs = [flat_spec, flat_spec, per_head_spec, flat_spec, per_head_spec, flat_spec]
    if v_transposed:
        assert nb_blk == 1
        for idx in (1, 5):
            out_shape[idx] = jax.ShapeDtypeStruct((nbat, width, r), BF16)
            out_specs[idx] = pl.BlockSpec((1, width, r_blk), lambda i: (i // nr, 0, i % nr))
    if emit_gv:
        out_shape.append(per_head)
        out_specs.append(per_head_spec)
    ws_t = w_s[:, :t_chunk, :t_chunk]
    bs_t = b_s[:, :t_chunk, None]
    once = pl.Buffered(1)
    return pl.pallas_call(
        functools.partial(_mixer_in_kernel, t_chunk=t_chunk, emit_gv=emit_gv, v_transposed=v_transposed),
        grid=(n_tiles,),
        in_specs=[
            pl.BlockSpec((nb_blk, r_blk, d), lambda i: (i // nr, i % nr, 0)),
            pl.BlockSpec((nb_blk, 1, d), lambda i: (i // nr, 0, 0)),
            pl.BlockSpec((nb_blk, 1, d), lambda i: (i // nr, 0, 0)),
            pl.BlockSpec((1, 1, d), lambda i: (0, 0, 0)),
            pl.BlockSpec(w_in_b.shape, lambda i: (0, 0), pipeline_mode=once),
            pl.BlockSpec((1, width), lambda i: (0, 0)),
            pl.BlockSpec((1, width), lambda i: (0, 0)),
            pl.BlockSpec((MLP_GROUPS, t_chunk, t_chunk), lambda i: (0, 0, 0)),
            pl.BlockSpec((MLP_GROUPS, t_chunk, 1), lambda i: (0, 0, 0)),
        ],
        out_specs=out_specs,
        out_shape=out_shape,
        scratch_shapes=[pltpu.VMEM((rows, width), F32)],
        compiler_params=_params(("arbitrary",)),
        name="mixer_in",
    )(x, shift[:, None, :], scale[:, None, :], g_mix.reshape(1, 1, d), w_in_b,
      ln_g.reshape(1, width), ln_b.reshape(1, width), ws_t, bs_t)


def _split_q(q):
    lane = lax.broadcasted_iota(jnp.int32, q.shape, 1)
    zero = jnp.zeros_like(q)
    return jnp.concatenate([jnp.where(lane < HEAD_DIM, q, zero), jnp.where(lane >= HEAD_DIM, q, zero)], axis=0)


def _softmax_step(s, v, m, l, acc):
    m_new = jnp.maximum(m, jnp.max(s, axis=-1, keepdims=True))
    alpha = jnp.exp2(m - m_new)
    p = jnp.exp2(s - m_new)
    l_new = alpha * l + jnp.sum(p, axis=-1, keepdims=True)
    acc_new = alpha * acc + jnp.dot(p.astype(BF16), v, preferred_element_type=F32)
    return m_new, l_new, acc_new


def _diff_lambda(lam_ref, lam_init):
    lv = lam_ref[...]
    s1 = jnp.sum(lv[0:1] * lv[1:2], axis=-1, keepdims=True)
    s2 = jnp.sum(lv[2:3] * lv[3:4], axis=-1, keepdims=True)
    return jnp.exp(s1) - jnp.exp(s2) + lam_init


def _finish_heads(l, acc, lam, subg, lam_init, n):
    o = acc / l
    a = o[:n] - lam * o[n:]
    a = a * lax.rsqrt(jnp.mean(a * a, axis=-1, keepdims=True) + EPS)
    return a * subg * (1.0 - lam_init)


def _qk(qs, k):
    return lax.dot_general(qs, k, (((1,), (1,)), ((), ())), preferred_element_type=F32)


def _attn_prompt_kernel(qt_ref, k_ref, vt_ref, bias_ref, lam_ref, subg_ref, o_ref,
                        p_s, alpha_s, r_s, acc_s, *, bq, bk, lam_init):
    qi = pl.program_id(2)
    halves = bq // ATTN_STRIP
    strips = [(c, h) for c in range(2) for h in range(halves)]
    n_near = bq // bk + 1
    qt = qt_ref[0]
    row = lax.broadcasted_iota(jnp.int32, qt.shape, 0)
    zero = jnp.zeros_like(qt)
    qs = (jnp.where(row < HEAD_DIM, qt, zero), jnp.where(row >= HEAD_DIM, qt, zero))

    ones = jnp.ones((ONES_ROWS, bk), BF16)

    def keys(j):
        return k_ref[0, pl.ds(pl.multiple_of(j * bk, bk), bk), :]

    def values(j):
        vt = vt_ref[0, :, pl.ds(pl.multiple_of(j * bk, bk), bk)]
        return jnp.concatenate([vt, ones], axis=0)

    def logits(k, n, bias):
        c, h = strips[n]
        cols = slice(h * ATTN_STRIP, (h + 1) * ATTN_STRIP)
        s = jnp.dot(k, qs[c][:, cols], preferred_element_type=F32)
        if isinstance(bias, tuple):
            s = s + bias_ref[0, pl.ds(bias[0], bias[1]), cols]
        elif bias is not None:
            s = s + bias
        return s

    def beat(new=None, old=None):
        if new is not None:
            k = keys(new[0])
        if old is not None:
            vt1 = values(old[0])
        for n in range(len(strips)):
            if new is not None and new[3][n]:
                _, slot, bias, live = new
                if isinstance(bias, tuple):
                    bias = (bias[0], live[n])
                s = logits(k[:live[n]], n, bias)
                r = r_s[n]
                p_s[slot, n, :live[n]] = jnp.exp2(s - r).astype(BF16)
                r_new = jnp.maximum(r, jnp.max(s, axis=0, keepdims=True))
                alpha_s[slot, n] = jnp.exp2(r - r_new)
                r_s[n] = r_new
            if old is not None and old[2][n]:
                _, slot, live = old
                pv = jnp.dot(vt1[:, :live[n]], p_s[slot, n, :live[n]], preferred_element_type=F32)
                acc_s[n] = (acc_s[n] + pv) * alpha_s[slot, n]

    n_far = jnp.maximum((qi * bq - MAX_DISTANCE) // bk, 0)
    first = jnp.where(qi == 0, 1, 0)
    near_bias = [(pl.multiple_of((i + first) * bk, bk), bk) for i in range(n_near)]

    k0 = k_ref[0, :R_INIT_KEYS, :]
    for n, (_, h) in enumerate(strips):
        bias0 = jnp.where(qi == 0, bias_ref[0, bk:bk + R_INIT_KEYS, h * ATTN_STRIP:(h + 1) * ATTN_STRIP], 0.0)
        r_s[n] = jnp.max(logits(k0, n, bias0), axis=0, keepdims=True)
    acc_s[...] = jnp.zeros(acc_s.shape, F32)
    alpha_s[1] = jnp.ones(alpha_s.shape[1:], F32)
    p_s[1] = jnp.zeros(p_s.shape[1:], BF16)

    n_pairs = n_far // 2

    all_keys = (bk,) * len(strips)

    def pair(j):
        beat(new=(j, 0, None, all_keys), old=(jnp.maximum(j - 1, 0), 1, all_keys))
        beat(new=(j + 1, 1, None, all_keys), old=(j, 0, all_keys))

    n_quads = n_far // 4

    @pl.loop(0, n_quads)
    def _(t):
        pair(4 * t)
        pair(4 * t + 2)

    @pl.loop(2 * n_quads, n_pairs)
    def _(t):
        pair(2 * t)

    x = 2 * n_pairs
    tail = [(x, jnp.where(n_far % 2 == 1, 0.0, MASK_VALUE), all_keys)]
    for i, bias_i in enumerate(near_bias):
        live = tuple(min(max((h + 1) * ATTN_STRIP - (i - 1) * bk, 0), bk) for _, h in strips)
        tail.append((n_far + i, bias_i, live))
    for t, (j, bias_t, live) in enumerate(tail):
        prev = (jnp.maximum(x - 1, 0), 1, all_keys) if t == 0 else (tail[t - 1][0], (t - 1) % 2, tail[t - 1][2])
        beat(new=(j, t % 2, bias_t, live), old=prev)
    beat(old=(tail[-1][0], (len(tail) - 1) % 2, tail[-1][2]))

    lam = _diff_lambda(lam_ref, lam_init)

    def emit_output():
        for h in range(halves):
            o = []
            for c in range(2):
                acc = acc_s[c * halves + h]
                o.append(acc[:V_HEAD_DIM] / acc[V_HEAD_DIM:V_HEAD_DIM + 1])
            a = o[0] - lam * o[1]
            a = a * lax.rsqrt(jnp.mean(a * a, axis=0, keepdims=True) + EPS)
            a = a * subg_ref[...] * (1.0 - lam_init)
            o_ref[0, h * ATTN_STRIP:(h + 1) * ATTN_STRIP, :] = a.T.astype(o_ref.dtype)

    emit_output()

    acc_all = acc_s[...]
    bad = jnp.where(jnp.abs(acc_all) < jnp.inf, 0.0, 1.0)
    bad = jnp.maximum(bad, jnp.where(acc_all[:, V_HEAD_DIM:V_HEAD_DIM + 1] > 0.0, 0.0, 1.0))
    bad = jnp.max(jnp.max(bad, axis=0), axis=0, keepdims=True)
    bad = jnp.max(bad, axis=1, keepdims=True)[0, 0]

    @pl.when(bad > 0.0)
    def _():
        r_s[...] = jnp.full(r_s.shape, MASK_VALUE, F32)
        acc_s[...] = jnp.zeros(acc_s.shape, F32)

        @pl.loop(0, n_far + n_near)
        def _(j):
            start = pl.multiple_of(jnp.maximum(j - n_far + first, 0) * bk, bk)
            k = keys(j)
            vt1 = values(j)
            for n, (_, h) in enumerate(strips):
                tile = bias_ref[0, pl.ds(start, bk), h * ATTN_STRIP:(h + 1) * ATTN_STRIP]
                s = logits(k, n, jnp.where(j < n_far, 0.0, tile))
                m_old = r_s[n]
                m_new = jnp.maximum(m_old, jnp.max(s, axis=0, keepdims=True))
                p = jnp.exp2(s - m_new).astype(BF16)
                acc_s[n] = jnp.exp2(m_old - m_new) * acc_s[n] + jnp.dot(vt1, p, preferred_element_type=F32)
                r_s[n] = m_new

        emit_output()


def _attn_prompt(qt, k, vt, bias, lam_vecs, sub_g_col, *, bq, bk, lam_init):
    b, s, _ = k.shape
    n_strips = 2 * (bq // ATTN_STRIP)
    assert bq % bk == 0 and bk >= MAX_DISTANCE and s >= bq + bk
    return pl.pallas_call(
        functools.partial(_attn_prompt_kernel, bq=bq, bk=bk, lam_init=lam_init),
        grid=(b, N_HEADS, s // bq),
        in_specs=[
            pl.BlockSpec((1, LANES, bq), lambda bi, h, qi: (bi, h, qi)),
            pl.BlockSpec((1, s, LANES), lambda bi, h, qi: (bi, 0, h)),
            pl.BlockSpec((1, V_HEAD_DIM, s), lambda bi, h, qi: (bi, h, 0)),
            pl.BlockSpec((1, bq + 2 * bk, bq), lambda bi, h, qi: (h, 0, 0)),
            pl.BlockSpec((4, HEAD_DIM), lambda bi, h, qi: (0, 0)),
            pl.BlockSpec((V_HEAD_DIM, 1), lambda bi, h, qi: (0, 0)),
        ],
        out_specs=pl.BlockSpec((1, bq, LANES), lambda bi, h, qi: (bi, qi, h)),
        out_shape=jax.ShapeDtypeStruct((b, s, N_HEADS * V_HEAD_DIM), BF16),
        scratch_shapes=[
            pltpu.VMEM((2, n_strips, bk, ATTN_STRIP), BF16),
            pltpu.VMEM((2, n_strips, 1, ATTN_STRIP), F32),
            pltpu.VMEM((n_strips, 1, ATTN_STRIP), F32),
            pltpu.VMEM((n_strips, V_HEAD_DIM + ONES_ROWS, ATTN_STRIP), F32),
        ],
        compiler_params=_params(("arbitrary", "arbitrary", "arbitrary")),
        name="attn_prompt",
    )(qt, k, vt, bias, lam_vecs, sub_g_col)


def _attn_sample_kernel(q_ref, ck_ref, cv_ref, nk_ref, nv_ref, bias_ref, lam_ref, subg_ref, o_ref,
                        *, past, lam_init):
    t = q_ref.shape[1]
    lam = _diff_lambda(lam_ref, lam_init)
    for hd in range(N_HEADS):
        cs = slice(hd * LANES, (hd + 1) * LANES)
        qs = _split_q(q_ref[0, :, cs])
        bias = bias_ref[hd]
        bias2 = jnp.concatenate([bias, bias], axis=0)
        carry = (jnp.full((2 * t, 1), MASK_VALUE, F32), jnp.zeros((2 * t, 1), F32),
                 jnp.zeros((2 * t, V_HEAD_DIM), F32))
        kc = ck_ref[0, pl.ds(hd, past, stride=N_HEADS), :].astype(BF16)
        vc = cv_ref[0, pl.ds(hd, past, stride=N_HEADS), :].astype(BF16)
        carry = _softmax_step(_qk(qs, kc) + bias2[:, :past], vc, *carry)
        m, l, acc = _softmax_step(_qk(qs, nk_ref[0, :, cs]) + bias2[:, past:], nv_ref[0, :, cs], *carry)
        o_ref[0, :, cs] = _finish_heads(l, acc, lam, subg_ref[...], lam_init, t).astype(o_ref.dtype)


def _attn_sample(q, cache_k, cache_v, new_k, new_v, bias, lam_vecs, sub_g, *, lam_init):
    b, t, width = q.shape
    past = cache_k.shape[1] // N_HEADS
    flat_spec = pl.BlockSpec((1, t, width), lambda bi: (bi, 0, 0))
    cache_spec = pl.BlockSpec((1, past * N_HEADS, LANES), lambda bi: (bi, 0, 0))
    return pl.pallas_call(
        functools.partial(_attn_sample_kernel, past=past, lam_init=lam_init),
        grid=(b,),
        in_specs=[
            flat_spec, cache_spec, cache_spec, flat_spec, flat_spec,
            pl.BlockSpec((N_HEADS, t, past + t), lambda bi: (0, 0, 0)),
            pl.BlockSpec((4, HEAD_DIM), lambda bi: (0, 0)),
            pl.BlockSpec((1, V_HEAD_DIM), lambda bi: (0, 0)),
        ],
        out_specs=flat_spec,
        out_shape=jax.ShapeDtypeStruct((b, t, width), BF16),
        compiler_params=_params(("arbitrary",)),
        name="attn_sample",
    )(q, cache_k, cache_v, new_k, new_v, bias, lam_vecs, sub_g)


def _mixer_out_kernel(x_ref, m_ref, a_ref, gt_ref, w_ref, o_ref):
    half = m_ref.shape[-1]
    y = jnp.dot(m_ref[...], w_ref[:half, :], preferred_element_type=F32)
    y = y + jnp.dot(a_ref[...], w_ref[half:, :], preferred_element_type=F32)
    nb, r, d = x_ref.shape
    o_ref[...] = x_ref[...] + gt_ref[...] * y.reshape(nb, r, d)


def _mixer_out(x, m, a, gate, w_out_b, *, nb_blk, r_blk):
    nbat, r, d = x.shape
    nr = r // r_blk
    rows = nb_blk * r_blk
    half = m.shape[-1]
    return pl.pallas_call(
        _mixer_out_kernel,
        grid=((nbat // nb_blk) * nr,),
        in_specs=[
            pl.BlockSpec((nb_blk, r_blk, d), lambda i: (i // nr, i % nr, 0)),
            pl.BlockSpec((rows, half), lambda i: (i, 0)),
            pl.BlockSpec((rows, half), lambda i: (i, 0)),
            pl.BlockSpec((nb_blk, 1, d), lambda i: (i // nr, 0, 0)),
            pl.BlockSpec((d, d), lambda i: (0, 0)),
        ],
        out_specs=pl.BlockSpec((nb_blk, r_blk, d), lambda i: (i // nr, i % nr, 0)),
        out_shape=jax.ShapeDtypeStruct(x.shape, F32),
        compiler_params=_params(("arbitrary",)),
        name="mixer_out",
    )(x, m, a, gate[:, None, :], w_out_b)


def _row_subtiles(nb, r, count):
    if nb >= count:
        per = nb // count
        return [(slice(s * per, (s + 1) * per), slice(None), slice(s * per * r, (s + 1) * per * r))
                for s in range(count)]
    per = r // count
    return [(slice(None), slice(s * per, (s + 1) * per), slice(s * per, (s + 1) * per)) for s in range(count)]


def _ffn_kernel(x_ref, sh_ref, sc_ref, gt_ref, g_ref, wg_ref, wu_ref, wo_ref, gf_ref, shf_ref, scf_ref,
                o_ref, h_s):
    f = pl.program_id(1)
    last = pl.num_programs(1) - 1
    nb, r, d = x_ref.shape
    subs = _row_subtiles(nb, r, FFN_ROW_GROUPS)

    def normalize(sub):
        bsl, rsl, rows = sub
        x = x_ref[bsl, rsl, :]
        y = x * lax.rsqrt(jnp.mean(x * x, axis=-1, keepdims=True) + EPS)
        hm = (y * g_ref[...]) * (1.0 + sc_ref[bsl]) + sh_ref[bsl]
        h_s[rows, :] = hm.reshape(-1, d).astype(BF16)

    def swiglu(rows, shape):
        h = h_s[rows, :]
        zg = jnp.dot(h, wg_ref[...], preferred_element_type=F32)
        zu = jnp.dot(h, wu_ref[...], preferred_element_type=F32)
        act = (zg * jax.nn.sigmoid(zg) * zu).astype(BF16)
        return jnp.dot(act, wo_ref[...], preferred_element_type=F32).reshape(shape)

    def finish(sub):
        bsl, rsl, rows = sub
        x = x_ref[bsl, rsl, :]
        x2 = x + gt_ref[bsl] * (o_ref[bsl, rsl, :] + swiglu(rows, x.shape))
        y = x2 * lax.rsqrt(jnp.mean(x2 * x2, axis=-1, keepdims=True) + EPS)
        o_ref[bsl, rsl, :] = (y * gf_ref[...]) * (1.0 + scf_ref[bsl]) + shf_ref[bsl]

    @pl.when(f == 0)
    def _():
        for bsl, rsl, rows in subs:
            normalize((bsl, rsl, rows))
            o_ref[bsl, rsl, :] = swiglu(rows, o_ref[bsl, rsl, :].shape)

    @pl.when(jnp.logical_and(f > 0, f < last))
    def _():
        o_ref[...] += swiglu(slice(None), o_ref.shape)

    @pl.when(f == last)
    def _():
        for sub in subs:
            finish(sub)


def _ffn(x, shift, scale, gate, g_ffn, w_in_b, w_out_b, g_final, shift_f, scale_f, *, nb_blk, r_blk, tf):
    nbat, r, d = x.shape
    d_ff = w_out_b.shape[0]
    nf = d_ff // tf
    assert nf >= 2, "first and last d_ff steps are distinct code paths"
    nr = r // r_blk
    rows = nb_blk * r_blk
    x_spec = pl.BlockSpec((nb_blk, r_blk, d), lambda i, f: (i // nr, i % nr, 0))
    vec_spec = pl.BlockSpec((nb_blk, 1, d), lambda i, f: (i // nr, 0, 0))
    par_spec = pl.BlockSpec((1, 1, d), lambda i, f: (0, 0, 0))
    return pl.pallas_call(
        _ffn_kernel,
        grid=((nbat // nb_blk) * nr, nf),
        in_specs=[
            x_spec, vec_spec, vec_spec, vec_spec, par_spec,
            pl.BlockSpec((d, tf), lambda i, f: (0, f)),
            pl.BlockSpec((d, tf), lambda i, f: (0, f + nf)),
            pl.BlockSpec((tf, d), lambda i, f: (f, 0)),
            par_spec, vec_spec, vec_spec,
        ],
        out_specs=x_spec,
        out_shape=jax.ShapeDtypeStruct(x.shape, F32),
        scratch_shapes=[pltpu.VMEM((rows, d), BF16)],
        compiler_params=_params(("arbitrary", "arbitrary")),
        name="ffn",
    )(x, shift[:, None, :], scale[:, None, :], gate[:, None, :], g_ffn.reshape(1, 1, d),
      w_in_b, w_in_b, w_out_b, g_final.reshape(1, 1, d), shift_f[:, None, :], scale_f[:, None, :])


ATTN_Q_BLOCK = 1024
ATTN_K_BEAT = 512
PROMPT_ROWS = 512
SAMPLE_BATCH_BLOCK = 8
FFN_TILE = 512
FFN_ROWS = 1024
FFN_ROW_GROUPS = 2


def kernel(x_prompt, x_sample, cache_k, cache_v, c_prompt, c_sample, rel_bias, w_ada, b_ada, w_ada_final,
           b_ada_final, g_mix, g_ffn, g_final, w_in, mlp_ln_g, mlp_ln_b, w_s, b_s, lambda_q1, lambda_k1,
           lambda_q2, lambda_k2, sub_g, w_out, w_ffn_in, w_ffn_out):
    B, S, D = x_prompt.shape
    DB, T, _ = x_sample.shape
    depth = w_in.shape[0]
    past = cache_k.shape[2]
    width = N_HEADS * V_HEAD_DIM
    mlp_chunk = w_s.shape[-1]

    c_all = jnp.concatenate([c_prompt, c_sample], axis=0)
    mod_f = _adaln(c_all, w_ada_final, b_ada_final)
    bias_p = _rel_bias_tiles(rel_bias, ATTN_Q_BLOCK, ATTN_Q_BLOCK + 2 * ATTN_K_BEAT, ATTN_K_BEAT, 0, True, True)
    bias_s = _rel_bias_tiles(rel_bias, T, past + T, past, 0, False, False)

    assert depth == 1, "the final adaLN norm is fused into the single layer's FFN kernel"
    lam_init = 0.8 - 0.6 * math.exp(-0.3 * 0)
    lam_vecs = jnp.stack([lambda_q1[0], lambda_k1[0], lambda_q2[0], lambda_k2[0]])
    subg = sub_g.reshape(1, V_HEAD_DIM)
    mod = _adaln(c_all, w_ada[0], b_ada[0])
    sh1, sc1, gt1, sh2, sc2, gt2 = jnp.split(mod, 6, axis=-1)
    shf, scf = jnp.split(mod_f, 2, axis=-1)
    w_in_b = w_in[0].astype(BF16)
    w_out_b = w_out[0].astype(BF16)
    w_f_in_b = w_ffn_in[0].astype(BF16)
    w_f_out_b = w_ffn_out[0].astype(BF16)
    mixer_w = (g_mix[0], w_in_b, mlp_ln_g[0], mlp_ln_b[0], w_s[0], b_s[0])

    m, qt, kp, kb, vp, vt = _mixer_in(x_prompt, sh1[:B], sc1[:B], *mixer_w, nb_blk=1, r_blk=PROMPT_ROWS,
                                     t_chunk=mlp_chunk, emit_gv=False, v_transposed=True)
    a = _attn_prompt(qt, kb.reshape(B, S, width), vt, bias_p, lam_vecs,
                     subg.reshape(V_HEAD_DIM, 1), bq=ATTN_Q_BLOCK, bk=ATTN_K_BEAT, lam_init=lam_init)
    xp = _mixer_out(x_prompt, m, a.reshape(B * S, width), gt1[:B], w_out_b, nb_blk=1, r_blk=PROMPT_ROWS)
    yp = _ffn(xp, sh2[:B], sc2[:B], gt2[:B], g_ffn[0], w_f_in_b, w_f_out_b, g_final, shf[:B], scf[:B],
              nb_blk=1, r_blk=FFN_ROWS, tf=FFN_TILE)

    m, q, ks, kb, vs, vb, gvs = _mixer_in(x_sample, sh1[B:], sc1[B:], *mixer_w, nb_blk=SAMPLE_BATCH_BLOCK,
                                          r_blk=T, t_chunk=T, emit_gv=True, v_transposed=False)
    a = _attn_sample(q.reshape(DB, T, width), cache_k.reshape(DB, past * N_HEADS, LANES),
                     cache_v.reshape(DB, past * N_HEADS, LANES), kb.reshape(DB, T, width),
                     vb.reshape(DB, T, width), bias_s, lam_vecs, subg, lam_init=lam_init)
    xs = _mixer_out(x_sample, m, a.reshape(DB * T, width), gt1[B:], w_out_b, nb_blk=SAMPLE_BATCH_BLOCK, r_blk=T)
    ys = _ffn(xs, sh2[B:], sc2[B:], gt2[B:], g_ffn[0], w_f_in_b, w_f_out_b, g_final, shf[B:], scf[B:],
              nb_blk=DB, r_blk=T, tf=FFN_TILE)

    head_shape = (N_HEADS, V_HEAD_DIM)
    return (yp, ys, kp.reshape(1, B, S, *head_shape), vp.reshape(1, B, S, *head_shape),
            ks.reshape(1, DB, T, *head_shape), vs.reshape(1, DB, T, *head_shape),
            gvs.reshape(1, DB, T, MLP_GROUPS, MLP_GROUP_DIM))
```

```python
import functools
import math

import jax
import jax.numpy as jnp
from jax import lax
from jax.experimental import pallas as pl
from jax.experimental.pallas import tpu as pltpu

LANES = 128
SUBLANES = 8
VMEM_LIMIT_BYTES = 56 * 1024 * 1024
MXU_WIDTH = 256

ATTN_STRIP = MXU_WIDTH
R_INIT_KEYS = 128

CHUNK = 64
N_HEADS = 8
HEAD_DIM = 64
V_HEAD_DIM = 128
MLP_GROUPS = 8
MLP_GROUP_DIM = 128
N_BUCKETS = 32
MAX_DISTANCE = 128
EPS = 1e-6
MASK_VALUE = -1e30
LOG2E = math.log2(math.e)

BF16 = jnp.bfloat16
F32 = jnp.float32


def _params(sem):
    return pltpu.CompilerParams(dimension_semantics=sem, vmem_limit_bytes=VMEM_LIMIT_BYTES)


def _adaln_kernel(c_ref, w_ref, b_ref, o_ref):
    c = c_ref[...]
    a = c * jax.nn.sigmoid(c)
    o_ref[...] = jnp.dot(a, w_ref[...], preferred_element_type=F32) + b_ref[...]


def _adaln(c, w, b, tn=1024):
    rows, d = c.shape
    n = w.shape[1]
    return pl.pallas_call(
        _adaln_kernel,
        grid=(n // tn,),
        in_specs=[
            pl.BlockSpec((rows, d), lambda j: (0, 0)),
            pl.BlockSpec((d, tn), lambda j: (0, j)),
            pl.BlockSpec((1, tn), lambda j: (0, j)),
        ],
        out_specs=pl.BlockSpec((rows, tn), lambda j: (0, j)),
        out_shape=jax.ShapeDtypeStruct((rows, n), F32),
        compiler_params=_params(("arbitrary",)),
        name="adaln",
    )(c, w, b.reshape(1, n))


def _rel_bias_kernel(tab_ref, o_ref, *, nq, nk, q_start, k_start, shift_far, keys_on_rows):
    h = pl.program_id(0)
    nb = N_BUCKETS // 2
    max_exact = nb // 2
    q_axis, k_axis = (1, 0) if keys_on_rows else (0, 1)
    rows, cols = (nk, nq) if keys_on_rows else (nq, nk)
    shift = tab_ref[nb - 1, h] if shift_far else 0.0

    def block(shape, q0, k0):
        q_pos = q0 + lax.broadcasted_iota(jnp.int32, shape, q_axis)
        k_pos = k0 + lax.broadcasted_iota(jnp.int32, shape, k_axis)
        rel = k_pos - q_pos
        ret = jnp.where(rel > 0, nb, 0)
        n = jnp.abs(rel)
        nf = jnp.maximum(n, 1).astype(F32)
        large = max_exact + (jnp.log(nf / max_exact) / math.log(MAX_DISTANCE / max_exact)
                             * (nb - max_exact)).astype(jnp.int32)
        large = jnp.minimum(large, nb - 1)
        bucket = ret + jnp.where(n < max_exact, n, large)
        bias = jnp.zeros(shape, F32)
        for bkt in range(N_BUCKETS):
            bias = jnp.where(bucket == bkt, tab_ref[bkt, h], bias)
        allowed = (k_pos // CHUNK) <= (q_pos // CHUNK)
        return jnp.where(allowed, (bias - shift) * LOG2E, MASK_VALUE)

    if rows % LANES or cols % LANES:
        o_ref[0] = block((rows, cols), q_start, k_start)
        return
    far_value = (tab_ref[nb - 1, h] - shift) * LOG2E
    for rb in range(rows // LANES):
        for cb in range(cols // LANES):
            k0 = k_start + LANES * (rb if keys_on_rows else cb)
            q0 = q_start + LANES * (cb if keys_on_rows else rb)
            sl = (0, slice(rb * LANES, (rb + 1) * LANES), slice(cb * LANES, (cb + 1) * LANES))
            if (k0 - q0) + (LANES - 1) <= -MAX_DISTANCE:
                o_ref[sl] = jnp.full((LANES, LANES), far_value, F32)
            elif (k0 - q0) - (LANES - 1) >= CHUNK:
                o_ref[sl] = jnp.full((LANES, LANES), MASK_VALUE, F32)
            else:
                o_ref[sl] = block((LANES, LANES), q0, k0)


def _rel_bias_tiles(rel_bias, nq, nk, q_start, k_start, shift_far, keys_on_rows):
    out_tile = (nk, nq) if keys_on_rows else (nq, nk)
    return pl.pallas_call(
        functools.partial(_rel_bias_kernel, nq=nq, nk=nk, q_start=q_start, k_start=k_start,
                          shift_far=shift_far, keys_on_rows=keys_on_rows),
        grid=(N_HEADS,),
        in_specs=[pl.BlockSpec(memory_space=pltpu.SMEM)],
        out_specs=pl.BlockSpec((1,) + out_tile, lambda h: (h, 0, 0)),
        out_shape=jax.ShapeDtypeStruct((N_HEADS,) + out_tile, F32),
        compiler_params=_params(("arbitrary",)),
        name="rel_bias",
    )(rel_bias)


def _mixer_in_kernel(x_ref, sh_ref, sc_ref, g_ref, w_ref, lng_ref, lnb_ref, ws_ref, bs_ref,
                     *refs, t_chunk, emit_gv, v_transposed):
    if emit_gv:
        m_ref, q_ref, k_ref, kb_ref, v_ref, vb_ref, gv_ref, u_s = refs
    else:
        m_ref, q_ref, k_ref, kb_ref, v_ref, vb_ref, u_s = refs
        gv_ref = None
    rows = u_s.shape[0]
    width = u_s.shape[1]

    def store_per_head(ref, val):
        for hd in range(N_HEADS):
            ref[pl.ds(hd, rows, stride=N_HEADS), :] = val[:, hd * LANES:(hd + 1) * LANES]

    x = x_ref[...]
    y = x * lax.rsqrt(jnp.mean(x * x, axis=-1, keepdims=True) + EPS)
    y = y * g_ref[...]
    hm = y * (1.0 + sc_ref[...]) + sh_ref[...]
    h = hm.reshape(rows, hm.shape[-1]).astype(BF16)

    def project(seg):
        return jnp.dot(h, w_ref[:, seg * width:(seg + 1) * width], preferred_element_type=F32)

    u_s[...] = jax.nn.gelu(project(0))

    g = jax.nn.gelu(project(1))
    mu = jnp.mean(g, axis=-1, keepdims=True)
    var = jnp.mean(jnp.square(g - mu), axis=-1, keepdims=True)
    gv = (g - mu) * lax.rsqrt(var + EPS) * lng_ref[...] + lnb_ref[...]
    if emit_gv:
        store_per_head(gv_ref, gv)
    gvb = gv.astype(BF16)
    ii = lax.broadcasted_iota(jnp.int32, (t_chunk, t_chunk), 0)
    jj = lax.broadcasted_iota(jnp.int32, (t_chunk, t_chunk), 1)
    mask = (jj // CHUNK) <= (ii // CHUNK)
    for grp in range(MLP_GROUPS):
        wg = jnp.where(mask, ws_ref[grp], 0.0).astype(BF16)
        bg = bs_ref[grp]
        cs = slice(grp * MLP_GROUP_DIM, (grp + 1) * MLP_GROUP_DIM)
        for c in range(rows // t_chunk):
            rs = slice(c * t_chunk, (c + 1) * t_chunk)
            mixed = jnp.dot(wg, gvb[rs, cs], preferred_element_type=F32) + bg
            m_ref[rs, cs] = (u_s[rs, cs] * mixed).astype(BF16)

    z = project(2) * (HEAD_DIM ** -0.5 * LOG2E)
    if v_transposed:
        q_ref[0] = z.T.astype(BF16)
    else:
        q_ref[...] = z.astype(BF16)

    z = project(3)
    store_per_head(k_ref, z)
    kb_ref[...] = z.astype(BF16)

    z = project(4)
    store_per_head(v_ref, z)
    if v_transposed:
        vb_ref[0] = z.T.astype(BF16)
    else:
        vb_ref[...] = z.astype(BF16)


def _mixer_in(x, shift, scale, g_mix, w_in_b, ln_g, ln_b, w_s, b_s, *, nb_blk, r_blk, t_chunk, emit_gv,
              v_transposed):
    nbat, r, d = x.shape
    width = d // 2
    assert w_in_b.shape[1] == 5 * width
    nr = r // r_blk
    n_tiles = (nbat // nb_blk) * nr
    rows = nb_blk * r_blk
    tokens = nbat * r
    row_idx = lambda i: (i, 0)
    flat = jax.ShapeDtypeStruct((tokens, width), BF16)
    per_head = jax.ShapeDtypeStruct((tokens * N_HEADS, LANES), F32)
    flat_spec = pl.BlockSpec((rows, width), row_idx)
    per_head_spec = pl.BlockSpec((rows * N_HEADS, LANES), row_idx)
    out_shape = [flat, flat, per_head, flat, per_head, flat]
    out_specs = [flat_spec, flat_spec, per_head_spec, flat_spec, per_head_spec, flat_spec]
    if v_transposed:
        assert nb_blk == 1
        for idx in (1, 5):
            out_shape[idx] = jax.ShapeDtypeStruct((nbat, width, r), BF16)
            out_specs[idx] = pl.BlockSpec((1, width, r_blk), lambda i: (i // nr, 0, i % nr))
    if emit_gv:
        out_shape.append(per_head)
        out_specs.append(per_head_spec)
    ws_t = w_s[:, :t_chunk, :t_chunk]
    bs_t = b_s[:, :t_chunk, None]
    once = pl.Buffered(1)
    return pl.pallas_call(
        functools.partial(_mixer_in_kernel, t_chunk=t_chunk, emit_gv=emit_gv, v_transposed=v_transposed),
        grid=(n_tiles,),
        in_specs=[
            pl.BlockSpec((nb_blk, r_blk, d), lambda i: (i // nr, i % nr, 0)),
            pl.BlockSpec((nb_blk, 1, d), lambda i: (i // nr, 0, 0)),
            pl.BlockSpec((nb_blk, 1, d), lambda i: (i // nr, 0, 0)),
            pl.BlockSpec((1, 1, d), lambda i: (0, 0, 0)),
            pl.BlockSpec(w_in_b.shape, lambda i: (0, 0), pipeline_mode=once),
            pl.BlockSpec((1, width), lambda i: (0, 0)),
            pl.BlockSpec((1, width), lambda i: (0, 0)),
            pl.BlockSpec((MLP_GROUPS, t_chunk, t_chunk), lambda i: (0, 0, 0)),
            pl.BlockSpec((MLP_GROUPS, t_chunk, 1), lambda i: (0, 0, 0)),
        ],
        out_specs=out_specs,
        out_shape=out_shape,
        scratch_shapes=[pltpu.VMEM((rows, width), F32)],
        compiler_params=_params(("arbitrary",)),
        name="mixer_in",
    )(x, shift[:, None, :], scale[:, None, :], g_mix.reshape(1, 1, d), w_in_b,
      ln_g.reshape(1, width), ln_b.reshape(1, width), ws_t, bs_t)


def _split_q(q):
    lane = lax.broadcasted_iota(jnp.int32, q.shape, 1)
    zero = jnp.zeros_like(q)
    return jnp.concatenate([jnp.where(lane < HEAD_DIM, q, zero), jnp.where(lane >= HEAD_DIM, q, zero)], axis=0)


def _softmax_step(s, v, m, l, acc):
    m_new = jnp.maximum(m, jnp.max(s, axis=-1, keepdims=True))
    alpha = jnp.exp2(m - m_new)
    p = jnp.exp2(s - m_new)
    l_new = alpha * l + jnp.sum(p, axis=-1, keepdims=True)
    acc_new = alpha * acc + jnp.dot(p.astype(BF16), v, preferred_element_type=F32)
    return m_new, l_new, acc_new


def _diff_lambda(lam_ref, lam_init):
    lv = lam_ref[...]
    s1 = jnp.sum(lv[0:1] * lv[1:2], axis=-1, keepdims=True)
    s2 = jnp.sum(lv[2:3] * lv[3:4], axis=-1, keepdims=True)
    return jnp.exp(s1) - jnp.exp(s2) + lam_init


def _finish_heads(l, acc, lam, subg, lam_init, n):
    o = acc / l
    a = o[:n] - lam * o[n:]
    a = a * lax.rsqrt(jnp.mean(a * a, axis=-1, keepdims=True) + EPS)
    return a * subg * (1.0 - lam_init)


def _qk(qs, k):
    return lax.dot_general(qs, k, (((1,), (1,)), ((), ())), preferred_element_type=F32)


def _attn_prompt_kernel(qt_ref, k_ref, vt_ref, bias_ref, lam_ref, subg_ref, o_ref,
                        p_s, alpha_s, r_s, l_s, acc_s, *, bq, bk, lam_init):
    qi = pl.program_id(2)
    halves = bq // ATTN_STRIP
    strips = [(c, h) for c in range(2) for h in range(halves)]
    n_near = bq // bk + 1
    qt = qt_ref[0]
    row = lax.broadcasted_iota(jnp.int32, qt.shape, 0)
    zero = jnp.zeros_like(qt)
    qs = (jnp.where(row < HEAD_DIM, qt, zero), jnp.where(row >= HEAD_DIM, qt, zero))

    def keys(j):
        return k_ref[0, pl.ds(pl.multiple_of(j * bk, bk), bk), :]

    def values(j):
        return vt_ref[0, :, pl.ds(pl.multiple_of(j * bk, bk), bk)]

    def logits(k, n, bias):
        c, h = strips[n]
        cols = slice(h * ATTN_STRIP, (h + 1) * ATTN_STRIP)
        s = jnp.dot(k, qs[c][:, cols], preferred_element_type=F32)
        if isinstance(bias, tuple):
            s = s + bias_ref[0, pl.ds(bias[0], bias[1]), cols]
        elif bias is not None:
            s = s + bias
        return s

    def beat(new=None, old=None):
        if new is not None:
            k = keys(new[0])
        if old is not None:
            vt1 = values(old[0])
        for n in range(len(strips)):
            if new is not None and new[3][n]:
                _, slot, bias, live = new
                if isinstance(bias, tuple):
                    bias = (bias[0], live[n])
                s = logits(k[:live[n]], n, bias)
                r = r_s[n]
                p = jnp.exp2(s - r)
                p_s[slot, n, :live[n]] = p.astype(BF16)
                r_new = jnp.maximum(r, jnp.max(s, axis=0, keepdims=True))
                alpha = jnp.exp2(r - r_new)
                alpha_s[slot, n] = alpha
                l_s[n] = (l_s[n] + jnp.sum(p, axis=0, keepdims=True)) * alpha
                r_s[n] = r_new
            if old is not None and old[2][n]:
                _, slot, live = old
                pv = jnp.dot(vt1[:, :live[n]], p_s[slot, n, :live[n]], preferred_element_type=F32)
                acc_s[n] = (acc_s[n] + pv) * alpha_s[slot, n]

    n_far = jnp.maximum((qi * bq - MAX_DISTANCE) // bk, 0)
    first = jnp.where(qi == 0, 1, 0)
    near_bias = [(pl.multiple_of((i + first) * bk, bk), bk) for i in range(n_near)]

    k0 = k_ref[0, :R_INIT_KEYS, :]
    for n, (_, h) in enumerate(strips):
        bias0 = jnp.where(qi == 0, bias_ref[0, bk:bk + R_INIT_KEYS, h * ATTN_STRIP:(h + 1) * ATTN_STRIP], 0.0)
        r_s[n] = jnp.max(logits(k0, n, bias0), axis=0, keepdims=True)
    acc_s[...] = jnp.zeros(acc_s.shape, F32)
    l_s[...] = jnp.zeros(l_s.shape, F32)
    alpha_s[1] = jnp.ones(alpha_s.shape[1:], F32)
    p_s[1] = jnp.zeros(p_s.shape[1:], BF16)

    n_pairs = n_far // 2

    all_keys = (bk,) * len(strips)

    def pair(j):
        beat(new=(j, 0, None, all_keys), old=(jnp.maximum(j - 1, 0), 1, all_keys))
        beat(new=(j + 1, 1, None, all_keys), old=(j, 0, all_keys))

    n_quads = n_far // 4

    @pl.loop(0, n_quads)
    def _(t):
        pair(4 * t)
        pair(4 * t + 2)

    @pl.loop(2 * n_quads, n_pairs)
    def _(t):
        pair(2 * t)

    x = 2 * n_pairs
    tail = [(x, jnp.where(n_far % 2 == 1, 0.0, MASK_VALUE), all_keys)]
    for i, bias_i in enumerate(near_bias):
        live = tuple(min(max((h + 1) * ATTN_STRIP - (i - 1) * bk, 0), bk) for _, h in strips)
        tail.append((n_far + i, bias_i, live))
    for t, (j, bias_t, live) in enumerate(tail):
        prev = (jnp.maximum(x - 1, 0), 1, all_keys) if t == 0 else (tail[t - 1][0], (t - 1) % 2, tail[t - 1][2])
        beat(new=(j, t % 2, bias_t, live), old=prev)
    beat(old=(tail[-1][0], (len(tail) - 1) % 2, tail[-1][2]))

    lam = _diff_lambda(lam_ref, lam_init)

    def emit_output():
        for h in range(halves):
            o = []
            for c in range(2):
                o.append(acc_s[c * halves + h] / l_s[c * halves + h])
            a = o[0] - lam * o[1]
            a = a * lax.rsqrt(jnp.mean(a * a, axis=0, keepdims=True) + EPS)
            a = a * subg_ref[...] * (1.0 - lam_init)
            o_ref[0, h * ATTN_STRIP:(h + 1) * ATTN_STRIP, :] = a.T.astype(o_ref.dtype)

    emit_output()

    l_all = l_s[...]
    bad = jnp.where(jnp.abs(acc_s[...]) < jnp.inf, 0.0, 1.0)
    bad = jnp.maximum(bad, jnp.where(jnp.logical_and(l_all > 0.0, l_all < jnp.inf), 0.0, 1.0))
    bad = jnp.max(jnp.max(bad, axis=0), axis=0, keepdims=True)
    bad = jnp.max(bad, axis=1, keepdims=True)[0, 0]

    @pl.when(bad > 0.0)
    def _():
        r_s[...] = jnp.full(r_s.shape, MASK_VALUE, F32)
        acc_s[...] = jnp.zeros(acc_s.shape, F32)
        l_s[...] = jnp.zeros(l_s.shape, F32)

        @pl.loop(0, n_far + n_near)
        def _(j):
            start = pl.multiple_of(jnp.maximum(j - n_far + first, 0) * bk, bk)
            k = keys(j)
            vt1 = values(j)
            for n, (_, h) in enumerate(strips):
                tile = bias_ref[0, pl.ds(start, bk), h * ATTN_STRIP:(h + 1) * ATTN_STRIP]
                s = logits(k, n, jnp.where(j < n_far, 0.0, tile))
                m_old = r_s[n]
                m_new = jnp.maximum(m_old, jnp.max(s, axis=0, keepdims=True))
                p = jnp.exp2(s - m_new)
                scale = jnp.exp2(m_old - m_new)
                acc_s[n] = scale * acc_s[n] + jnp.dot(vt1, p.astype(BF16), preferred_element_type=F32)
                l_s[n] = scale * l_s[n] + jnp.sum(p, axis=0, keepdims=True)
                r_s[n] = m_new

        emit_output()


def _attn_prompt(qt, k, vt, bias, lam_vecs, sub_g_col, *, bq, bk, lam_init):
    b, s, _ = k.shape
    n_strips = 2 * (bq // ATTN_STRIP)
    assert bq % bk == 0 and bk >= MAX_DISTANCE and s >= bq + bk
    return pl.pallas_call(
        functools.partial(_attn_prompt_kernel, bq=bq, bk=bk, lam_init=lam_init),
        grid=(b, N_HEADS, s // bq),
        in_specs=[
            pl.BlockSpec((1, LANES, bq), lambda bi, h, qi: (bi, h, qi)),
            pl.BlockSpec((1, s, LANES), lambda bi, h, qi: (bi, 0, h)),
            pl.BlockSpec((1, V_HEAD_DIM, s), lambda bi, h, qi: (bi, h, 0)),
            pl.BlockSpec((1, bq + 2 * bk, bq), lambda bi, h, qi: (h, 0, 0)),
            pl.BlockSpec((4, HEAD_DIM), lambda bi, h, qi: (0, 0)),
            pl.BlockSpec((V_HEAD_DIM, 1), lambda bi, h, qi: (0, 0)),
        ],
        out_specs=pl.BlockSpec((1, bq, LANES), lambda bi, h, qi: (bi, qi, h)),
        out_shape=jax.ShapeDtypeStruct((b, s, N_HEADS * V_HEAD_DIM), BF16),
        scratch_shapes=[
            pltpu.VMEM((2, n_strips, bk, ATTN_STRIP), BF16),
            pltpu.VMEM((2, n_strips, 1, ATTN_STRIP), F32),
            pltpu.VMEM((n_strips, 1, ATTN_STRIP), F32),
            pltpu.VMEM((n_strips, 1, ATTN_STRIP), F32),
            pltpu.VMEM((n_strips, V_HEAD_DIM, ATTN_STRIP), F32),
        ],
        compiler_params=_params(("arbitrary", "arbitrary", "arbitrary")),
        name="attn_prompt",
    )(qt, k, vt, bias, lam_vecs, sub_g_col)


def _attn_sample_kernel(q_ref, ck_ref, cv_ref, nk_ref, nv_ref, bias_ref, lam_ref, subg_ref, o_ref,
                        *, past, lam_init):
    t = q_ref.shape[1]
    lam = _diff_lambda(lam_ref, lam_init)
    for hd in range(N_HEADS):
        cs = slice(hd * LANES, (hd + 1) * LANES)
        qs = _split_q(q_ref[0, :, cs])
        bias = bias_ref[hd]
        bias2 = jnp.concatenate([bias, bias], axis=0)
        carry = (jnp.full((2 * t, 1), MASK_VALUE, F32), jnp.zeros((2 * t, 1), F32),
                 jnp.zeros((2 * t, V_HEAD_DIM), F32))
        kc = ck_ref[0, pl.ds(hd, past, stride=N_HEADS), :].astype(BF16)
        vc = cv_ref[0, pl.ds(hd, past, stride=N_HEADS), :].astype(BF16)
        carry = _softmax_step(_qk(qs, kc) + bias2[:, :past], vc, *carry)
        m, l, acc = _softmax_step(_qk(qs, nk_ref[0, :, cs]) + bias2[:, past:], nv_ref[0, :, cs], *carry)
        o_ref[0, :, cs] = _finish_heads(l, acc, lam, subg_ref[...], lam_init, t).astype(o_ref.dtype)


def _attn_sample(q, cache_k, cache_v, new_k, new_v, bias, lam_vecs, sub_g, *, lam_init):
    b, t, width = q.shape
    past = cache_k.shape[1] // N_HEADS
    flat_spec = pl.BlockSpec((1, t, width), lambda bi: (bi, 0, 0))
    cache_spec = pl.BlockSpec((1, past * N_HEADS, LANES), lambda bi: (bi, 0, 0))
    return pl.pallas_call(
        functools.partial(_attn_sample_kernel, past=past, lam_init=lam_init),
        grid=(b,),
        in_specs=[
            flat_spec, cache_spec, cache_spec, flat_spec, flat_spec,
            pl.BlockSpec((N_HEADS, t, past + t), lambda bi: (0, 0, 0)),
            pl.BlockSpec((4, HEAD_DIM), lambda bi: (0, 0)),
            pl.BlockSpec((1, V_HEAD_DIM), lambda bi: (0, 0)),
        ],
        out_specs=flat_spec,
        out_shape=jax.ShapeDtypeStruct((b, t, width), BF16),
        compiler_params=_params(("arbitrary",)),
        name="attn_sample",
    )(q, cache_k, cache_v, new_k, new_v, bias, lam_vecs, sub_g)


def _mixer_out_kernel(x_ref, m_ref, a_ref, gt_ref, w_ref, o_ref):
    half = m_ref.shape[-1]
    y = jnp.dot(m_ref[...], w_ref[:half, :], preferred_element_type=F32)
    y = y + jnp.dot(a_ref[...], w_ref[half:, :], preferred_element_type=F32)
    nb, r, d = x_ref.shape
    o_ref[...] = x_ref[...] + gt_ref[...] * y.reshape(nb, r, d)


def _mixer_out(x, m, a, gate, w_out_b, *, nb_blk, r_blk):
    nbat, r, d = x.shape
    nr = r // r_blk
    rows = nb_blk * r_blk
    half = m.shape[-1]
    return pl.pallas_call(
        _mixer_out_kernel,
        grid=((nbat // nb_blk) * nr,),
        in_specs=[
            pl.BlockSpec((nb_blk, r_blk, d), lambda i: (i // nr, i % nr, 0)),
            pl.BlockSpec((rows, half), lambda i: (i, 0)),
            pl.BlockSpec((rows, half), lambda i: (i, 0)),
            pl.BlockSpec((nb_blk, 1, d), lambda i: (i // nr, 0, 0)),
            pl.BlockSpec((d, d), lambda i: (0, 0)),
        ],
        out_specs=pl.BlockSpec((nb_blk, r_blk, d), lambda i: (i // nr, i % nr, 0)),
        out_shape=jax.ShapeDtypeStruct(x.shape, F32),
        compiler_params=_params(("arbitrary",)),
        name="mixer_out",
    )(x, m, a, gate[:, None, :], w_out_b)


def _row_subtiles(nb, r, count):
    if nb >= count:
        per = nb // count
        return [(slice(s * per, (s + 1) * per), slice(None), slice(s * per * r, (s + 1) * per * r))
                for s in range(count)]
    per = r // count
    return [(slice(None), slice(s * per, (s + 1) * per), slice(s * per, (s + 1) * per)) for s in range(count)]


def _ffn_kernel(x_ref, sh_ref, sc_ref, gt_ref, g_ref, wg_ref, wu_ref, wo_ref, gf_ref, shf_ref, scf_ref,
                o_ref, h_s):
    f = pl.program_id(1)
    last = pl.num_programs(1) - 1
    nb, r, d = x_ref.shape
    subs = _row_subtiles(nb, r, FFN_ROW_GROUPS)

    def normalize(sub):
        bsl, rsl, rows = sub
        x = x_ref[bsl, rsl, :]
        y = x * lax.rsqrt(jnp.mean(x * x, axis=-1, keepdims=True) + EPS)
        hm = (y * g_ref[...]) * (1.0 + sc_ref[bsl]) + sh_ref[bsl]
        h_s[rows, :] = hm.reshape(-1, d).astype(BF16)

    def swiglu(rows, shape):
        h = h_s[rows, :]
        zg = jnp.dot(h, wg_ref[...], preferred_element_type=F32)
        zu = jnp.dot(h, wu_ref[...], preferred_element_type=F32)
        act = (zg * jax.nn.sigmoid(zg) * zu).astype(BF16)
        return jnp.dot(act, wo_ref[...], preferred_element_type=F32).reshape(shape)

    def finish(sub):
        bsl, rsl, rows = sub
        x = x_ref[bsl, rsl, :]
        x2 = x + gt_ref[bsl] * (o_ref[bsl, rsl, :] + swiglu(rows, x.shape))
        y = x2 * lax.rsqrt(jnp.mean(x2 * x2, axis=-1, keepdims=True) + EPS)
        o_ref[bsl, rsl, :] = (y * gf_ref[...]) * (1.0 + scf_ref[bsl]) + shf_ref[bsl]

    @pl.when(f == 0)
    def _():
        for bsl, rsl, rows in subs:
            normalize((bsl, rsl, rows))
            o_ref[bsl, rsl, :] = swiglu(rows, o_ref[bsl, rsl, :].shape)

    @pl.when(jnp.logical_and(f > 0, f < last))
    def _():
        o_ref[...] += swiglu(slice(None), o_ref.shape)

    @pl.when(f == last)
    def _():
        for sub in subs:
            finish(sub)


def _ffn(x, shift, scale, gate, g_ffn, w_in_b, w_out_b, g_final, shift_f, scale_f, *, nb_blk, r_blk, tf):
    nbat, r, d = x.shape
    d_ff = w_out_b.shape[0]
    nf = d_ff // tf
    assert nf >= 2, "first and last d_ff steps are distinct code paths"
    nr = r // r_blk
    rows = nb_blk * r_blk
    x_spec = pl.BlockSpec((nb_blk, r_blk, d), lambda i, f: (i // nr, i % nr, 0))
    vec_spec = pl.BlockSpec((nb_blk, 1, d), lambda i, f: (i // nr, 0, 0))
    par_spec = pl.BlockSpec((1, 1, d), lambda i, f: (0, 0, 0))
    return pl.pallas_call(
        _ffn_kernel,
        grid=((nbat // nb_blk) * nr, nf),
        in_specs=[
            x_spec, vec_spec, vec_spec, vec_spec, par_spec,
            pl.BlockSpec((d, tf), lambda i, f: (0, f)),
            pl.BlockSpec((d, tf), lambda i, f: (0, f + nf)),
            pl.BlockSpec((tf, d), lambda i, f: (f, 0)),
            par_spec, vec_spec, vec_spec,
        ],
        out_specs=x_spec,
        out_shape=jax.ShapeDtypeStruct(x.shape, F32),
        scratch_shapes=[pltpu.VMEM((rows, d), BF16)],
        compiler_params=_params(("arbitrary", "arbitrary")),
        name="ffn",
    )(x, shift[:, None, :], scale[:, None, :], gate[:, None, :], g_ffn.reshape(1, 1, d),
      w_in_b, w_in_b, w_out_b, g_final.reshape(1, 1, d), shift_f[:, None, :], scale_f[:, None, :])


ATTN_Q_BLOCK = 1024
ATTN_K_BEAT = 512
PROMPT_ROWS = 512
SAMPLE_BATCH_BLOCK = 8
FFN_TILE = 512
FFN_ROWS = 1024
FFN_ROW_GROUPS = 2


def kernel(x_prompt, x_sample, cache_k, cache_v, c_prompt, c_sample, rel_bias, w_ada, b_ada, w_ada_final,
           b_ada_final, g_mix, g_ffn, g_final, w_in, mlp_ln_g, mlp_ln_b, w_s, b_s, lambda_q1, lambda_k1,
           lambda_q2, lambda_k2, sub_g, w_out, w_ffn_in, w_ffn_out):
    B, S, D = x_prompt.shape
    DB, T, _ = x_sample.shape
    depth = w_in.shape[0]
    past = cache_k.shape[2]
    width = N_HEADS * V_HEAD_DIM
    mlp_chunk = w_s.shape[-1]

    c_all = jnp.concatenate([c_prompt, c_sample], axis=0)
    mod_f = _adaln(c_all, w_ada_final, b_ada_final)
    bias_p = _rel_bias_tiles(rel_bias, ATTN_Q_BLOCK, ATTN_Q_BLOCK + 2 * ATTN_K_BEAT, ATTN_K_BEAT, 0, True, True)
    bias_s = _rel_bias_tiles(rel_bias, T, past + T, past, 0, False, False)

    assert depth == 1, "the final adaLN norm is fused into the single layer's FFN kernel"
    lam_init = 0.8 - 0.6 * math.exp(-0.3 * 0)
    lam_vecs = jnp.stack([lambda_q1[0], lambda_k1[0], lambda_q2[0], lambda_k2[0]])
    subg = sub_g.reshape(1, V_HEAD_DIM)
    mod = _adaln(c_all, w_ada[0], b_ada[0])
    sh1, sc1, gt1, sh2, sc2, gt2 = jnp.split(mod, 6, axis=-1)
    shf, scf = jnp.split(mod_f, 2, axis=-1)
    w_in_b = w_in[0].astype(BF16)
    w_out_b = w_out[0].astype(BF16)
    w_f_in_b = w_ffn_in[0].astype(BF16)
    w_f_out_b = w_ffn_out[0].astype(BF16)
    mixer_w = (g_mix[0], w_in_b, mlp_ln_g[0], mlp_ln_b[0], w_s[0], b_s[0])

    m, qt, kp, kb, vp, vt = _mixer_in(x_prompt, sh1[:B], sc1[:B], *mixer_w, nb_blk=1, r_blk=PROMPT_ROWS,
                                     t_chunk=mlp_chunk, emit_gv=False, v_transposed=True)
    a = _attn_prompt(qt, kb.reshape(B, S, width), vt, bias_p, lam_vecs,
                     subg.reshape(V_HEAD_DIM, 1), bq=ATTN_Q_BLOCK, bk=ATTN_K_BEAT, lam_init=lam_init)
    xp = _mixer_out(x_prompt, m, a.reshape(B * S, width), gt1[:B], w_out_b, nb_blk=1, r_blk=PROMPT_ROWS)
    yp = _ffn(xp, sh2[:B], sc2[:B], gt2[:B], g_ffn[0], w_f_in_b, w_f_out_b, g_final, shf[:B], scf[:B],
              nb_blk=1, r_blk=FFN_ROWS, tf=FFN_TILE)

    m, q, ks, kb, vs, vb, gvs = _mixer_in(x_sample, sh1[B:], sc1[B:], *mixer_w, nb_blk=SAMPLE_BATCH_BLOCK,
                                          r_blk=T, t_chunk=T, emit_gv=True, v_transposed=False)
    a = _attn_sample(q.reshape(DB, T, width), cache_k.reshape(DB, past * N_HEADS, LANES),
                     cache_v.reshape(DB, past * N_HEADS, LANES), kb.reshape(DB, T, width),
                     vb.reshape(DB, T, width), bias_s, lam_vecs, subg, lam_init=lam_init)
    xs = _mixer_out(x_sample, m, a.reshape(DB * T, width), gt1[B:], w_out_b, nb_blk=SAMPLE_BATCH_BLOCK, r_blk=T)
    ys = _ffn(xs, sh2[B:], sc2[B:], gt2[B:], g_ffn[0], w_f_in_b, w_f_out_b, g_final, shf[B:], scf[B:],
              nb_blk=DB, r_blk=T, tf=FFN_TILE)

    head_shape = (N_HEADS, V_HEAD_DIM)
    return (yp, ys, kp.reshape(1, B, S, *head_shape), vp.reshape(1, B, S, *head_shape),
            ks.reshape(1, DB, T, *head_shape), vs.reshape(1, DB, T, *head_shape),
            gvs.reshape(1, DB, T, MLP_GROUPS, MLP_GROUP_DIM))
```

```python
import functools
import math

import jax
import jax.numpy as jnp
from jax import lax
from jax.experimental import pallas as pl
from jax.experimental.pallas import tpu as pltpu

LANES = 128
SUBLANES = 8
VMEM_LIMIT_BYTES = 56 * 1024 * 1024
MXU_WIDTH = 256

ATTN_STRIP = MXU_WIDTH
R_INIT_KEYS = 128

CHUNK = 64
N_HEADS = 8
HEAD_DIM = 64
V_HEAD_DIM = 128
MLP_GROUPS = 8
MLP_GROUP_DIM = 128
N_BUCKETS = 32
MAX_DISTANCE = 128
EPS = 1e-6
MASK_VALUE = -1e30
LOG2E = math.log2(math.e)

BF16 = jnp.bfloat16
F32 = jnp.float32


def _params(sem):
    return pltpu.CompilerParams(dimension_semantics=sem, vmem_limit_bytes=VMEM_LIMIT_BYTES)


def _adaln_kernel(c_ref, w_ref, b_ref, o_ref):
    c = c_ref[...]
    a = c * jax.nn.sigmoid(c)
    o_ref[...] = jnp.dot(a, w_ref[...], preferred_element_type=F32) + b_ref[...]


def _adaln(c, w, b, tn=1024):
    rows, d = c.shape
    n = w.shape[1]
    return pl.pallas_call(
        _adaln_kernel,
        grid=(n // tn,),
        in_specs=[
            pl.BlockSpec((rows, d), lambda j: (0, 0)),
            pl.BlockSpec((d, tn), lambda j: (0, j)),
            pl.BlockSpec((1, tn), lambda j: (0, j)),
        ],
        out_specs=pl.BlockSpec((rows, tn), lambda j: (0, j)),
        out_shape=jax.ShapeDtypeStruct((rows, n), F32),
        compiler_params=_params(("arbitrary",)),
        name="adaln",
    )(c, w, b.reshape(1, n))


def _rel_bias_kernel(tab_ref, o_ref, *, nq, nk, q_start, k_start, shift_far, keys_on_rows):
    h = pl.program_id(0)
    nb = N_BUCKETS // 2
    max_exact = nb // 2
    q_axis, k_axis = (1, 0) if keys_on_rows else (0, 1)
    rows, cols = (nk, nq) if keys_on_rows else (nq, nk)
    shift = tab_ref[nb - 1, h] if shift_far else 0.0

    def block(shape, q0, k0):
        q_pos = q0 + lax.broadcasted_iota(jnp.int32, shape, q_axis)
        k_pos = k0 + lax.broadcasted_iota(jnp.int32, shape, k_axis)
        rel = k_pos - q_pos
        ret = jnp.where(rel > 0, nb, 0)
        n = jnp.abs(rel)
        nf = jnp.maximum(n, 1).astype(F32)
        large = max_exact + (jnp.log(nf / max_exact) / math.log(MAX_DISTANCE / max_exact)
                             * (nb - max_exact)).astype(jnp.int32)
        large = jnp.minimum(large, nb - 1)
        bucket = ret + jnp.where(n < max_exact, n, large)
        bias = jnp.zeros(shape, F32)
        for bkt in range(N_BUCKETS):
            bias = jnp.where(bucket == bkt, tab_ref[bkt, h], bias)
        allowed = (k_pos // CHUNK) <= (q_pos // CHUNK)
        return jnp.where(allowed, (bias - shift) * LOG2E, MASK_VALUE)

    if rows % LANES or cols % LANES:
        o_ref[0] = block((rows, cols), q_start, k_start)
        return
    far_value = (tab_ref[nb - 1, h] - shift) * LOG2E
    for rb in range(rows // LANES):
        for cb in range(cols // LANES):
            k0 = k_start + LANES * (rb if keys_on_rows else cb)
            q0 = q_start + LANES * (cb if keys_on_rows else rb)
            sl = (0, slice(rb * LANES, (rb + 1) * LANES), slice(cb * LANES, (cb + 1) * LANES))
            if (k0 - q0) + (LANES - 1) <= -MAX_DISTANCE:
                o_ref[sl] = jnp.full((LANES, LANES), far_value, F32)
            elif (k0 - q0) - (LANES - 1) >= CHUNK:
                o_ref[sl] = jnp.full((LANES, LANES), MASK_VALUE, F32)
            else:
                o_ref[sl] = block((LANES, LANES), q0, k0)


def _rel_bias_tiles(rel_bias, nq, nk, q_start, k_start, shift_far, keys_on_rows):
    out_tile = (nk, nq) if keys_on_rows else (nq, nk)
    return pl.pallas_call(
        functools.partial(_rel_bias_kernel, nq=nq, nk=nk, q_start=q_start, k_start=k_start,
                          shift_far=shift_far, keys_on_rows=keys_on_rows),
        grid=(N_HEADS,),
        in_specs=[pl.BlockSpec(memory_space=pltpu.SMEM)],
        out_specs=pl.BlockSpec((1,) + out_tile, lambda h: (h, 0, 0)),
        out_shape=jax.ShapeDtypeStruct((N_HEADS,) + out_tile, F32),
        compiler_params=_params(("arbitrary",)),
        name="rel_bias",
    )(rel_bias)


def _mixer_in_kernel(x_ref, sh_ref, sc_ref, g_ref, w_ref, lng_ref, lnb_ref, ws_ref, bs_ref,
                     *refs, t_chunk, emit_gv, v_transposed):
    if emit_gv:
        m_ref, q_ref, k_ref, kb_ref, v_ref, vb_ref, gv_ref, u_s = refs
    else:
        m_ref, q_ref, k_ref, kb_ref, v_ref, vb_ref, u_s = refs
        gv_ref = None
    rows = u_s.shape[0]
    width = u_s.shape[1]

    def store_per_head(ref, val):
        for hd in range(N_HEADS):
            ref[pl.ds(hd, rows, stride=N_HEADS), :] = val[:, hd * LANES:(hd + 1) * LANES]

    x = x_ref[...]
    y = x * lax.rsqrt(jnp.mean(x * x, axis=-1, keepdims=True) + EPS)
    y = y * g_ref[...]
    hm = y * (1.0 + sc_ref[...]) + sh_ref[...]
    h = hm.reshape(rows, hm.shape[-1]).astype(BF16)

    def project(seg):
        return jnp.dot(h, w_ref[:, seg * width:(seg + 1) * width], preferred_element_type=F32)

    u_s[...] = jax.nn.gelu(project(0))

    g = jax.nn.gelu(project(1))
    mu = jnp.mean(g, axis=-1, keepdims=True)
    var = jnp.mean(jnp.square(g - mu), axis=-1, keepdims=True)
    gv = (g - mu) * lax.rsqrt(var + EPS) * lng_ref[...] + lnb_ref[...]
    if emit_gv:
        store_per_head(gv_ref, gv)
    gvb = gv.astype(BF16)
    ii = lax.broadcasted_iota(jnp.int32, (t_chunk, t_chunk), 0)
    jj = lax.broadcasted_iota(jnp.int32, (t_chunk, t_chunk), 1)
    mask = (jj // CHUNK) <= (ii // CHUNK)
    for grp in range(MLP_GROUPS):
        wg = jnp.where(mask, ws_ref[grp], 0.0).astype(BF16)
        bg = bs_ref[grp]
        cs = slice(grp * MLP_GROUP_DIM, (grp + 1) * MLP_GROUP_DIM)
        chunks = [slice(c * t_chunk, (c + 1) * t_chunk) for c in range(rows // t_chunk)]
        mixed = jnp.dot(wg, jnp.concatenate([gvb[rs, cs] for rs in chunks], axis=1),
                        preferred_element_type=F32) + bg
        for c, rs in enumerate(chunks):
            m_ref[rs, cs] = (u_s[rs, cs] * mixed[:, c * MLP_GROUP_DIM:(c + 1) * MLP_GROUP_DIM]).astype(BF16)

    z = project(2) * (HEAD_DIM ** -0.5 * LOG2E)
    if v_transposed:
        q_ref[0] = z.T.astype(BF16)
    else:
        q_ref[...] = z.astype(BF16)

    z = project(3)
    store_per_head(k_ref, z)
    kb_ref[...] = z.astype(BF16)

    z = project(4)
    store_per_head(v_ref, z)
    if v_transposed:
        vb_ref[0] = z.T.astype(BF16)
    else:
        vb_ref[...] = z.astype(BF16)


def _mixer_in(x, shift, scale, g_mix, w_in_b, ln_g, ln_b, w_s, b_s, *, nb_blk, r_blk, t_chunk, emit_gv,
              v_transposed):
    nbat, r, d = x.shape
    width = d // 2
    assert w_in_b.shape[1] == 5 * width
    nr = r // r_blk
    n_tiles = (nbat // nb_blk) * nr
    rows = nb_blk * r_blk
    tokens = nbat * r
    row_idx = lambda i: (i, 0)
    flat = jax.ShapeDtypeStruct((tokens, width), BF16)
    per_head = jax.ShapeDtypeStruct((tokens * N_HEADS, LANES), F32)
    flat_spec = pl.BlockSpec((rows, width), row_idx)
    per_head_spec = pl.BlockSpec((rows * N_HEADS, LANES), row_idx)
    out_shape = [flat, flat, per_head, flat, per_head, flat]
    out_specs = [flat_spec, flat_spec, per_head_spec, flat_spec, per_head_spec, flat_spec]
    if v_transposed:
        assert nb_blk == 1
        for idx in (1, 5):
            out_shape[idx] = jax.ShapeDtypeStruct((nbat, width, r), BF16)
            out_specs[idx] = pl.BlockSpec((1, width, r_blk), lambda i: (i // nr, 0, i % nr))
    if emit_gv:
        out_shape.append(per_head)
        out_specs.append(per_head_spec)
    ws_t = w_s[:, :t_chunk, :t_chunk]
    bs_t = b_s[:, :t_chunk, None]
    once = pl.Buffered(1)
    return pl.pallas_call(
        functools.partial(_mixer_in_kernel, t_chunk=t_chunk, emit_gv=emit_gv, v_transposed=v_transposed),
        grid=(n_tiles,),
        in_specs=[
            pl.BlockSpec((nb_blk, r_blk, d), lambda i: (i // nr, i % nr, 0)),
            pl.BlockSpec((nb_blk, 1, d), lambda i: (i // nr, 0, 0)),
            pl.BlockSpec((nb_blk, 1, d), lambda i: (i // nr, 0, 0)),
            pl.BlockSpec((1, 1, d), lambda i: (0, 0, 0)),
            pl.BlockSpec(w_in_b.shape, lambda i: (0, 0), pipeline_mode=once),
            pl.BlockSpec((1, width), lambda i: (0, 0)),
            pl.BlockSpec((1, width), lambda i: (0, 0)),
            pl.BlockSpec((MLP_GROUPS, t_chunk, t_chunk), lambda i: (0, 0, 0)),
            pl.BlockSpec((MLP_GROUPS, t_chunk, 1), lambda i: (0, 0, 0)),
        ],
        out_specs=out_specs,
        out_shape=out_shape,
        scratch_shapes=[pltpu.VMEM((rows, width), F32)],
        compiler_params=_params(("arbitrary",)),
        name="mixer_in",
    )(x, shift[:, None, :], scale[:, None, :], g_mix.reshape(1, 1, d), w_in_b,
      ln_g.reshape(1, width), ln_b.reshape(1, width), ws_t, bs_t)


def _split_q(q):
    lane = lax.broadcasted_iota(jnp.int32, q.shape, 1)
    zero = jnp.zeros_like(q)
    return jnp.concatenate([jnp.where(lane < HEAD_DIM, q, zero), jnp.where(lane >= HEAD_DIM, q, zero)], axis=0)


def _softmax_step(s, v, m, l, acc):
    m_new = jnp.maximum(m, jnp.max(s, axis=-1, keepdims=True))
    alpha = jnp.exp2(m - m_new)
    p = jnp.exp2(s - m_new)
    l_new = alpha * l + jnp.sum(p, axis=-1, keepdims=True)
    acc_new = alpha * acc + jnp.dot(p.astype(BF16), v, preferred_element_type=F32)
    return m_new, l_new, acc_new


def _diff_lambda(lam_ref, lam_init):
    lv = lam_ref[...]
    s1 = jnp.sum(lv[0:1] * lv[1:2], axis=-1, keepdims=True)
    s2 = jnp.sum(lv[2:3] * lv[3:4], axis=-1, keepdims=True)
    return jnp.exp(s1) - jnp.exp(s2) + lam_init


def _finish_heads(l, acc, lam, subg, lam_init, n):
    o = acc / l
    a = o[:n] - lam * o[n:]
    a = a * lax.rsqrt(jnp.mean(a * a, axis=-1, keepdims=True) + EPS)
    return a * subg * (1.0 - lam_init)


def _qk(qs, k):
    return lax.dot_general(qs, k, (((1,), (1,)), ((), ())), preferred_element_type=F32)


def _attn_prompt_kernel(qt_ref, k_ref, vt_ref, bias_ref, lam_ref, subg_ref, o_ref,
                        p_s, alpha_s, r_s, l_s, acc_s, *, bq, bk, lam_init):
    qi = pl.program_id(2)
    halves = bq // ATTN_STRIP
    strips = [(c, h) for c in range(2) for h in range(halves)]
    n_near = bq // bk + 1
    qt = qt_ref[0]
    row = lax.broadcasted_iota(jnp.int32, qt.shape, 0)
    zero = jnp.zeros_like(qt)
    qs = (jnp.where(row < HEAD_DIM, qt, zero), jnp.where(row >= HEAD_DIM, qt, zero))

    def keys(j):
        return k_ref[0, pl.ds(pl.multiple_of(j * bk, bk), bk), :]

    def values(j):
        return vt_ref[0, :, pl.ds(pl.multiple_of(j * bk, bk), bk)]

    def logits(k, n, bias):
        c, h = strips[n]
        cols = slice(h * ATTN_STRIP, (h + 1) * ATTN_STRIP)
        s = jnp.dot(k, qs[c][:, cols], preferred_element_type=F32)
        if isinstance(bias, tuple):
            s = s + bias_ref[0, pl.ds(bias[0], bias[1]), cols]
        elif bias is not None:
            s = s + bias
        return s

    def beat(new=None, old=None):
        if new is not None:
            k = keys(new[0])
        if old is not None:
            vt1 = values(old[0])
        for n in range(len(strips)):
            if new is not None and new[3][n]:
                _, slot, bias, live = new
                if isinstance(bias, tuple):
                    bias = (bias[0], live[n])
                s = logits(k[:live[n]], n, bias)
                r = r_s[n]
                p = jnp.exp2(s - r)
                p_s[slot, n, :live[n]] = p.astype(BF16)
                r_new = jnp.maximum(r, jnp.max(s, axis=0, keepdims=True))
                alpha = jnp.exp2(r - r_new)
                alpha_s[slot, n] = alpha
                l_s[n] = (l_s[n] + jnp.sum(p, axis=0, keepdims=True)) * alpha
                r_s[n] = r_new
            if old is not None and old[2][n]:
                _, slot, live = old
                pv = jnp.dot(vt1[:, :live[n]], p_s[slot, n, :live[n]], preferred_element_type=F32)
                acc_s[n] = (acc_s[n] + pv) * alpha_s[slot, n]

    n_far = jnp.maximum((qi * bq - MAX_DISTANCE) // bk, 0)
    first = jnp.where(qi == 0, 1, 0)
    near_bias = [(pl.multiple_of((i + first) * bk, bk), bk) for i in range(n_near)]

    k0 = k_ref[0, :R_INIT_KEYS, :]
    for n, (_, h) in enumerate(strips):
        bias0 = jnp.where(qi == 0, bias_ref[0, bk:bk + R_INIT_KEYS, h * ATTN_STRIP:(h + 1) * ATTN_STRIP], 0.0)
        r_s[n] = jnp.max(logits(k0, n, bias0), axis=0, keepdims=True)
    acc_s[...] = jnp.zeros(acc_s.shape, F32)
    l_s[...] = jnp.zeros(l_s.shape, F32)
    alpha_s[1] = jnp.ones(alpha_s.shape[1:], F32)
    p_s[1] = jnp.zeros(p_s.shape[1:], BF16)

    n_pairs = n_far // 2

    all_keys = (bk,) * len(strips)

    def pair(j):
        beat(new=(j, 0, None, all_keys), old=(jnp.maximum(j - 1, 0), 1, all_keys))
        beat(new=(j + 1, 1, None, all_keys), old=(j, 0, all_keys))

    n_quads = n_far // 4

    @pl.loop(0, n_quads)
    def _(t):
        pair(4 * t)
        pair(4 * t + 2)

    @pl.loop(2 * n_quads, n_pairs)
    def _(t):
        pair(2 * t)

    x = 2 * n_pairs
    tail = [(x, jnp.where(n_far % 2 == 1, 0.0, MASK_VALUE), all_keys)]
    for i, bias_i in enumerate(near_bias):
        live = tuple(min(max((h + 1) * ATTN_STRIP - (i - 1) * bk, 0), bk) for _, h in strips)
        tail.append((n_far + i, bias_i, live))
    for t, (j, bias_t, live) in enumerate(tail):
        prev = (jnp.maximum(x - 1, 0), 1, all_keys) if t == 0 else (tail[t - 1][0], (t - 1) % 2, tail[t - 1][2])
        beat(new=(j, t % 2, bias_t, live), old=prev)
    beat(old=(tail[-1][0], (len(tail) - 1) % 2, tail[-1][2]))

    lam = _diff_lambda(lam_ref, lam_init)

    def emit_output():
        for h in range(halves):
            o = []
            for c in range(2):
                o.append(acc_s[c * halves + h] / l_s[c * halves + h])
            a = o[0] - lam * o[1]
            a = a * lax.rsqrt(jnp.mean(a * a, axis=0, keepdims=True) + EPS)
            a = a * subg_ref[...] * (1.0 - lam_init)
            o_ref[0, h * ATTN_STRIP:(h + 1) * ATTN_STRIP, :] = a.T.astype(o_ref.dtype)

    emit_output()

    l_all = l_s[...]
    bad = jnp.where(jnp.abs(acc_s[...]) < jnp.inf, 0.0, 1.0)
    bad = jnp.maximum(bad, jnp.where(jnp.logical_and(l_all > 0.0, l_all < jnp.inf), 0.0, 1.0))
    bad = jnp.max(jnp.max(bad, axis=0), axis=0, keepdims=True)
    bad = jnp.max(bad, axis=1, keepdims=True)[0, 0]

    @pl.when(bad > 0.0)
    def _():
        r_s[...] = jnp.full(r_s.shape, MASK_VALUE, F32)
        acc_s[...] = jnp.zeros(acc_s.shape, F32)
        l_s[...] = jnp.zeros(l_s.shape, F32)

        @pl.loop(0, n_far + n_near)
        def _(j):
            start = pl.multiple_of(jnp.maximum(j - n_far + first, 0) * bk, bk)
            k = keys(j)
            vt1 = values(j)
            for n, (_, h) in enumerate(strips):
                tile = bias_ref[0, pl.ds(start, bk), h * ATTN_STRIP:(h + 1) * ATTN_STRIP]
                s = logits(k, n, jnp.where(j < n_far, 0.0, tile))
                m_old = r_s[n]
                m_new = jnp.maximum(m_old, jnp.max(s, axis=0, keepdims=True))
                p = jnp.exp2(s - m_new)
                scale = jnp.exp2(m_old - m_new)
                acc_s[n] = scale * acc_s[n] + jnp.dot(vt1, p.astype(BF16), preferred_element_type=F32)
                l_s[n] = scale * l_s[n] + jnp.sum(p, axis=0, keepdims=True)
                r_s[n] = m_new

        emit_output()


def _attn_prompt(qt, k, vt, bias, lam_vecs, sub_g_col, *, bq, bk, lam_init):
    b, s, _ = k.shape
    n_strips = 2 * (bq // ATTN_STRIP)
    assert bq % bk == 0 and bk >= MAX_DISTANCE and s >= bq + bk
    return pl.pallas_call(
        functools.partial(_attn_prompt_kernel, bq=bq, bk=bk, lam_init=lam_init),
        grid=(b, N_HEADS, s // bq),
        in_specs=[
            pl.BlockSpec((1, LANES, bq), lambda bi, h, qi: (bi, h, qi)),
            pl.BlockSpec((1, s, LANES), lambda bi, h, qi: (bi, 0, h)),
            pl.BlockSpec((1, V_HEAD_DIM, s), lambda bi, h, qi: (bi, h, 0)),
            pl.BlockSpec((1, bq + 2 * bk, bq), lambda bi, h, qi: (h, 0, 0)),
            pl.BlockSpec((4, HEAD_DIM), lambda bi, h, qi: (0, 0)),
            pl.BlockSpec((V_HEAD_DIM, 1), lambda bi, h, qi: (0, 0)),
        ],
        out_specs=pl.BlockSpec((1, bq, LANES), lambda bi, h, qi: (bi, qi, h)),
        out_shape=jax.ShapeDtypeStruct((b, s, N_HEADS * V_HEAD_DIM), BF16),
        scratch_shapes=[
            pltpu.VMEM((2, n_strips, bk, ATTN_STRIP), BF16),
            pltpu.VMEM((2, n_strips, 1, ATTN_STRIP), F32),
            pltpu.VMEM((n_strips, 1, ATTN_STRIP), F32),
            pltpu.VMEM((n_strips, 1, ATTN_STRIP), F32),
            pltpu.VMEM((n_strips, V_HEAD_DIM, ATTN_STRIP), F32),
        ],
        compiler_params=_params(("arbitrary", "arbitrary", "arbitrary")),
        name="attn_prompt",
    )(qt, k, vt, bias, lam_vecs, sub_g_col)


def _attn_sample_kernel(q_ref, ck_ref, cv_ref, nk_ref, nv_ref, bias_ref, lam_ref, subg_ref, o_ref,
                        *, past, lam_init):
    t = q_ref.shape[1]
    lam = _diff_lambda(lam_ref, lam_init)
    for hd in range(N_HEADS):
        cs = slice(hd * LANES, (hd + 1) * LANES)
        qs = _split_q(q_ref[0, :, cs])
        bias = bias_ref[hd]
        bias2 = jnp.concatenate([bias, bias], axis=0)
        carry = (jnp.full((2 * t, 1), MASK_VALUE, F32), jnp.zeros((2 * t, 1), F32),
                 jnp.zeros((2 * t, V_HEAD_DIM), F32))
        kc = ck_ref[0, pl.ds(hd, past, stride=N_HEADS), :].astype(BF16)
        vc = cv_ref[0, pl.ds(hd, past, stride=N_HEADS), :].astype(BF16)
        carry = _softmax_step(_qk(qs, kc) + bias2[:, :past], vc, *carry)
        m, l, acc = _softmax_step(_qk(qs, nk_ref[0, :, cs]) + bias2[:, past:], nv_ref[0, :, cs], *carry)
        o_ref[0, :, cs] = _finish_heads(l, acc, lam, subg_ref[...], lam_init, t).astype(o_ref.dtype)


def _attn_sample(q, cache_k, cache_v, new_k, new_v, bias, lam_vecs, sub_g, *, lam_init):
    b, t, width = q.shape
    past = cache_k.shape[1] // N_HEADS
    flat_spec = pl.BlockSpec((1, t, width), lambda bi: (bi, 0, 0))
    cache_spec = pl.BlockSpec((1, past * N_HEADS, LANES), lambda bi: (bi, 0, 0))
    return pl.pallas_call(
        functools.partial(_attn_sample_kernel, past=past, lam_init=lam_init),
        grid=(b,),
        in_specs=[
            flat_spec, cache_spec, cache_spec, flat_spec, flat_spec,
            pl.BlockSpec((N_HEADS, t, past + t), lambda bi: (0, 0, 0)),
            pl.BlockSpec((4, HEAD_DIM), lambda bi: (0, 0)),
            pl.BlockSpec((1, V_HEAD_DIM), lambda bi: (0, 0)),
        ],
        out_specs=flat_spec,
        out_shape=jax.ShapeDtypeStruct((b, t, width), BF16),
        compiler_params=_params(("arbitrary",)),
        name="attn_sample",
    )(q, cache_k, cache_v, new_k, new_v, bias, lam_vecs, sub_g)


def _mixer_out_kernel(x_ref, m_ref, a_ref, gt_ref, w_ref, o_ref):
    half = m_ref.shape[-1]
    y = jnp.dot(m_ref[...], w_ref[:half, :], preferred_element_type=F32)
    y = y + jnp.dot(a_ref[...], w_ref[half:, :], preferred_element_type=F32)
    nb, r, d = x_ref.shape
    o_ref[...] = x_ref[...] + gt_ref[...] * y.reshape(nb, r, d)


def _mixer_out(x, m, a, gate, w_out_b, *, nb_blk, r_blk):
    nbat, r, d = x.shape
    nr = r // r_blk
    rows = nb_blk * r_blk
    half = m.shape[-1]
    return pl.pallas_call(
        _mixer_out_kernel,
        grid=((nbat // nb_blk) * nr,),
        in_specs=[
            pl.BlockSpec((nb_blk, r_blk, d), lambda i: (i // nr, i % nr, 0)),
            pl.BlockSpec((rows, half), lambda i: (i, 0)),
            pl.BlockSpec((rows, half), lambda i: (i, 0)),
            pl.BlockSpec((nb_blk, 1, d), lambda i: (i // nr, 0, 0)),
            pl.BlockSpec((d, d), lambda i: (0, 0)),
        ],
        out_specs=pl.BlockSpec((nb_blk, r_blk, d), lambda i: (i // nr, i % nr, 0)),
        out_shape=jax.ShapeDtypeStruct(x.shape, F32),
        compiler_params=_params(("arbitrary",)),
        name="mixer_out",
    )(x, m, a, gate[:, None, :], w_out_b)


def _row_subtiles(nb, r, count):
    if nb >= count:
        per = nb // count
        return [(slice(s * per, (s + 1) * per), slice(None), slice(s * per * r, (s + 1) * per * r))
                for s in range(count)]
    per = r // count
    return [(slice(None), slice(s * per, (s + 1) * per), slice(s * per, (s + 1) * per)) for s in range(count)]


def _ffn_kernel(x_ref, sh_ref, sc_ref, gt_ref, g_ref, wg_ref, wu_ref, wo_ref, gf_ref, shf_ref, scf_ref,
                o_ref, h_s):
    f = pl.program_id(1)
    last = pl.num_programs(1) - 1
    nb, r, d = x_ref.shape
    subs = _row_subtiles(nb, r, FFN_ROW_GROUPS)

    def normalize(sub):
        bsl, rsl, rows = sub
        x = x_ref[bsl, rsl, :]
        y = x * lax.rsqrt(jnp.mean(x * x, axis=-1, keepdims=True) + EPS)
        hm = (y * g_ref[...]) * (1.0 + sc_ref[bsl]) + sh_ref[bsl]
        h_s[rows, :] = hm.reshape(-1, d).astype(BF16)

    def swiglu(rows, shape):
        h = h_s[rows, :]
        zg = jnp.dot(h, wg_ref[...], preferred_element_type=F32)
        zu = jnp.dot(h, wu_ref[...], preferred_element_type=F32)
        act = (zg * jax.nn.sigmoid(zg) * zu).astype(BF16)
        return jnp.dot(act, wo_ref[...], preferred_element_type=F32).reshape(shape)

    def finish(sub):
        bsl, rsl, rows = sub
        x = x_ref[bsl, rsl, :]
        x2 = x + gt_ref[bsl] * (o_ref[bsl, rsl, :] + swiglu(rows, x.shape))
        y = x2 * lax.rsqrt(jnp.mean(x2 * x2, axis=-1, keepdims=True) + EPS)
        o_ref[bsl, rsl, :] = (y * gf_ref[...]) * (1.0 + scf_ref[bsl]) + shf_ref[bsl]

    @pl.when(f == 0)
    def _():
        for bsl, rsl, rows in subs:
            normalize((bsl, rsl, rows))
            o_ref[bsl, rsl, :] = swiglu(rows, o_ref[bsl, rsl, :].shape)

    @pl.when(jnp.logical_and(f > 0, f < last))
    def _():
        o_ref[...] += swiglu(slice(None), o_ref.shape)

    @pl.when(f == last)
    def _():
        for sub in subs:
            finish(sub)


def _ffn(x, shift, scale, gate, g_ffn, w_in_b, w_out_b, g_final, shift_f, scale_f, *, nb_blk, r_blk, tf):
    nbat, r, d = x.shape
    d_ff = w_out_b.shape[0]
    nf = d_ff // tf
    assert nf >= 2, "first and last d_ff steps are distinct code paths"
    nr = r // r_blk
    rows = nb_blk * r_blk
    x_spec = pl.BlockSpec((nb_blk, r_blk, d), lambda i, f: (i // nr, i % nr, 0))
    vec_spec = pl.BlockSpec((nb_blk, 1, d), lambda i, f: (i // nr, 0, 0))
    par_spec = pl.BlockSpec((1, 1, d), lambda i, f: (0, 0, 0))
    return pl.pallas_call(
        _ffn_kernel,
        grid=((nbat // nb_blk) * nr, nf),
        in_specs=[
            x_spec, vec_spec, vec_spec, vec_spec, par_spec,
            pl.BlockSpec((d, tf), lambda i, f: (0, f)),
            pl.BlockSpec((d, tf), lambda i, f: (0, f + nf)),
            pl.BlockSpec((tf, d), lambda i, f: (f, 0)),
            par_spec, vec_spec, vec_spec,
        ],
        out_specs=x_spec,
        out_shape=jax.ShapeDtypeStruct(x.shape, F32),
        scratch_shapes=[pltpu.VMEM((rows, d), BF16)],
        compiler_params=_params(("arbitrary", "arbitrary")),
        name="ffn",
    )(x, shift[:, None, :], scale[:, None, :], gate[:, None, :], g_ffn.reshape(1, 1, d),
      w_in_b, w_in_b, w_out_b, g_final.reshape(1, 1, d), shift_f[:, None, :], scale_f[:, None, :])


ATTN_Q_BLOCK = 1024
ATTN_K_BEAT = 512
PROMPT_ROWS = 512
SAMPLE_BATCH_BLOCK = 8
FFN_TILE = 512
FFN_ROWS = 1024
FFN_ROW_GROUPS = 2


def kernel(x_prompt, x_sample, cache_k, cache_v, c_prompt, c_sample, rel_bias, w_ada, b_ada, w_ada_final,
           b_ada_final, g_mix, g_ffn, g_final, w_in, mlp_ln_g, mlp_ln_b, w_s, b_s, lambda_q1, lambda_k1,
           lambda_q2, lambda_k2, sub_g, w_out, w_ffn_in, w_ffn_out):
    B, S, D = x_prompt.shape
    DB, T, _ = x_sample.shape
    depth = w_in.shape[0]
    past = cache_k.shape[2]
    width = N_HEADS * V_HEAD_DIM
    mlp_chunk = w_s.shape[-1]

    c_all = jnp.concatenate([c_prompt, c_sample], axis=0)
    mod_f = _adaln(c_all, w_ada_final, b_ada_final)
    bias_p = _rel_bias_tiles(rel_bias, ATTN_Q_BLOCK, ATTN_Q_BLOCK + 2 * ATTN_K_BEAT, ATTN_K_BEAT, 0, True, True)
    bias_s = _rel_bias_tiles(rel_bias, T, past + T, past, 0, False, False)

    assert depth == 1, "the final adaLN norm is fused into the single layer's FFN kernel"
    lam_init = 0.8 - 0.6 * math.exp(-0.3 * 0)
    lam_vecs = jnp.stack([lambda_q1[0], lambda_k1[0], lambda_q2[0], lambda_k2[0]])
    subg = sub_g.reshape(1, V_HEAD_DIM)
    mod = _adaln(c_all, w_ada[0], b_ada[0])
    sh1, sc1, gt1, sh2, sc2, gt2 = jnp.split(mod, 6, axis=-1)
    shf, scf = jnp.split(mod_f, 2, axis=-1)
    w_in_b = w_in[0].astype(BF16)
    w_out_b = w_out[0].astype(BF16)
    w_f_in_b = w_ffn_in[0].astype(BF16)
    w_f_out_b = w_ffn_out[0].astype(BF16)
    mixer_w = (g_mix[0], w_in_b, mlp_ln_g[0], mlp_ln_b[0], w_s[0], b_s[0])

    m, qt, kp, kb, vp, vt = _mixer_in(x_prompt, sh1[:B], sc1[:B], *mixer_w, nb_blk=1, r_blk=PROMPT_ROWS,
                                     t_chunk=mlp_chunk, emit_gv=False, v_transposed=True)
    a = _attn_prompt(qt, kb.reshape(B, S, width), vt, bias_p, lam_vecs,
                     subg.reshape(V_HEAD_DIM, 1), bq=ATTN_Q_BLOCK, bk=ATTN_K_BEAT, lam_init=lam_init)
    xp = _mixer_out(x_prompt, m, a.reshape(B * S, width), gt1[:B], w_out_b, nb_blk=1, r_blk=PROMPT_ROWS)
    yp = _ffn(xp, sh2[:B], sc2[:B], gt2[:B], g_ffn[0], w_f_in_b, w_f_out_b, g_final, shf[:B], scf[:B],
              nb_blk=1, r_blk=FFN_ROWS, tf=FFN_TILE)

    m, q, ks, kb, vs, vb, gvs = _mixer_in(x_sample, sh1[B:], sc1[B:], *mixer_w, nb_blk=SAMPLE_BATCH_BLOCK,
                                          r_blk=T, t_chunk=T, emit_gv=True, v_transposed=False)
    a = _attn_sample(q.reshape(DB, T, width), cache_k.reshape(DB, past * N_HEADS, LANES),
                     cache_v.reshape(DB, past * N_HEADS, LANES), kb.reshape(DB, T, width),
                     vb.reshape(DB, T, width), bias_s, lam_vecs, subg, lam_init=lam_init)
    xs = _mixer_out(x_sample, m, a.reshape(DB * T, width), gt1[B:], w_out_b, nb_blk=SAMPLE_BATCH_BLOCK, r_blk=T)
    ys = _ffn(xs, sh2[B:], sc2[B:], gt2[B:], g_ffn[0], w_f_in_b, w_f_out_b, g_final, shf[B:], scf[B:],
              nb_blk=DB, r_blk=T, tf=FFN_TILE)

    head_shape = (N_HEADS, V_HEAD_DIM)
    return (yp, ys, kp.reshape(1, B, S, *head_shape), vp.reshape(1, B, S, *head_shape),
            ks.reshape(1, DB, T, *head_shape), vs.reshape(1, DB, T, *head_shape),
            gvs.reshape(1, DB, T, MLP_GROUPS, MLP_GROUP_DIM))
```
